```python
import jax, jax.numpy as jnp
from jax import lax
import numpy as np

D_MODEL = 2048
BATCH = 4
SEQ = 2048
DEPTH = 4
DEC_BATCH = 128
DEC_SEQ = 1
PAST_LEN = 16384
PAGE_SIZE = 128

N_MIXERS = 2
N_A = (DEPTH + 1) // 2
N_B = DEPTH // 2
N_META = 16
CHUNK = 64
D_FF = 5504
HA_DK = 128
HA_HEADS = D_MODEL // HA_DK
HA_DV = D_MODEL // HA_HEADS
HA_FDIM = HA_HEADS * HA_DK
HB_HEADS = 8
HB_DK = D_MODEL // HB_HEADS
HB_VDIM = 2 * D_MODEL
HB_DV = HB_VDIM // HB_HEADS
ROPE_BASE = 10000.0
EPS = 1e-6
LB_FLOOR = 1e-30

kernel_name = 'hgrn2_retention_macaron_step'


def rms_norm(x, gain=None):
    xf = x.astype(jnp.float32)
    y = xf * lax.rsqrt(jnp.mean(xf * xf, axis=-1, keepdims=True) + EPS)
    if gain is not None:
        y = y * gain.astype(jnp.float32)
    return y.astype(x.dtype)


def swiglu(h, w_gate, w_up, w_down):
    return (jax.nn.silu(h @ w_gate) * (h @ w_up)) @ w_down


def split_heads(a, n_heads):
    b, l, _ = a.shape
    return a.reshape(b, l, n_heads, -1).transpose(0, 2, 1, 3)


def rotary(x, pos):
    half = x.shape[-1] // 2
    inv = ROPE_BASE ** (-jnp.linspace(0.0, 1.0, half, dtype=jnp.float32))
    ang = pos.astype(jnp.float32)[:, None] * inv[None, :]
    cos = jnp.cos(ang)[None, :, None, :]
    sin = jnp.sin(ang)[None, :, None, :]
    xf = x.astype(jnp.float32)
    x1, x2 = xf[..., :half], xf[..., half:]
    return jnp.concatenate([x1 * cos - x2 * sin, x1 * sin + x2 * cos], axis=-1)


def chunk_scan(q, k, v, log_decay, state0, chunk):
    b, h, l, _ = q.shape
    n = l // chunk

    def blocks(a):
        return jnp.moveaxis(a.reshape(b, h, n, chunk, a.shape[-1]), 2, 0)

    causal = jnp.tril(jnp.ones((chunk, chunk), dtype=bool))[:, :, None]
    scalar = log_decay.shape[-1] == 1

    def step(s, inp):
        qc, kc, vc, gc = inp
        cum = jnp.cumsum(gc, axis=2)
        last = cum[:, :, -1:, :]
        inter = jnp.einsum('bhtd,bhdv->bhtv', qc * jnp.exp(cum), s)
        diff = jnp.where(causal, cum[:, :, :, None, :] - cum[:, :, None, :, :], 0.0)
        dec = jnp.where(causal, jnp.exp(diff), 0.0)
        if scalar:
            scores = jnp.einsum('bhtd,bhsd->bhts', qc, kc) * dec[..., 0]
        else:
            scores = jnp.einsum('bhtd,bhsd,bhtsd->bhts', qc, kc, dec)
        out = inter + jnp.einsum('bhts,bhsv->bhtv', scores, vc)
        s_new = (jnp.exp(last[:, :, 0, :])[..., None] * s
                 + jnp.einsum('bhsd,bhsv->bhdv', kc * jnp.exp(last - cum), vc))
        return s_new, out

    s_fin, outs = lax.scan(step, state0, (blocks(q), blocks(k), blocks(v), blocks(log_decay)))
    return jnp.moveaxis(outs, 0, 2).reshape(b, h, l, v.shape[-1]), s_fin


def gated_recurrence(q, k, v, log_decay, state0, pad_front):
    l = q.shape[2]
    total = pad_front + l
    if total <= CHUNK:
        chunk, pad_back = total, 0
    else:
        chunk, pad_back = CHUNK, (-total) % CHUNK

    def pad(a):
        return jnp.pad(a.astype(jnp.float32), ((0, 0), (0, 0), (pad_front, pad_back), (0, 0)))

    out, s = chunk_scan(pad(q), pad(k), pad(v), pad(log_decay), state0.astype(jnp.float32), chunk)
    return out[:, :, pad_front:pad_front + l], s


def hgrn2_mixer(h, state0, lb, wq, wf, wi, wg, wo, gain, pad_front):
    bsz, l, _ = h.shape
    q = split_heads(jax.nn.silu(h @ wq), HA_HEADS)
    z = split_heads((h @ wf).astype(jnp.float32), HA_HEADS)
    lb_h = lb.astype(jnp.float32).reshape(HA_HEADS, HA_DK)[None, :, None, :]
    log_lb = jnp.log(jnp.maximum(lb_h, LB_FLOOR))
    log_f = jnp.logaddexp(log_lb, jnp.log1p(-lb_h) + jax.nn.log_sigmoid(z))
    k = (1.0 - lb_h) * jax.nn.sigmoid(-z)
    v = split_heads(h @ wi, HA_HEADS)
    o, s = gated_recurrence(q, k, v, log_f, state0, pad_front)
    o = rms_norm(o.transpose(0, 2, 1, 3), gain.reshape(HA_HEADS, HA_DV))
    o = o.reshape(bsz, l, D_MODEL).astype(h.dtype) * jax.nn.silu(h @ wg)
    return o @ wo, s


def retention_mixer(h, pos, state0, wq, wk, wv, wg, wo, pad_front):
    bsz, l, _ = h.shape
    q = rotary((h @ wq).reshape(bsz, l, HB_HEADS, HB_DK), pos).transpose(0, 2, 1, 3)
    k = rotary((h @ wk).reshape(bsz, l, HB_HEADS, HB_DK), pos).transpose(0, 2, 1, 3) * (HB_DK ** -0.5)
    v = split_heads(h @ wv, HB_HEADS)
    log_gamma = jnp.log1p(-jnp.exp2(-5.0 - jnp.arange(HB_HEADS, dtype=jnp.float32)))
    log_decay = jnp.broadcast_to(log_gamma[None, :, None, None], (bsz, HB_HEADS, l, 1))
    o, s = gated_recurrence(q, k, v, log_decay, state0, pad_front)
    o = rms_norm(o.transpose(0, 2, 1, 3)).reshape(bsz, l, HB_VDIM).astype(h.dtype)
    o = o * jax.nn.silu(h @ wg)
    return o @ wo, s


def trunk(x, pos, state_a, state_b, pad_front, p):
    new_a, new_b = [], []
    for i in range(DEPTH):
        x = x + 0.5 * swiglu(rms_norm(x, p['norm_ffn'][i, 0]), p['ffn_w_gate'][i, 0],
                             p['ffn_w_up'][i, 0], p['ffn_w_down'][i, 0])
        hn = rms_norm(x, p['norm_mix'][i])
        j = i // N_MIXERS
        if i % N_MIXERS == 0:
            mixed, s = hgrn2_mixer(hn, state_a[j], p['lb'][j], p['hg_wq'][j], p['hg_wf'][j],
                                   p['hg_wi'][j], p['hg_wg'][j], p['hg_wo'][j], p['hg_norm'][j], pad_front)
            new_a.append(s)
        else:
            mixed, s = retention_mixer(hn, pos, state_b[j], p['rt_wq'][j], p['rt_wk'][j],
                                       p['rt_wv'][j], p['rt_wg'][j], p['rt_wo'][j], pad_front)
            new_b.append(s)
        x = x + mixed.astype(x.dtype)
        x = x + 0.5 * swiglu(rms_norm(x, p['norm_ffn'][i, 1]), p['ffn_w_gate'][i, 1],
                             p['ffn_w_up'][i, 1], p['ffn_w_down'][i, 1])
    return rms_norm(x, p['norm_final']), jnp.stack(new_a), jnp.stack(new_b)


def setup_inputs(seed: int = 0) -> dict:
    key = jax.random.key(seed)
    ks = jax.random.split(key, 24)
    D = D_MODEL

    def nrm(k, shape, scale):
        return jax.random.normal(k, shape, jnp.float32) * scale

    return {
        'x_prompt': nrm(ks[0], (BATCH, SEQ, D), 1.0),
        'x_sample': nrm(ks[1], (DEC_BATCH, DEC_SEQ, D), 1.0),
        'state_hgrn': nrm(ks[2], (N_A, DEC_BATCH, HA_HEADS, HA_DK, HA_DV), 0.5),
        'state_ret': nrm(ks[3], (N_B, DEC_BATCH, HB_HEADS, HB_DK, HB_DV), 0.1),
        'meta_tokens': nrm(ks[4], (N_META, D), 1.0),
        'norm_ffn': 1.0 + nrm(ks[5], (DEPTH, 2, D), 0.02),
        'ffn_w_gate': nrm(ks[6], (DEPTH, 2, D, D_FF), D ** -0.5),
        'ffn_w_up': nrm(ks[7], (DEPTH, 2, D, D_FF), D ** -0.5),
        'ffn_w_down': nrm(ks[8], (DEPTH, 2, D_FF, D), D_FF ** -0.5),
        'norm_mix': 1.0 + nrm(ks[9], (DEPTH, D), 0.02),
        'hg_wq': nrm(ks[10], (N_A, D, HA_FDIM), D ** -0.5),
        'hg_wf': nrm(ks[11], (N_A, D, HA_FDIM), D ** -0.5),
        'hg_wi': nrm(ks[12], (N_A, D, D), D ** -0.5),
        'hg_wg': nrm(ks[13], (N_A, D, D), D ** -0.5),
        'hg_wo': nrm(ks[14], (N_A, D, D), D ** -0.5),
        'hg_norm': 1.0 + nrm(ks[15], (N_A, D), 0.02),
        'hg_lb_logits': nrm(ks[16], (N_A, HA_FDIM), 1.0),
        'rt_wq': nrm(ks[17], (N_B, D, D), D ** -0.5),
        'rt_wk': nrm(ks[18], (N_B, D, D), D ** -0.5),
        'rt_wv': nrm(ks[19], (N_B, D, HB_VDIM), D ** -0.5),
        'rt_wg': nrm(ks[20], (N_B, D, HB_VDIM), D ** -0.5),
        'rt_wo': nrm(ks[21], (N_B, HB_VDIM, D), HB_VDIM ** -0.5),
        'norm_final': 1.0 + nrm(ks[22], (D,), 0.02),
    }


def reference(x_prompt, x_sample, state_hgrn, state_ret, meta_tokens, norm_ffn, ffn_w_gate, ffn_w_up,
              ffn_w_down, norm_mix, hg_wq, hg_wf, hg_wi, hg_wg, hg_wo, hg_norm, hg_lb_logits,
              rt_wq, rt_wk, rt_wv, rt_wg, rt_wo, norm_final):
    probs = jax.nn.softmax(hg_lb_logits.astype(jnp.float32), axis=0)
    lower_bounds = jnp.cumsum(probs, axis=0) - probs[0:1]
    params = {
        'norm_ffn': norm_ffn, 'ffn_w_gate': ffn_w_gate, 'ffn_w_up': ffn_w_up, 'ffn_w_down': ffn_w_down,
        'norm_mix': norm_mix, 'lb': lower_bounds, 'hg_wq': hg_wq, 'hg_wf': hg_wf, 'hg_wi': hg_wi,
        'hg_wg': hg_wg, 'hg_wo': hg_wo, 'hg_norm': hg_norm, 'rt_wq': rt_wq, 'rt_wk': rt_wk,
        'rt_wv': rt_wv, 'rt_wg': rt_wg, 'rt_wo': rt_wo, 'norm_final': norm_final,
    }
    bsz, seq, _ = x_prompt.shape
    meta = jnp.broadcast_to(meta_tokens.astype(x_prompt.dtype)[None], (bsz, N_META, D_MODEL))
    h_prompt = jnp.concatenate([meta, x_prompt], axis=1)
    pos_prompt = jnp.arange(N_META + seq, dtype=jnp.int32)
    zero_a = jnp.zeros((N_A, bsz, HA_HEADS, HA_DK, HA_DV), jnp.float32)
    zero_b = jnp.zeros((N_B, bsz, HB_HEADS, HB_DK, HB_DV), jnp.float32)
    y_full, hgrn_p, ret_p = trunk(h_prompt, pos_prompt, zero_a, zero_b, CHUNK - N_META, params)
    y_prompt = y_full[:, N_META:]
    pos_sample = PAST_LEN + jnp.arange(x_sample.shape[1], dtype=jnp.int32)
    y_sample, hgrn_s, ret_s = trunk(x_sample, pos_sample, state_hgrn, state_ret, 0, params)
    return (y_prompt, y_sample, hgrn_p, hgrn_s, ret_p, ret_s)
```

```python
import functools
import math

import numpy as np
import jax
import jax.numpy as jnp
from jax import lax
from jax.experimental import pallas as pl
from jax.experimental.pallas import tpu as pltpu

F32 = jnp.float32
BF16 = jnp.bfloat16

EPS = 1e-6
LB_FLOOR = 1e-30
ROPE_BASE = 10000.0
N_META = 16
PAST_LEN = 16384
HA_DK = 128
HB_HEADS = 8

LANE = 128
CHUNK = 128
FF_TILE = 512
COL_TILE = 512
ROW_TILES = (768, 512, 256, 128)
VMEM_LIMIT = 56 * 1024 * 1024

_NT = (((1,), (1,)), ((), ()))
_TN = (((0,), (0,)), ((), ()))


def _round_up(a, m):
    return (a + m - 1) // m * m


def _cparams(n_axes):
    return pltpu.CompilerParams(dimension_semantics=("arbitrary",) * n_axes,
                                vmem_limit_bytes=VMEM_LIMIT)


def _rms(x):
    return x * lax.rsqrt(jnp.mean(x * x, axis=-1, keepdims=True) + EPS)


def _sigmoid(x):
    return 1.0 / (1.0 + jnp.exp(-x))


def _silu(x):
    return x * _sigmoid(x)


def _log_sigmoid(x):
    return -(jnp.maximum(-x, 0.0) + jnp.log1p(jnp.exp(-jnp.abs(x))))


def _logaddexp(a, b):
    return jnp.maximum(a, b) + jnp.log1p(jnp.exp(-jnp.abs(a - b)))


def _ffn_body(x_ref, gain_ref, wg_ref, wu_ref, wd_ref, fgain_ref, o_ref, h_ref, *, final_norm):
    j = pl.program_id(1)

    @pl.when(j == 0)
    def _():
        x = x_ref[...]
        h_ref[...] = (_rms(x) * gain_ref[...]).astype(BF16)
        o_ref[...] = x

    h = h_ref[...]
    g = jnp.dot(h, wg_ref[...], preferred_element_type=F32)
    u = jnp.dot(h, wu_ref[...], preferred_element_type=F32)
    a = (0.5 * _silu(g) * u).astype(BF16)
    o_ref[...] += jnp.dot(a, wd_ref[...], preferred_element_type=F32)

    if final_norm:
        @pl.when(j == pl.num_programs(1) - 1)
        def _():
            o_ref[...] = _rms(o_ref[...]) * fgain_ref[...]


def _ffn(x, gain, wg, wu, wd, layer, idx, fgain, tm):
    R, D = x.shape
    fpad = wg.shape[-1]
    final_norm = fgain is not None
    if fgain is None:
        fgain = gain
    return pl.pallas_call(
        functools.partial(_ffn_body, final_norm=final_norm),
        grid=(R // tm, fpad // FF_TILE),
        in_specs=[
            pl.BlockSpec((tm, D), lambda i, j: (i, 0)),
            pl.BlockSpec((1, D), lambda i, j: (0, 0)),
            pl.BlockSpec((None, None, D, FF_TILE), lambda i, j: (layer, idx, 0, j)),
            pl.BlockSpec((None, None, D, FF_TILE), lambda i, j: (layer, idx, 0, j)),
            pl.BlockSpec((None, None, FF_TILE, D), lambda i, j: (layer, idx, j, 0)),
            pl.BlockSpec((1, D), lambda i, j: (0, 0)),
        ],
        out_specs=pl.BlockSpec((tm, D), lambda i, j: (i, 0)),
        out_shape=jax.ShapeDtypeStruct((R, D), F32),
        scratch_shapes=[pltpu.VMEM((tm, D), BF16)],
        compiler_params=_cparams(2),
        name="ffn",
    )(x, gain.reshape(1, D), wg, wu, wd, fgain.reshape(1, D))


def _proj_body(x_ref, gain_ref, w_ref, o_ref, h_ref, *, n_silu):
    j = pl.program_id(1)

    @pl.when(j == 0)
    def _():
        h_ref[...] = (_rms(x_ref[...]) * gain_ref[...]).astype(BF16)

    y = jnp.dot(h_ref[...], w_ref[...], preferred_element_type=F32)

    @pl.when(j < n_silu)
    def _():
        o_ref[...] = _silu(y)

    @pl.when(j >= n_silu)
    def _():
        o_ref[...] = y


def _proj(x, gain, w, n_silu_cols, tm):
    R, D = x.shape
    N = w.shape[1]
    return pl.pallas_call(
        functools.partial(_proj_body, n_silu=n_silu_cols // COL_TILE),
        grid=(R // tm, N // COL_TILE),
        in_specs=[
            pl.BlockSpec((tm, D), lambda i, j: (i, 0)),
            pl.BlockSpec((1, D), lambda i, j: (0, 0)),
            pl.BlockSpec((D, COL_TILE), lambda i, j: (0, j)),
        ],
        out_specs=pl.BlockSpec((tm, COL_TILE), lambda i, j: (i, j)),
        out_shape=jax.ShapeDtypeStruct((R, N), F32),
        scratch_shapes=[pltpu.VMEM((tm, D), BF16)],
        compiler_params=_cparams(2),
        name="proj",
    )(x, gain.reshape(1, D), w)


def _outproj_body(a_ref, w_ref, x_ref, o_ref):
    o_ref[...] = x_ref[...] + jnp.dot(a_ref[...], w_ref[...], preferred_element_type=F32)


def _outproj(a, w, x, tm):
    R, K = a.shape
    D = w.shape[1]
    return pl.pallas_call(
        _outproj_body,
        grid=(R // tm, D // COL_TILE),
        in_specs=[
            pl.BlockSpec((tm, K), lambda i, j: (i, 0)),
            pl.BlockSpec((K, COL_TILE), lambda i, j: (0, j)),
            pl.BlockSpec((tm, COL_TILE), lambda i, j: (i, j)),
        ],
        out_specs=pl.BlockSpec((tm, COL_TILE), lambda i, j: (i, j)),
        out_shape=jax.ShapeDtypeStruct((R, D), F32),
        compiler_params=_cparams(2),
        name="outproj",
    )(a, w, x)


def _hgrn_tables(C):
    n_lvl = int(math.log2(C))
    assert 1 << n_lvl == C
    r = np.arange(C)
    msum = np.zeros((n_lvl + 2, C, C), np.float32)
    upper = np.zeros((n_lvl, C, 1), np.float32)
    same = np.zeros((n_lvl + 1, C, C), np.float32)
    for l in range(n_lvl):
        half = 1 << l
        blk = 2 * half
        m = r - r % blk + half - 1
        up = (r % blk) >= half
        u = r[None, :]
        msum[l] = np.where(up[:, None], (u > m[:, None]) & (u <= r[:, None]),
                           (u > r[:, None]) & (u <= m[:, None]))
        upper[l, :, 0] = up
        same[l] = (r[:, None] // blk) == (r[None, :] // blk)
    same[n_lvl] = np.eye(C)
    msum[n_lvl] = r[None, :] <= r[:, None]
    msum[n_lvl + 1] = r[None, :] > r[:, None]
    upper = np.broadcast_to(upper, (n_lvl, C, LANE)).copy()
    return (jnp.asarray(msum.reshape(-1, C), BF16), jnp.asarray(upper, F32), jnp.asarray(same, F32))


def _lower_bound(logits, layer):
    e = jnp.exp(logits - jnp.max(logits, axis=0, keepdims=True))
    probs = e / jnp.sum(e, axis=0, keepdims=True)
    cs = probs[0:1]
    for i in range(1, layer + 1):
        cs = cs + probs[i:i + 1]
    return cs - probs[0:1]


def _hgrn_gates(z, lb):
    log_f = _logaddexp(jnp.log(jnp.maximum(lb, LB_FLOOR)), jnp.log1p(-lb) + _log_sigmoid(z))
    k = (1.0 - lb) * _sigmoid(-z)
    return log_f, k


def _split3(g):
    hi = g.astype(BF16)
    r1 = g - hi.astype(F32)
    mid = r1.astype(BF16)
    lo = (r1 - mid.astype(F32)).astype(BF16)
    return hi, mid, lo


def _hgrn_prompt_body(q_ref, z_ref, v_ref, gate_ref, logit_ref, hgain_ref, msum_ref, upper_ref,
                      same_ref, og_ref, so_ref, st_ref, stm_ref, *, layer, n_chunks):
    C = CHUNK
    n_lvl = upper_ref.shape[0]
    s = pl.program_id(1)
    is_meta = s == 0
    chunk = (s - 1) % n_chunks

    @pl.when(is_meta)
    def _():
        st_ref[...] = jnp.zeros_like(st_ref)

    @pl.when((s > 0) & (chunk == 0))
    def _():
        st_ref[...] = stm_ref[...]

    lb = _lower_bound(logit_ref[...], layer)
    q = q_ref[...]
    v = v_ref[...].astype(BF16)
    g, k = _hgrn_gates(z_ref[...], lb)
    n_valid = jnp.where(is_meta, N_META, C)
    valid = (lax.broadcasted_iota(jnp.int32, (C, 1), 0) < n_valid).astype(F32)
    g = g * valid
    k = k * valid

    hi, mid, lo = _split3(g)
    e3 = jnp.dot(msum_ref[...], jnp.concatenate([hi, mid, lo], axis=1), preferred_element_type=F32)
    x = jnp.exp(e3[:, :LANE] + e3[:, LANE:2 * LANE] + e3[:, 2 * LANE:])

    scores = lax.dot_general(q.astype(BF16), k.astype(BF16), _NT, preferred_element_type=F32) * same_ref[n_lvl]
    for l in range(n_lvl):
        xl = x[l * C:(l + 1) * C]
        up = upper_ref[l]
        ql = (q * xl * up).astype(BF16)
        kl = (k * xl * (1.0 - up)).astype(BF16)
        scores = scores + lax.dot_general(ql, kl, _NT, preferred_element_type=F32) * same_ref[l]

    xc = x[n_lvl * C:(n_lvl + 1) * C]
    xt = x[(n_lvl + 1) * C:]
    st = st_ref[...]
    o = lax.dot_general((q * xc).astype(BF16), st.astype(BF16), _NT, preferred_element_type=F32)
    o = o + jnp.dot(scores.astype(BF16), v, preferred_element_type=F32)
    st_new = st * xc[C - 1:C, :] + lax.dot_general(v, (k * xt).astype(BF16), _TN, preferred_element_type=F32)
    st_ref[...] = st_new

    og_ref[...] = (_rms(o) * hgain_ref[...] * gate_ref[...]).astype(BF16)

    @pl.when(is_meta)
    def _():
        stm_ref[...] = st_new

    @pl.when((s > 0) & (chunk == n_chunks - 1))
    def _():
        so_ref[...] = st_new.T


def _hgrn_prompt(p, logits, hgain, layer, n_batch, seq, meta_blk, tables):
    R = p.shape[0]
    D = p.shape[1] // 4
    H = D // HA_DK
    C = CHUNK
    n_chunks = seq // C
    msum, upper, same = tables

    def rows(s):
        return jnp.where(s == 0, meta_blk, s - 1)

    def sec(k):
        return pl.BlockSpec((C, LANE), lambda h, s: (rows(s), k * H + h))

    return pl.pallas_call(
        functools.partial(_hgrn_prompt_body, layer=layer, n_chunks=n_chunks),
        grid=(H, 1 + n_batch * n_chunks),
        in_specs=[
            sec(0), sec(2), sec(3), sec(1),
            pl.BlockSpec((logits.shape[0], LANE), lambda h, s: (0, h)),
            pl.BlockSpec((1, LANE), lambda h, s: (0, h)),
            pl.BlockSpec(msum.shape, lambda h, s: (0, 0)),
            pl.BlockSpec(upper.shape, lambda h, s: (0, 0, 0)),
            pl.BlockSpec(same.shape, lambda h, s: (0, 0, 0)),
        ],
        out_specs=[
            pl.BlockSpec((C, LANE), lambda h, s: (rows(s), h)),
            pl.BlockSpec((None, None, HA_DK, LANE), lambda h, s: (jnp.maximum(s - 1, 0) // n_chunks, h, 0, 0)),
        ],
        out_shape=[
            jax.ShapeDtypeStruct((R, D), BF16),
            jax.ShapeDtypeStruct((n_batch, H, HA_DK, LANE), F32),
        ],
        scratch_shapes=[pltpu.VMEM((LANE, HA_DK), F32), pltpu.VMEM((LANE, HA_DK), F32)],
        compiler_params=_cparams(2),
        name="hgrn_prompt",
    )(p, p, p, p, logits, hgain.reshape(1, D), msum, upper, same)


def _col_bcast(x):
    bs = x.shape[0]
    return jnp.concatenate([x] * (LANE // bs), axis=0).T


def _hgrn_sample_body(q_ref, z_ref, v_ref, gate_ref, logit_ref, hgain_ref, s_ref, og_in_ref, sn_in_ref,
                      og_ref, sn_ref, o_scr, *, layer, bs):
    del og_in_ref, sn_in_ref
    i = pl.program_id(1)

    @pl.when(i == 0)
    def _():
        og_ref[...] = jnp.zeros_like(og_ref)

    rows = pl.ds(pl.multiple_of(i * bs, bs), bs)
    lb = _lower_bound(logit_ref[...], layer)
    g, k = _hgrn_gates(z_ref[rows, :], lb)
    v = v_ref[rows, :]
    ft = _col_bcast(jnp.exp(g))
    kt = _col_bcast(k)
    qt = _col_bcast(q_ref[rows, :])
    for b in range(bs):
        sn = ft[:, b:b + 1] * s_ref[b] + kt[:, b:b + 1] * v[b:b + 1, :]
        sn_ref[b] = sn
        o_scr[b:b + 1, :] = jnp.sum(qt[:, b:b + 1] * sn, axis=0, keepdims=True)
    og_ref[rows, :] = (_rms(o_scr[...]) * hgain_ref[...] * gate_ref[rows, :]).astype(BF16)


def _hgrn_sample(p, logits, hgain, state, og, new_state, layer, n_sample, sample_blk):
    D = p.shape[1] // 4
    H = D // HA_DK
    bs = 16 if n_sample % 16 == 0 else 8
    assert n_sample % bs == 0 and n_sample <= LANE
    sp = LANE

    def sec(k):
        return pl.BlockSpec((sp, LANE), lambda h, i: (sample_blk, k * H + h))

    state_spec = pl.BlockSpec((None, bs, None, HA_DK, LANE), lambda h, i: (layer, i, h, 0, 0))
    any_spec = pl.BlockSpec(memory_space=pl.ANY)
    args = [p, p, p, p, logits, hgain.reshape(1, D), state, og]
    in_specs = [sec(0), sec(2), sec(3), sec(1),
                pl.BlockSpec((logits.shape[0], LANE), lambda h, i: (0, h)),
                pl.BlockSpec((1, LANE), lambda h, i: (0, h)),
                state_spec, any_spec]
    aliases = {7: 0}
    if new_state is None:
        def body(q, z, v, gt, lg, hg, s, og_in, og_out, sn_out, o_scr):
            _hgrn_sample_body(q, z, v, gt, lg, hg, s, og_in, None, og_out, sn_out, o_scr, layer=layer, bs=bs)
    else:
        args.append(new_state)
        in_specs.append(any_spec)
        aliases[8] = 1
        body = functools.partial(_hgrn_sample_body, layer=layer, bs=bs)
    return pl.pallas_call(
        body,
        grid=(H, n_sample // bs),
        in_specs=in_specs,
        out_specs=[pl.BlockSpec((sp, LANE), lambda h, i: (sample_blk, h)), state_spec],
        out_shape=[jax.ShapeDtypeStruct(og.shape, og.dtype), jax.ShapeDtypeStruct(state.shape, F32)],
        scratch_shapes=[pltpu.VMEM((bs, LANE), F32)],
        input_output_aliases=aliases,
        compiler_params=_cparams(2),
        name="hgrn_sample",
    )(*args)


def _rotary(x, cos, sin):
    half = x.shape[-1] // 2
    x1, x2 = x[:, :half], x[:, half:]
    return jnp.concatenate([x1 * cos - x2 * sin, x1 * sin + x2 * cos], axis=-1)


def _ret_prompt_body(lg_ref, q_ref, k_ref, v_ref, gate_ref, cos_ref, sin_ref, og_ref, so_ref,
                     s_ref, sm_ref, *, n_chunks):
    C = CHUNK
    h = pl.program_id(0)
    s = pl.program_id(1)
    is_meta = s == 0
    chunk = (s - 1) % n_chunks

    @pl.when(is_meta)
    def _():
        s_ref[...] = jnp.zeros_like(s_ref)

    @pl.when((s > 0) & (chunk == 0))
    def _():
        s_ref[...] = sm_ref[...]

    lg = lg_ref[h]
    cos, sin = cos_ref[...], sin_ref[...]
    dk = q_ref.shape[-1]
    n_valid = jnp.where(is_meta, N_META, C)
    t_col = lax.broadcasted_iota(jnp.int32, (C, 1), 0)
    t_row = lax.broadcasted_iota(jnp.int32, (1, C), 1)
    n_col = jnp.minimum(t_col + 1, n_valid).astype(F32)
    n_row = jnp.minimum(t_row + 1, n_valid).astype(F32)
    n_last = n_valid.astype(F32)
    q = _rotary(q_ref[...], cos, sin)
    k = _rotary(k_ref[...], cos, sin) * (dk ** -0.5) * (t_col < n_valid).astype(F32)
    v = v_ref[...].astype(BF16)

    decay = jnp.where(t_col >= t_row, jnp.exp(lg * (n_col - n_row)), 0.0)
    scores = lax.dot_general(q.astype(BF16), k.astype(BF16), _NT, preferred_element_type=F32) * decay
    st = s_ref[...]
    o = jnp.dot((q * jnp.exp(lg * n_col)).astype(BF16), st.astype(BF16), preferred_element_type=F32)
    o = o + jnp.dot(scores.astype(BF16), v, preferred_element_type=F32)
    s_new = jnp.exp(lg * n_last) * st + lax.dot_general(
        (k * jnp.exp(lg * (n_last - n_col))).astype(BF16), v, _TN, preferred_element_type=F32)
    s_ref[...] = s_new

    og_ref[...] = (_rms(o) * gate_ref[...]).astype(BF16)

    @pl.when(is_meta)
    def _():
        sm_ref[...] = s_new

    @pl.when((s > 0) & (chunk == n_chunks - 1))
    def _():
        so_ref[...] = s_new


def _ret_prompt(p, log_gamma, cos_tab, sin_tab, n_batch, seq, meta_blk):
    R = p.shape[0]
    D = p.shape[1] // 6
    H = HB_HEADS
    dk = D // H
    dv = 2 * D // H
    C = CHUNK
    n_chunks = seq // C

    def rows(s):
        return jnp.where(s == 0, meta_blk, s - 1)

    def tab_rows(s):
        return jnp.where(s == 0, 0, 1 + (s - 1) % n_chunks)

    qk_off = 2 * D // dk
    v_off = 4 * D // dv
    return pl.pallas_call(
        functools.partial(_ret_prompt_body, n_chunks=n_chunks),
        grid=(H, 1 + n_batch * n_chunks),
        in_specs=[
            pl.BlockSpec(memory_space=pltpu.SMEM),
            pl.BlockSpec((C, dk), lambda h, s: (rows(s), qk_off + h)),
            pl.BlockSpec((C, dk), lambda h, s: (rows(s), qk_off + H + h)),
            pl.BlockSpec((C, dv), lambda h, s: (rows(s), v_off + h)),
            pl.BlockSpec((C, dv), lambda h, s: (rows(s), h)),
            pl.BlockSpec((C, dk // 2), lambda h, s: (tab_rows(s), 0)),
            pl.BlockSpec((C, dk // 2), lambda h, s: (tab_rows(s), 0)),
        ],
        out_specs=[
            pl.BlockSpec((C, dv), lambda h, s: (rows(s), h)),
            pl.BlockSpec((None, None, dk, dv), lambda h, s: (jnp.maximum(s - 1, 0) // n_chunks, h, 0, 0)),
        ],
        out_shape=[
            jax.ShapeDtypeStruct((R, 2 * D), BF16),
            jax.ShapeDtypeStruct((n_batch, H, dk, dv), F32),
        ],
        scratch_shapes=[pltpu.VMEM((dk, dv), F32), pltpu.VMEM((dk, dv), F32)],
        compiler_params=_cparams(2),
        name="ret_prompt",
    )(log_gamma, p, p, p, p, cos_tab, sin_tab)


def _ret_sample_body(lg_ref, q_ref, k_ref, v_ref, gate_ref, cos_ref, sin_ref, s_ref, og_in_ref, sn_in_ref,
                     og_ref, sn_ref, o_scr, *, bs):
    del og_in_ref, sn_in_ref
    h = pl.program_id(0)
    i = pl.program_id(1)

    @pl.when(i == 0)
    def _():
        og_ref[...] = jnp.zeros_like(og_ref)

    rows = pl.ds(pl.multiple_of(i * bs, bs), bs)
    gamma = jnp.exp(lg_ref[h])
    cos, sin = cos_ref[...], sin_ref[...]
    dk = q_ref.shape[-1]
    q = _rotary(q_ref[rows, :], cos, sin)
    k = _rotary(k_ref[rows, :], cos, sin) * (dk ** -0.5)
    v = v_ref[rows, :]
    qt = jnp.concatenate([_col_bcast(q[:, :LANE]), _col_bcast(q[:, LANE:])], axis=0)
    kt = jnp.concatenate([_col_bcast(k[:, :LANE]), _col_bcast(k[:, LANE:])], axis=0)
    for b in range(bs):
        sn = gamma * s_ref[b] + kt[:, b:b + 1] * v[b:b + 1, :]
        sn_ref[b] = sn
        o_scr[b:b + 1, :] = jnp.sum(qt[:, b:b + 1] * sn, axis=0, keepdims=True)
    og_ref[rows, :] = (_rms(o_scr[...]) * gate_ref[rows, :]).astype(BF16)


def _ret_sample(p, log_gamma, cos_s, sin_s, state, og, new_state, layer, n_sample, sample_blk):
    D = p.shape[1] // 6
    H = HB_HEADS
    dk = D // H
    dv = 2 * D // H
    bs = 8
    assert n_sample % bs == 0 and n_sample <= LANE and dk == 2 * LANE
    sp = LANE
    qk_off = 2 * D // dk
    v_off = 4 * D // dv
    state_spec = pl.BlockSpec((None, bs, None, dk, dv), lambda h, i: (layer, i, h, 0, 0))
    any_spec = pl.BlockSpec(memory_space=pl.ANY)
    args = [log_gamma, p, p, p, p, cos_s, sin_s, state, og]
    in_specs = [
        pl.BlockSpec(memory_space=pltpu.SMEM),
        pl.BlockSpec((sp, dk), lambda h, i: (sample_blk, qk_off + h)),
        pl.BlockSpec((sp, dk), lambda h, i: (sample_blk, qk_off + H + h)),
        pl.BlockSpec((sp, dv), lambda h, i: (sample_blk, v_off + h)),
        pl.BlockSpec((sp, dv), lambda h, i: (sample_blk, h)),
        pl.BlockSpec((1, dk // 2), lambda h, i: (0, 0)),
        pl.BlockSpec((1, dk // 2), lambda h, i: (0, 0)),
        state_spec, any_spec]
    aliases = {8: 0}
    if new_state is None:
        def body(lg, q, k, v, gt, cs, sn_, s, og_in, og_out, sn_out, o_scr):
            _ret_sample_body(lg, q, k, v, gt, cs, sn_, s, og_in, None, og_out, sn_out, o_scr, bs=bs)
    else:
        args.append(new_state)
        in_specs.append(any_spec)
        aliases[9] = 1
        body = functools.partial(_ret_sample_body, bs=bs)
    return pl.pallas_call(
        body,
        grid=(H, n_sample // bs),
        in_specs=in_specs,
        out_specs=[pl.BlockSpec((sp, dv), lambda h, i: (sample_blk, h)), state_spec],
        out_shape=[jax.ShapeDtypeStruct(og.shape, og.dtype), jax.ShapeDtypeStruct(state.shape, F32)],
        scratch_shapes=[pltpu.VMEM((bs, dv), F32)],
        input_output_aliases=aliases,
        compiler_params=_cparams(2),
        name="ret_sample",
    )(*args)


def _rope_tables(pos, half):
    inv = ROPE_BASE ** (-jnp.linspace(0.0, 1.0, half, dtype=F32))
    ang = pos.astype(F32)[:, None] * inv[None, :]
    return jnp.cos(ang), jnp.sin(ang)


def kernel(x_prompt, x_sample, state_hgrn, state_ret, meta_tokens, norm_ffn, ffn_w_gate, ffn_w_up, ffn_w_down, norm_mix, hg_wq, hg_wf, hg_wi, hg_wg, hg_wo, hg_norm, hg_lb_logits, rt_wq, rt_wk, rt_wv, rt_wg, rt_wo, norm_final):
    B, L, D = x_prompt.shape
    NS = x_sample.shape[0]
    depth = norm_mix.shape[0]
    n_meta = meta_tokens.shape[0]
    assert n_meta == N_META and x_sample.shape[1] == 1 and L % CHUNK == 0 and NS <= LANE
    C = CHUNK

    tp = B * L
    sp = LANE
    rows0 = tp + sp + C
    tm = next(t for t in ROW_TILES if _round_up(rows0, t) - rows0 < LANE) if rows0 >= ROW_TILES[0] else rows0
    R = _round_up(rows0, tm)
    x = jnp.concatenate([
        x_prompt.reshape(tp, D), x_sample.reshape(NS, D), jnp.zeros((sp - NS, D), F32),
        meta_tokens.astype(F32), jnp.zeros((R - tp - sp - n_meta, D), F32)], axis=0)
    sample_blk = tp // LANE
    meta_blk = (tp + sp) // C

    d_ff = ffn_w_gate.shape[-1]
    fpad = _round_up(d_ff, FF_TILE) - d_ff
    wg = jnp.pad(ffn_w_gate.astype(BF16), ((0, 0), (0, 0), (0, 0), (0, fpad)))
    wu = jnp.pad(ffn_w_up.astype(BF16), ((0, 0), (0, 0), (0, 0), (0, fpad)))
    wd = jnp.pad(ffn_w_down.astype(BF16), ((0, 0), (0, 0), (0, fpad), (0, 0)))

    half = D // HB_HEADS // 2
    cos_p, sin_p = _rope_tables(jnp.arange(n_meta + L, dtype=jnp.int32), half)
    cos_s, sin_s = _rope_tables(PAST_LEN + jnp.arange(1, dtype=jnp.int32), half)

    def chunk_table(t):
        meta = jnp.concatenate([t[:n_meta], jnp.zeros((C - n_meta, half), F32)], axis=0)
        return jnp.concatenate([meta, t[n_meta:]], axis=0)

    cos_tab, sin_tab = chunk_table(cos_p), chunk_table(sin_p)
    log_gamma = jnp.log1p(-jnp.exp2(-5.0 - jnp.arange(HB_HEADS, dtype=F32)))
    tables = _hgrn_tables(C)

    new_hgrn_p, new_ret_p = [], []
    new_hgrn_s = new_ret_s = None
    for i in range(depth):
        x = _ffn(x, norm_ffn[i, 0], wg, wu, wd, i, 0, None, tm)
        j = i // 2
        if i % 2 == 0:
            w_in = jnp.concatenate([hg_wq[j].astype(BF16), hg_wg[j].astype(BF16),
                                    hg_wf[j].astype(BF16), hg_wi[j].astype(BF16)], axis=1)
            p = _proj(x, norm_mix[i], w_in, 2 * D, tm)
            og, s_p = _hgrn_prompt(p, hg_lb_logits, hg_norm[j], j, B, L, meta_blk, tables)
            og, new_hgrn_s = _hgrn_sample(p, hg_lb_logits, hg_norm[j], state_hgrn, og, new_hgrn_s,
                                          j, NS, sample_blk)
            new_hgrn_p.append(s_p)
            x = _outproj(og, hg_wo[j].astype(BF16), x, tm)
        else:
            w_in = jnp.concatenate([rt_wg[j].astype(BF16), rt_wq[j].astype(BF16),
                                    rt_wk[j].astype(BF16), rt_wv[j].astype(BF16)], axis=1)
            p = _proj(x, norm_mix[i], w_in, 2 * D, tm)
            og, s_p = _ret_prompt(p, log_gamma, cos_tab, sin_tab, B, L, meta_blk)
            og, new_ret_s = _ret_sample(p, log_gamma, cos_s, sin_s, state_ret, og, new_ret_s,
                                        j, NS, sample_blk)
            new_ret_p.append(s_p)
            x = _outproj(og, rt_wo[j].astype(BF16), x, tm)
        x = _ffn(x, norm_ffn[i, 1], wg, wu, wd, i, 1, norm_final if i == depth - 1 else None, tm)

    y_prompt = x[:tp].reshape(B, L, D)
    y_sample = x[tp:tp + NS].reshape(NS, 1, D)
    return (y_prompt, y_sample, jnp.stack(new_hgrn_p), new_hgrn_s, jnp.stack(new_ret_p), new_ret_s)
```

```python
import functools
import math

import numpy as np
import jax
import jax.numpy as jnp
from jax import lax
from jax.experimental import pallas as pl
from jax.experimental.pallas import tpu as pltpu

F32 = jnp.float32
BF16 = jnp.bfloat16

EPS = 1e-6
LB_FLOOR = 1e-30
ROPE_BASE = 10000.0
N_META = 16
PAST_LEN = 16384
HA_DK = 128
HB_HEADS = 8

LANE = 128
CHUNK = 128
FF_TILE = 512
PROJ_TILE = 1024
OUT_TILE = 512
ROW_TILES = (768, 512, 256, 128)
VMEM_LIMIT = 56 * 1024 * 1024

_NT = (((1,), (1,)), ((), ()))
_TN = (((0,), (0,)), ((), ()))


def _round_up(a, m):
    return (a + m - 1) // m * m


def _cparams(n_axes):
    return pltpu.CompilerParams(dimension_semantics=("arbitrary",) * n_axes,
                                vmem_limit_bytes=VMEM_LIMIT)


def _rms(x):
    return x * lax.rsqrt(jnp.mean(x * x, axis=-1, keepdims=True) + EPS)


def _sigmoid(x):
    return 1.0 / (1.0 + jnp.exp(-x))


def _silu(x):
    return x * _sigmoid(x)


def _log_sigmoid(x):
    return -(jnp.maximum(-x, 0.0) + jnp.log1p(jnp.exp(-jnp.abs(x))))


def _logaddexp(a, b):
    return jnp.maximum(a, b) + jnp.log1p(jnp.exp(-jnp.abs(a - b)))


def _cast_body(w_ref, o_ref, *, axis, n_valid_last):
    j = pl.program_id(2)
    w = w_ref[...]
    idx = lax.broadcasted_iota(jnp.int32, w.shape, axis)
    limit = jnp.where(j == pl.num_programs(2) - 1, n_valid_last, w.shape[axis])
    o_ref[...] = jnp.where(idx < limit, w, 0.0).astype(BF16)


def _cast_ffn_in(w):
    n_l, n_i, D, F = w.shape
    nj = pl.cdiv(F, FF_TILE)
    return pl.pallas_call(
        functools.partial(_cast_body, axis=1, n_valid_last=F - (nj - 1) * FF_TILE),
        grid=(n_l, n_i, nj),
        in_specs=[pl.BlockSpec((None, None, D, FF_TILE), lambda l, i, j: (l, i, 0, j))],
        out_specs=pl.BlockSpec((None, None, None, D, FF_TILE), lambda l, i, j: (l, i, j, 0, 0)),
        out_shape=jax.ShapeDtypeStruct((n_l, n_i, nj, D, FF_TILE), BF16),
        compiler_params=_cparams(3),
        name="cast_ffn_in",
    )(w)


def _cast_ffn_out(w):
    n_l, n_i, F, D = w.shape
    nj = pl.cdiv(F, FF_TILE)
    return pl.pallas_call(
        functools.partial(_cast_body, axis=0, n_valid_last=F - (nj - 1) * FF_TILE),
        grid=(n_l, n_i, nj),
        in_specs=[pl.BlockSpec((None, None, FF_TILE, D), lambda l, i, j: (l, i, j, 0))],
        out_specs=pl.BlockSpec((None, None, FF_TILE, D), lambda l, i, j: (l, i, j, 0)),
        out_shape=jax.ShapeDtypeStruct((n_l, n_i, nj * FF_TILE, D), BF16),
        compiler_params=_cparams(3),
        name="cast_ffn_out",
    )(w)


def _ffn_body(x_ref, gain_ref, wg_ref, wu_ref, wd_ref, fgain_ref, o_ref, h_ref, *, final_norm):
    j = pl.program_id(1)

    @pl.when(j == 0)
    def _():
        x = x_ref[...]
        h_ref[...] = (_rms(x) * gain_ref[...]).astype(BF16)
        o_ref[...] = x

    h = h_ref[...]
    g = jnp.dot(h, wg_ref[...], preferred_element_type=F32)
    u = jnp.dot(h, wu_ref[...], preferred_element_type=F32)
    a = (0.5 * _silu(g) * u).astype(BF16)
    o_ref[...] += jnp.dot(a, wd_ref[...], preferred_element_type=F32)

    if final_norm:
        @pl.when(j == pl.num_programs(1) - 1)
        def _():
            o_ref[...] = _rms(o_ref[...]) * fgain_ref[...]


def _ffn(x, gain, wg, wu, wd, layer, idx, fgain, tm):
    R, D = x.shape
    nj = wg.shape[2]
    final_norm = fgain is not None
    if fgain is None:
        fgain = gain
    return pl.pallas_call(
        functools.partial(_ffn_body, final_norm=final_norm),
        grid=(R // tm, nj),
        in_specs=[
            pl.BlockSpec((tm, D), lambda i, j: (i, 0)),
            pl.BlockSpec((1, D), lambda i, j: (0, 0)),
            pl.BlockSpec((None, None, None, D, FF_TILE), lambda i, j: (layer, idx, j, 0, 0)),
            pl.BlockSpec((None, None, None, D, FF_TILE), lambda i, j: (layer, idx, j, 0, 0)),
            pl.BlockSpec((None, None, FF_TILE, D), lambda i, j: (layer, idx, j, 0)),
            pl.BlockSpec((1, D), lambda i, j: (0, 0)),
        ],
        out_specs=pl.BlockSpec((tm, D), lambda i, j: (i, 0)),
        out_shape=jax.ShapeDtypeStruct((R, D), F32),
        scratch_shapes=[pltpu.VMEM((tm, D), BF16)],
        compiler_params=_cparams(2),
        name="ffn",
    )(x, gain.reshape(1, D), wg, wu, wd, fgain.reshape(1, D))


def _proj_body(x_ref, gain_ref, w_ref, o_ref, h_ref, *, n_silu):
    j = pl.program_id(1)

    @pl.when(j == 0)
    def _():
        h_ref[...] = (_rms(x_ref[...]) * gain_ref[...]).astype(BF16)

    y = jnp.dot(h_ref[...], w_ref[...], preferred_element_type=F32)
    if n_silu > 0:
        y = jnp.where(j < n_silu, _silu(y), y)
    o_ref[...] = y.astype(BF16)


def _proj(x, gain, w, n_silu_cols, tm):
    R, D = x.shape
    N = w.shape[1]
    return pl.pallas_call(
        functools.partial(_proj_body, n_silu=n_silu_cols // PROJ_TILE),
        grid=(R // tm, N // PROJ_TILE),
        in_specs=[
            pl.BlockSpec((tm, D), lambda i, j: (i, 0)),
            pl.BlockSpec((1, D), lambda i, j: (0, 0)),
            pl.BlockSpec((D, PROJ_TILE), lambda i, j: (0, j)),
        ],
        out_specs=pl.BlockSpec((tm, PROJ_TILE), lambda i, j: (i, j)),
        out_shape=jax.ShapeDtypeStruct((R, N), BF16),
        scratch_shapes=[pltpu.VMEM((tm, D), BF16)],
        compiler_params=_cparams(2),
        name="proj",
    )(x, gain.reshape(1, D), w)


def _lower_bound(logits, layer):
    e = jnp.exp(logits - jnp.max(logits, axis=0, keepdims=True))
    probs = e / jnp.sum(e, axis=0, keepdims=True)
    cs = probs[0:1]
    for i in range(1, layer + 1):
        cs = cs + probs[i:i + 1]
    return cs - probs[0:1]


def _gate_proj_body(x_ref, gain_ref, w_ref, logit_ref, g_ref, k_ref, h_ref, *, layer):
    j = pl.program_id(1)

    @pl.when(j == 0)
    def _():
        h_ref[...] = (_rms(x_ref[...]) * gain_ref[...]).astype(BF16)

    z = jnp.dot(h_ref[...], w_ref[...], preferred_element_type=F32)
    lb = _lower_bound(logit_ref[...], layer)
    g_ref[...] = _logaddexp(jnp.log(jnp.maximum(lb, LB_FLOOR)), jnp.log1p(-lb) + _log_sigmoid(z))
    k_ref[...] = ((1.0 - lb) * _sigmoid(-z)).astype(BF16)


def _gate_proj(x, gain, w, logits, layer, tm):
    R, D = x.shape
    N = w.shape[1]
    spec = pl.BlockSpec((tm, PROJ_TILE), lambda i, j: (i, j))
    return pl.pallas_call(
        functools.partial(_gate_proj_body, layer=layer),
        grid=(R // tm, N // PROJ_TILE),
        in_specs=[
            pl.BlockSpec((tm, D), lambda i, j: (i, 0)),
            pl.BlockSpec((1, D), lambda i, j: (0, 0)),
            pl.BlockSpec((D, PROJ_TILE), lambda i, j: (0, j)),
            pl.BlockSpec((logits.shape[0], PROJ_TILE), lambda i, j: (0, j)),
        ],
        out_specs=[spec, spec],
        out_shape=[jax.ShapeDtypeStruct((R, N), F32), jax.ShapeDtypeStruct((R, N), BF16)],
        scratch_shapes=[pltpu.VMEM((tm, D), BF16)],
        compiler_params=_cparams(2),
        name="gate_proj",
    )(x, gain.reshape(1, D), w, logits)


def _outproj_body(a_ref, w_ref, x_ref, o_ref):
    o_ref[...] = x_ref[...] + jnp.dot(a_ref[...], w_ref[...], preferred_element_type=F32)


def _outproj(a, w, x, tm):
    R, K = a.shape
    D = w.shape[1]
    return pl.pallas_call(
        _outproj_body,
        grid=(R // tm, D // OUT_TILE),
        in_specs=[
            pl.BlockSpec((tm, K), lambda i, j: (i, 0)),
            pl.BlockSpec((K, OUT_TILE), lambda i, j: (0, j)),
            pl.BlockSpec((tm, OUT_TILE), lambda i, j: (i, j)),
        ],
        out_specs=pl.BlockSpec((tm, OUT_TILE), lambda i, j: (i, j)),
        out_shape=jax.ShapeDtypeStruct((R, D), F32),
        compiler_params=_cparams(2),
        name="outproj",
    )(a, w, x)


def _hgrn_tables(C):
    n_lvl = int(math.log2(C))
    assert 1 << n_lvl == C
    n_fine = min(n_lvl, 3)
    r = np.arange(C)
    msum = np.zeros((n_fine + 1, C, C), np.float32)
    pair = np.zeros((n_lvl + 1, C, C), np.float32)
    u = r[None, :]
    for l in range(n_lvl):
        half = 1 << l
        blk = 2 * half
        m = r - r % blk + half - 1
        up = (r % blk) >= half
        if l < n_fine:
            msum[l] = np.where(up[:, None], (u > m[:, None]) & (u <= r[:, None]),
                               (u > r[:, None]) & (u <= m[:, None]))
        pair[l] = ((r[:, None] // blk) == (r[None, :] // blk)) & up[:, None] & ~up[None, :]
    pair[n_lvl] = np.eye(C)
    msum[n_fine] = u <= r[:, None]
    return jnp.asarray(msum.reshape(-1, C), BF16), jnp.asarray(pair, F32)


def _split3(g):
    hi = g.astype(BF16)
    r1 = g - hi.astype(F32)
    mid = r1.astype(BF16)
    lo = (r1 - mid.astype(F32)).astype(BF16)
    return hi, mid, lo


def _hgrn_chunks(qs, gs, ks, vs, gates, hgain, sts, msum, pair_ref, valid):
    n = range(len(qs))
    C = qs[0].shape[0]
    n_lvl = pair_ref.shape[0] - 1
    n_fine = msum.shape[0] // C - 1
    qf = [q.astype(F32) for q in qs]
    kf = [k.astype(F32) for k in ks]
    if valid is not None:
        gs = [g * valid for g in gs]
        kf = [k * valid for k in kf]
        ks = [k.astype(BF16) for k in kf]
    e3 = [jnp.dot(msum, jnp.concatenate(_split3(g), axis=1), preferred_element_type=F32) for g in gs]
    e = [t[:, :LANE] + t[:, LANE:2 * LANE] + t[:, 2 * LANE:] for t in e3]
    c = [t[n_fine * C:] for t in e]
    x_fine = [jnp.exp(t[:n_fine * C]) for t in e]

    scores = [lax.dot_general(qs[b], ks[b], _NT, preferred_element_type=F32) * pair_ref[n_lvl] for b in n]
    for l in range(n_lvl):
        if l < n_fine:
            xl = [t[l * C:(l + 1) * C] for t in x_fine]
        else:
            half = 1 << l
            up = lax.broadcasted_iota(jnp.int32, (1, 2 * half, 1), 1) >= half
            xl = []
            for t in c:
                c3 = t.reshape(C // (2 * half), 2 * half, LANE)
                cm = c3[:, half - 1:half, :]
                xl.append(jnp.exp(jnp.where(up, c3 - cm, cm - c3)).reshape(C, LANE))
        scores = [scores[b] + lax.dot_general((qf[b] * xl[b]).astype(BF16), (kf[b] * xl[b]).astype(BF16), _NT,
                                              preferred_element_type=F32) * pair_ref[l] for b in n]
    xc = [jnp.exp(t) for t in c]
    xt = [jnp.exp(t[C - 1:C, :] - t) for t in c]
    o = [lax.dot_general((qf[b] * xc[b]).astype(BF16), sts[b].astype(BF16), _NT, preferred_element_type=F32)
         for b in n]
    o = [o[b] + jnp.dot(scores[b].astype(BF16), vs[b], preferred_element_type=F32) for b in n]
    st_new = [sts[b] * xc[b][C - 1:C, :] + lax.dot_general(vs[b], (kf[b] * xt[b]).astype(BF16), _TN,
                                                           preferred_element_type=F32) for b in n]
    og = [(_rms(o[b]) * hgain * gates[b].astype(F32)).astype(BF16) for b in n]
    return og, st_new


def _hgrn_prompt_body(q_ref, gate_ref, v_ref, g_ref, k_ref, hgain_ref, msum_ref, pair_ref,
                      og_ref, so_ref, st_ref, *, n_batch, n_chunks):
    C = CHUNK
    s = pl.program_id(1)

    def step(rows, sts, valid):
        return _hgrn_chunks([q_ref[r, :] for r in rows], [g_ref[r, :] for r in rows], [k_ref[r, :] for r in rows],
                            [v_ref[r, :] for r in rows], [gate_ref[r, :] for r in rows],
                            hgain_ref[...], sts, msum_ref[...], pair_ref, valid)

    @pl.when(s == 0)
    def _():
        valid = (lax.broadcasted_iota(jnp.int32, (C, 1), 0) < N_META).astype(F32)
        og, st_new = step([pl.ds(C, C)], [jnp.zeros((LANE, HA_DK), F32)], valid)
        og_ref[...] = jnp.zeros_like(og_ref)
        og_ref[pl.ds(C, C), :] = og[0]
        for b in range(n_batch):
            st_ref[b] = st_new[0]

    @pl.when(s > 0)
    def _():
        og, st_new = step([pl.ds(b * C, C) for b in range(n_batch)], [st_ref[b] for b in range(n_batch)], None)
        for b in range(n_batch):
            og_ref[pl.ds(b * C, C), :] = og[b]
            st_ref[b] = st_new[b]

    @pl.when(s == n_chunks)
    def _():
        for b in range(n_batch):
            so_ref[b] = st_ref[b].T


def _hgrn_prompt(pb, g, k, hgain, n_batch, seq, tables):
    R = pb.shape[0]
    D = pb.shape[1] // 3
    H = D // HA_DK
    C = CHUNK
    B = n_batch
    n_chunks = seq // C
    msum, pair = tables
    assert B >= 2 and R >= n_chunks * B * C + 2 * C

    def spec(col0):
        return pl.BlockSpec((B * C, LANE), lambda h, s: (jnp.where(s == 0, n_chunks, s - 1), col0 + h))

    return pl.pallas_call(
        functools.partial(_hgrn_prompt_body, n_batch=B, n_chunks=n_chunks),
        grid=(H, 1 + n_chunks),
        in_specs=[spec(0), spec(H), spec(2 * H), spec(0), spec(0),
                  pl.BlockSpec((1, LANE), lambda h, s: (0, h)),
                  pl.BlockSpec(msum.shape, lambda h, s: (0, 0)),
                  pl.BlockSpec(pair.shape, lambda h, s: (0, 0, 0))],
        out_specs=[spec(0), pl.BlockSpec((B, None, HA_DK, LANE), lambda h, s: (0, h, 0, 0))],
        out_shape=[jax.ShapeDtypeStruct((R, D), BF16), jax.ShapeDtypeStruct((B, H, HA_DK, LANE), F32)],
        scratch_shapes=[pltpu.VMEM((B, LANE, HA_DK), F32)],
        compiler_params=_cparams(2),
        name="hgrn_prompt",
    )(pb, pb, pb, g, k, hgain.reshape(1, D), msum, pair)


def _col_bcast(x):
    bs = x.shape[0]
    return jnp.concatenate([x] * (LANE // bs), axis=0).T


def _hgrn_sample_body(q_ref, gate_ref, v_ref, g_ref, k_ref, hgain_ref, s_ref, og_in_ref, sn_in_ref,
                      og_ref, sn_ref, o_scr, *, bs):
    del og_in_ref, sn_in_ref
    i = pl.program_id(1)

    @pl.when(i == 0)
    def _():
        og_ref[...] = jnp.zeros_like(og_ref)

    rows = pl.ds(pl.multiple_of(i * bs, bs), bs)
    q = q_ref[rows, :]
    v = v_ref[rows, :].astype(F32)
    ft = _col_bcast(jnp.exp(g_ref[rows, :]))
    kt = _col_bcast(k_ref[rows, :].astype(F32))
    for b in range(bs):
        sn = ft[:, b:b + 1] * s_ref[b] + kt[:, b:b + 1] * v[b:b + 1, :]
        sn_ref[b] = sn
        o_scr[b:b + 1, :] = jnp.dot(q, sn.astype(BF16), preferred_element_type=F32)[b:b + 1, :]
    og_ref[rows, :] = (_rms(o_scr[...]) * hgain_ref[...] * gate_ref[rows, :].astype(F32)).astype(BF16)


def _hgrn_sample(pb, g, k, hgain, state, og, new_state, layer, n_sample, sample_blk):
    D = pb.shape[1] // 3
    H = D // HA_DK
    bs = 16
    assert n_sample % bs == 0 and n_sample <= LANE
    sp = LANE

    def sec(col0):
        return pl.BlockSpec((sp, LANE), lambda h, i: (sample_blk, col0 + h))

    state_spec = pl.BlockSpec((None, bs, None, HA_DK, LANE), lambda h, i: (layer, i, h, 0, 0))
    any_spec = pl.BlockSpec(memory_space=pl.ANY)
    args = [pb, pb, pb, g, k, hgain.reshape(1, D), state, og]
    in_specs = [sec(0), sec(H), sec(2 * H), sec(0), sec(0),
                pl.BlockSpec((1, LANE), lambda h, i: (0, h)), state_spec, any_spec]
    aliases = {7: 0}
    if new_state is None:
        def body(q, gt, v, g_, k_, hg, s, og_in, og_out, sn_out, o_scr):
            _hgrn_sample_body(q, gt, v, g_, k_, hg, s, og_in, None, og_out, sn_out, o_scr, bs=bs)
    else:
        args.append(new_state)
        in_specs.append(any_spec)
        aliases[8] = 1
        body = functools.partial(_hgrn_sample_body, bs=bs)
    return pl.pallas_call(
        body,
        grid=(H, n_sample // bs),
        in_specs=in_specs,
        out_specs=[pl.BlockSpec((sp, LANE), lambda h, i: (sample_blk, h)), state_spec],
        out_shape=[jax.ShapeDtypeStruct(og.shape, og.dtype), jax.ShapeDtypeStruct(state.shape, F32)],
        scratch_shapes=[pltpu.VMEM((bs, LANE), F32)],
        input_output_aliases=aliases,
        compiler_params=_cparams(2),
        name="hgrn_sample",
    )(*args)


def _rotary(x, cos, sin):
    half = x.shape[-1] // 2
    x1, x2 = x[:, :half], x[:, half:]
    return jnp.concatenate([x1 * cos - x2 * sin, x1 * sin + x2 * cos], axis=-1)


def _ret_chunks(qs, ks, vs, gates, cos, sin, lg, sts, n_valid):
    n = range(len(qs))
    C, dk = qs[0].shape
    t_col = lax.broadcasted_iota(jnp.int32, (C, 1), 0)
    t_row = lax.broadcasted_iota(jnp.int32, (1, C), 1)
    n_col = jnp.minimum(t_col + 1, n_valid).astype(F32)
    n_row = jnp.minimum(t_row + 1, n_valid).astype(F32)
    n_last = float(min(C, n_valid))
    decay = jnp.where(t_col >= t_row, jnp.exp(lg * (n_col - n_row)), 0.0)
    q_scale = jnp.exp(lg * n_col)
    k_scale = jnp.exp(lg * (n_last - n_col))
    k_norm = dk ** -0.5
    if n_valid < C:
        k_norm = k_norm * (t_col < n_valid).astype(F32)
    q = [_rotary(t.astype(F32), cos, sin) for t in qs]
    k = [_rotary(t.astype(F32), cos, sin) * k_norm for t in ks]
    scores = [lax.dot_general(q[b].astype(BF16), k[b].astype(BF16), _NT, preferred_element_type=F32) * decay
              for b in n]
    o = [jnp.dot((q[b] * q_scale).astype(BF16), sts[b].astype(BF16), preferred_element_type=F32) for b in n]
    o = [o[b] + jnp.dot(scores[b].astype(BF16), vs[b], preferred_element_type=F32) for b in n]
    st_new = [jnp.exp(lg * n_last) * sts[b] + lax.dot_general((k[b] * k_scale).astype(BF16), vs[b], _TN,
                                                              preferred_element_type=F32) for b in n]
    og = [(_rms(o[b]) * gates[b].astype(F32)).astype(BF16) for b in n]
    return og, st_new


def _ret_prompt_body(lg_ref, q_ref, k_ref, v_ref, gate_ref, cos_ref, sin_ref, og_ref, so_ref, st_ref,
                     *, n_batch, n_chunks):
    C = CHUNK
    lg = lg_ref[pl.program_id(0)]
    s = pl.program_id(1)

    def step(rows, sts, n_valid):
        return _ret_chunks([q_ref[r, :] for r in rows], [k_ref[r, :] for r in rows], [v_ref[r, :] for r in rows],
                           [gate_ref[r, :] for r in rows], cos_ref[...], sin_ref[...], lg, sts, n_valid)

    @pl.when(s == 0)
    def _():
        og, st_new = step([pl.ds(C, C)], [jnp.zeros(st_ref.shape[1:], F32)], N_META)
        og_ref[...] = jnp.zeros_like(og_ref)
        og_ref[pl.ds(C, C), :] = og[0]
        for b in range(n_batch):
            st_ref[b] = st_new[0]

    @pl.when(s > 0)
    def _():
        og, st_new = step([pl.ds(b * C, C) for b in range(n_batch)], [st_ref[b] for b in range(n_batch)], C)
        for b in range(n_batch):
            og_ref[pl.ds(b * C, C), :] = og[b]
            st_ref[b] = st_new[b]

    @pl.when(s == n_chunks)
    def _():
        so_ref[...] = st_ref[...]


def _ret_prompt(p, log_gamma, cos_tab, sin_tab, n_batch, seq):
    R = p.shape[0]
    D = p.shape[1] // 6
    H = HB_HEADS
    dk = D // H
    dv = 2 * D // H
    C = CHUNK
    B = n_batch
    n_chunks = seq // C
    qk_off = 2 * D // dk
    v_off = 4 * D // dv

    def spec(width, col0):
        return pl.BlockSpec((B * C, width), lambda h, s: (jnp.where(s == 0, n_chunks, s - 1), col0 + h))

    tab_spec = pl.BlockSpec((C, dk // 2), lambda h, s: (s, 0))
    return pl.pallas_call(
        functools.partial(_ret_prompt_body, n_batch=B, n_chunks=n_chunks),
        grid=(H, 1 + n_chunks),
        in_specs=[pl.BlockSpec(memory_space=pltpu.SMEM),
                  spec(dk, qk_off), spec(dk, qk_off + H), spec(dv, v_off), spec(dv, 0),
                  tab_spec, tab_spec],
        out_specs=[spec(dv, 0), pl.BlockSpec((B, None, dk, dv), lambda h, s: (0, h, 0, 0))],
        out_shape=[jax.ShapeDtypeStruct((R, 2 * D), BF16), jax.ShapeDtypeStruct((B, H, dk, dv), F32)],
        scratch_shapes=[pltpu.VMEM((B, dk, dv), F32)],
        compiler_params=_cparams(2),
        name="ret_prompt",
    )(log_gamma, p, p, p, p, cos_tab, sin_tab)


def _ret_sample_body(lg_ref, q_ref, k_ref, v_ref, gate_ref, cos_ref, sin_ref, s_ref, og_in_ref, sn_in_ref,
                     og_ref, sn_ref, o_scr, *, bs):
    del og_in_ref, sn_in_ref
    h = pl.program_id(0)
    i = pl.program_id(1)

    @pl.when(i == 0)
    def _():
        og_ref[...] = jnp.zeros_like(og_ref)

    rows = pl.ds(pl.multiple_of(i * bs, bs), bs)
    gamma = jnp.exp(lg_ref[h])
    cos, sin = cos_ref[...], sin_ref[...]
    dk = q_ref.shape[-1]
    q = _rotary(q_ref[rows, :].astype(F32), cos, sin)
    k = _rotary(k_ref[rows, :].astype(F32), cos, sin) * (dk ** -0.5)
    v = v_ref[rows, :].astype(F32)
    qt = jnp.concatenate([_col_bcast(q[:, :LANE]), _col_bcast(q[:, LANE:])], axis=0)
    kt = jnp.concatenate([_col_bcast(k[:, :LANE]), _col_bcast(k[:, LANE:])], axis=0)
    for b in range(bs):
        sn = gamma * s_ref[b] + kt[:, b:b + 1] * v[b:b + 1, :]
        sn_ref[b] = sn
        o_scr[b:b + 1, :] = jnp.sum(qt[:, b:b + 1] * sn, axis=0, keepdims=True)
    og_ref[rows, :] = (_rms(o_scr[...]) * gate_ref[rows, :].astype(F32)).astype(BF16)


def _ret_sample(p, log_gamma, cos_s, sin_s, state, og, new_state, layer, n_sample, sample_blk):
    D = p.shape[1] // 6
    H = HB_HEADS
    dk = D // H
    dv = 2 * D // H
    bs = 8
    assert n_sample % bs == 0 and n_sample <= LANE and dk == 2 * LANE
    sp = LANE
    qk_off = 2 * D // dk
    v_off = 4 * D // dv
    state_spec = pl.BlockSpec((None, bs, None, dk, dv), lambda h, i: (layer, i, h, 0, 0))
    any_spec = pl.BlockSpec(memory_space=pl.ANY)
    args = [log_gamma, p, p, p, p, cos_s, sin_s, state, og]
    in_specs = [
        pl.BlockSpec(memory_space=pltpu.SMEM),
        pl.BlockSpec((sp, dk), lambda h, i: (sample_blk, qk_off + h)),
        pl.BlockSpec((sp, dk), lambda h, i: (sample_blk, qk_off + H + h)),
        pl.BlockSpec((sp, dv), lambda h, i: (sample_blk, v_off + h)),
        pl.BlockSpec((sp, dv), lambda h, i: (sample_blk, h)),
        pl.BlockSpec((1, dk // 2), lambda h, i: (0, 0)),
        pl.BlockSpec((1, dk // 2), lambda h, i: (0, 0)),
        state_spec, any_spec]
    aliases = {8: 0}
    if new_state is None:
        def body(lg, q, k, v, gt, cs, sn_, s, og_in, og_out, sn_out, o_scr):
            _ret_sample_body(lg, q, k, v, gt, cs, sn_, s, og_in, None, og_out, sn_out, o_scr, bs=bs)
    else:
        args.append(new_state)
        in_specs.append(any_spec)
        aliases[9] = 1
        body = functools.partial(_ret_sample_body, bs=bs)
    return pl.pallas_call(
        body,
        grid=(H, n_sample // bs),
        in_specs=in_specs,
        out_specs=[pl.BlockSpec((sp, dv), lambda h, i: (sample_blk, h)), state_spec],
        out_shape=[jax.ShapeDtypeStruct(og.shape, og.dtype), jax.ShapeDtypeStruct(state.shape, F32)],
        scratch_shapes=[pltpu.VMEM((bs, dv), F32)],
        input_output_aliases=aliases,
        compiler_params=_cparams(2),
        name="ret_sample",
    )(*args)


def _rope_tables(pos, half):
    inv = ROPE_BASE ** (-jnp.linspace(0.0, 1.0, half, dtype=F32))
    ang = pos.astype(F32)[:, None] * inv[None, :]
    return jnp.cos(ang), jnp.sin(ang)


def kernel(x_prompt, x_sample, state_hgrn, state_ret, meta_tokens, norm_ffn, ffn_w_gate, ffn_w_up, ffn_w_down, norm_mix, hg_wq, hg_wf, hg_wi, hg_wg, hg_wo, hg_norm, hg_lb_logits, rt_wq, rt_wk, rt_wv, rt_wg, rt_wo, norm_final):
    B, L, D = x_prompt.shape
    NS = x_sample.shape[0]
    depth = norm_mix.shape[0]
    n_meta = meta_tokens.shape[0]
    C = CHUNK
    assert n_meta == N_META and x_sample.shape[1] == 1 and L % C == 0 and NS <= LANE
    n_chunks = L // C

    tp = B * L
    sp = LANE
    rows0 = tp + sp + C
    tm = next(t for t in ROW_TILES if _round_up(rows0, t) - rows0 < LANE) if rows0 >= ROW_TILES[0] else rows0
    R = _round_up(rows0, tm)
    x = jnp.concatenate([
        x_prompt.reshape(B, n_chunks, C, D).transpose(1, 0, 2, 3).reshape(tp, D),
        x_sample.reshape(NS, D), jnp.zeros((sp - NS, D), F32),
        meta_tokens.astype(F32), jnp.zeros((R - tp - sp - n_meta, D), F32)], axis=0)
    sample_blk = tp // LANE

    wg = _cast_ffn_in(ffn_w_gate)
    wu = _cast_ffn_in(ffn_w_up)
    wd = _cast_ffn_out(ffn_w_down)

    half = D // HB_HEADS // 2
    cos_p, sin_p = _rope_tables(jnp.arange(n_meta + L, dtype=jnp.int32), half)
    cos_s, sin_s = _rope_tables(PAST_LEN + jnp.arange(1, dtype=jnp.int32), half)

    def chunk_table(t):
        meta = jnp.concatenate([t[:n_meta], jnp.zeros((C - n_meta, half), F32)], axis=0)
        return jnp.concatenate([meta, t[n_meta:]], axis=0)

    cos_tab, sin_tab = chunk_table(cos_p), chunk_table(sin_p)
    log_gamma = jnp.log1p(-jnp.exp2(-5.0 - jnp.arange(HB_HEADS, dtype=F32)))
    tables = _hgrn_tables(C)

    new_hgrn_p, new_ret_p = [], []
    new_hgrn_s = new_ret_s = None
    for i in range(depth):
        x = _ffn(x, norm_ffn[i, 0], wg, wu, wd, i, 0, None, tm)
        j = i // 2
        if i % 2 == 0:
            w_in = jnp.concatenate([hg_wq[j].astype(BF16), hg_wg[j].astype(BF16), hg_wi[j].astype(BF16)], axis=1)
            pb = _proj(x, norm_mix[i], w_in, 2 * D, tm)
            g, k = _gate_proj(x, norm_mix[i], hg_wf[j].astype(BF16), hg_lb_logits, j, tm)
            og, s_p = _hgrn_prompt(pb, g, k, hg_norm[j], B, L, tables)
            og, new_hgrn_s = _hgrn_sample(pb, g, k, hg_norm[j], state_hgrn, og, new_hgrn_s, j, NS, sample_blk)
            new_hgrn_p.append(s_p)
            x = _outproj(og, hg_wo[j].astype(BF16), x, tm)
        else:
            w_in = jnp.concatenate([rt_wg[j].astype(BF16), rt_wq[j].astype(BF16),
                                    rt_wk[j].astype(BF16), rt_wv[j].astype(BF16)], axis=1)
            pb = _proj(x, norm_mix[i], w_in, 2 * D, tm)
            og, s_p = _ret_prompt(pb, log_gamma, cos_tab, sin_tab, B, L)
            og, new_ret_s = _ret_sample(pb, log_gamma, cos_s, sin_s, state_ret, og, new_ret_s, j, NS, sample_blk)
            new_ret_p.append(s_p)
            x = _outproj(og, rt_wo[j].astype(BF16), x, tm)
        x = _ffn(x, norm_ffn[i, 1], wg, wu, wd, i, 1, norm_final if i == depth - 1 else None, tm)

    y_prompt = x[:tp].reshape(n_chunks, B, C, D).transpose(1, 0, 2, 3).reshape(B, L, D)
    y_sample = x[tp:tp + NS].reshape(NS, 1, D)
    return (y_prompt, y_sample, jnp.stack(new_hgrn_p), new_hgrn_s, jnp.stack(new_ret_p), new_ret_s)
```

```python
import functools
import math

import numpy as np
import jax
import jax.numpy as jnp
from jax import lax
from jax.experimental import pallas as pl
from jax.experimental.pallas import tpu as pltpu

F32 = jnp.float32
BF16 = jnp.bfloat16

EPS = 1e-6
LB_FLOOR = 1e-30
ROPE_BASE = 10000.0
N_META = 16
PAST_LEN = 16384
HA_DK = 128
HB_HEADS = 8

LANE = 128
CHUNK = 128
HGRN_HEADS_PER_STEP = 2
FF_TILE = 512
PROJ_TILE = 2048
GATE_TILE = 1024
OUT_TILE = 1024
ROW_TILES = (768, 512, 256, 128)
VMEM_LIMIT = 56 * 1024 * 1024

_NT = (((1,), (1,)), ((), ()))
_TN = (((0,), (0,)), ((), ()))


def _round_up(a, m):
    return (a + m - 1) // m * m


def _cparams(n_axes):
    return pltpu.CompilerParams(dimension_semantics=("arbitrary",) * n_axes,
                                vmem_limit_bytes=VMEM_LIMIT)


def _rms(x):
    return x * lax.rsqrt(jnp.mean(x * x, axis=-1, keepdims=True) + EPS)


def _sigmoid(x):
    return 1.0 / (1.0 + jnp.exp(-x))


def _silu(x):
    return x * _sigmoid(x)


def _cast_body(w_ref, o_ref, *, axis, n_valid_last):
    j = pl.program_id(2)
    w = w_ref[...]
    idx = lax.broadcasted_iota(jnp.int32, w.shape, axis)
    limit = jnp.where(j == pl.num_programs(2) - 1, n_valid_last, w.shape[axis])
    o_ref[...] = jnp.where(idx < limit, w, 0.0).astype(BF16)


def _cast_ffn_in(w):
    n_l, n_i, D, F = w.shape
    nj = pl.cdiv(F, FF_TILE)
    return pl.pallas_call(
        functools.partial(_cast_body, axis=1, n_valid_last=F - (nj - 1) * FF_TILE),
        grid=(n_l, n_i, nj),
        in_specs=[pl.BlockSpec((None, None, D, FF_TILE), lambda l, i, j: (l, i, 0, j))],
        out_specs=pl.BlockSpec((None, None, None, D, FF_TILE), lambda l, i, j: (l, i, j, 0, 0)),
        out_shape=jax.ShapeDtypeStruct((n_l, n_i, nj, D, FF_TILE), BF16),
        compiler_params=_cparams(3),
        name="cast_ffn_in",
    )(w)


def _cast_ffn_out(w):
    n_l, n_i, F, D = w.shape
    nj = pl.cdiv(F, FF_TILE)
    return pl.pallas_call(
        functools.partial(_cast_body, axis=0, n_valid_last=F - (nj - 1) * FF_TILE),
        grid=(n_l, n_i, nj),
        in_specs=[pl.BlockSpec((None, None, FF_TILE, D), lambda l, i, j: (l, i, j, 0))],
        out_specs=pl.BlockSpec((None, None, FF_TILE, D), lambda l, i, j: (l, i, j, 0)),
        out_shape=jax.ShapeDtypeStruct((n_l, n_i, nj * FF_TILE, D), BF16),
        compiler_params=_cparams(3),
        name="cast_ffn_out",
    )(w)


def _ffn_body(x_ref, gain_ref, wg_ref, wu_ref, wd_ref, fgain_ref, o_ref, h_ref, *, final_norm):
    j = pl.program_id(1)

    @pl.when(j == 0)
    def _():
        x = x_ref[...]
        h_ref[...] = (_rms(x) * gain_ref[...]).astype(BF16)
        o_ref[...] = x

    h = h_ref[...]
    g = jnp.dot(h, wg_ref[...], preferred_element_type=F32)
    u = jnp.dot(h, wu_ref[...], preferred_element_type=F32)
    a = (0.5 * _silu(g) * u).astype(BF16)
    o_ref[...] += jnp.dot(a, wd_ref[...], preferred_element_type=F32)

    if final_norm:
        @pl.when(j == pl.num_programs(1) - 1)
        def _():
            o_ref[...] = _rms(o_ref[...]) * fgain_ref[...]


def _ffn(x, gain, wg, wu, wd, layer, idx, fgain, tm):
    R, D = x.shape
    nj = wg.shape[2]
    final_norm = fgain is not None
    if fgain is None:
        fgain = gain
    return pl.pallas_call(
        functools.partial(_ffn_body, final_norm=final_norm),
        grid=(R // tm, nj),
        in_specs=[
            pl.BlockSpec((tm, D), lambda i, j: (i, 0)),
            pl.BlockSpec((1, D), lambda i, j: (0, 0)),
            pl.BlockSpec((None, None, None, D, FF_TILE), lambda i, j: (layer, idx, j, 0, 0)),
            pl.BlockSpec((None, None, None, D, FF_TILE), lambda i, j: (layer, idx, j, 0, 0)),
            pl.BlockSpec((None, None, FF_TILE, D), lambda i, j: (layer, idx, j, 0)),
            pl.BlockSpec((1, D), lambda i, j: (0, 0)),
        ],
        out_specs=pl.BlockSpec((tm, D), lambda i, j: (i, 0)),
        out_shape=jax.ShapeDtypeStruct((R, D), F32),
        scratch_shapes=[pltpu.VMEM((tm, D), BF16)],
        compiler_params=_cparams(2),
        name="ffn",
    )(x, gain.reshape(1, D), wg, wu, wd, fgain.reshape(1, D))


def _proj_body(x_ref, gain_ref, w_ref, o_ref, h_ref, *, n_silu):
    j = pl.program_id(1)

    @pl.when(j == 0)
    def _():
        h_ref[...] = (_rms(x_ref[...]) * gain_ref[...]).astype(BF16)

    y = jnp.dot(h_ref[...], w_ref[...], preferred_element_type=F32)
    if n_silu > 0:
        y = jnp.where(j < n_silu, _silu(y), y)
    o_ref[...] = y.astype(BF16)


def _proj(x, gain, w, n_silu_cols, tm):
    R, D = x.shape
    N = w.shape[1]
    return pl.pallas_call(
        functools.partial(_proj_body, n_silu=n_silu_cols // PROJ_TILE),
        grid=(R // tm, N // PROJ_TILE),
        in_specs=[
            pl.BlockSpec((tm, D), lambda i, j: (i, 0)),
            pl.BlockSpec((1, D), lambda i, j: (0, 0)),
            pl.BlockSpec((D, PROJ_TILE), lambda i, j: (0, j)),
        ],
        out_specs=pl.BlockSpec((tm, PROJ_TILE), lambda i, j: (i, j)),
        out_shape=jax.ShapeDtypeStruct((R, N), BF16),
        scratch_shapes=[pltpu.VMEM((tm, D), BF16)],
        compiler_params=_cparams(2),
        name="proj",
    )(x, gain.reshape(1, D), w)


def _lower_bound(logits, layer):
    e = jnp.exp(logits - jnp.max(logits, axis=0, keepdims=True))
    probs = e / jnp.sum(e, axis=0, keepdims=True)
    cs = probs[0:1]
    for i in range(1, layer + 1):
        cs = cs + probs[i:i + 1]
    return cs - probs[0:1]


def _gate_proj_body(x_ref, gain_ref, w_ref, logit_ref, g_ref, k_ref, h_ref, *, layer):
    j = pl.program_id(1)

    @pl.when(j == 0)
    def _():
        h_ref[...] = (_rms(x_ref[...]) * gain_ref[...]).astype(BF16)

    z = jnp.dot(h_ref[...], w_ref[...], preferred_element_type=F32)
    lb = _lower_bound(logit_ref[...], layer)
    t = jnp.exp(-jnp.abs(z))
    r = 1.0 / (1.0 + t)
    tr = t * r
    pos = z >= 0.0
    g_ref[...] = jnp.log(jnp.maximum(lb, LB_FLOOR) + (1.0 - lb) * jnp.where(pos, r, tr))
    k_ref[...] = ((1.0 - lb) * jnp.where(pos, tr, r)).astype(BF16)


def _gate_proj(x, gain, w, logits, layer, tm):
    R, D = x.shape
    N = w.shape[1]
    spec = pl.BlockSpec((tm, GATE_TILE), lambda i, j: (i, j))
    return pl.pallas_call(
        functools.partial(_gate_proj_body, layer=layer),
        grid=(R // tm, N // GATE_TILE),
        in_specs=[
            pl.BlockSpec((tm, D), lambda i, j: (i, 0)),
            pl.BlockSpec((1, D), lambda i, j: (0, 0)),
            pl.BlockSpec((D, GATE_TILE), lambda i, j: (0, j)),
            pl.BlockSpec((logits.shape[0], GATE_TILE), lambda i, j: (0, j)),
        ],
        out_specs=[spec, spec],
        out_shape=[jax.ShapeDtypeStruct((R, N), F32), jax.ShapeDtypeStruct((R, N), BF16)],
        scratch_shapes=[pltpu.VMEM((tm, D), BF16)],
        compiler_params=_cparams(2),
        name="gate_proj",
    )(x, gain.reshape(1, D), w, logits)


def _outproj_body(a_ref, w_ref, x_ref, o_ref):
    o_ref[...] = x_ref[...] + jnp.dot(a_ref[...], w_ref[...], preferred_element_type=F32)


def _outproj(a, w, x, tm):
    R, K = a.shape
    D = w.shape[1]
    return pl.pallas_call(
        _outproj_body,
        grid=(R // tm, D // OUT_TILE),
        in_specs=[
            pl.BlockSpec((tm, K), lambda i, j: (i, 0)),
            pl.BlockSpec((K, OUT_TILE), lambda i, j: (0, j)),
            pl.BlockSpec((tm, OUT_TILE), lambda i, j: (i, j)),
        ],
        out_specs=pl.BlockSpec((tm, OUT_TILE), lambda i, j: (i, j)),
        out_shape=jax.ShapeDtypeStruct((R, D), F32),
        compiler_params=_cparams(2),
        name="outproj",
    )(a, w, x)


def _hgrn_tables(C):
    n_lvl = int(math.log2(C))
    assert 1 << n_lvl == C
    n_fine = min(n_lvl, 3)
    r = np.arange(C)
    msum = np.zeros((n_fine + 1, C, C), np.float32)
    pair = np.zeros((n_lvl + 1, C, C), np.float32)
    u = r[None, :]
    for l in range(n_lvl):
        half = 1 << l
        blk = 2 * half
        m = r - r % blk + half - 1
        up = (r % blk) >= half
        if l < n_fine:
            msum[l] = np.where(up[:, None], (u > m[:, None]) & (u <= r[:, None]),
                               (u > r[:, None]) & (u <= m[:, None]))
        pair[l] = ((r[:, None] // blk) == (r[None, :] // blk)) & up[:, None] & ~up[None, :]
    pair[n_lvl] = np.eye(C)
    msum[n_fine] = u <= r[:, None]
    return jnp.asarray(msum.reshape(-1, C), BF16), jnp.asarray(pair, F32)


def _split3(g):
    hi = g.astype(BF16)
    r1 = g - hi.astype(F32)
    mid = r1.astype(BF16)
    lo = (r1 - mid.astype(F32)).astype(BF16)
    return hi, mid, lo


def _hgrn_chunks(qs, gs, ks, vs, gates, hgains, sts, msum, pair_ref, valid):
    n = range(len(qs))
    C = qs[0].shape[0]
    n_lvl = pair_ref.shape[0] - 1
    n_fine = msum.shape[0] // C - 1
    qf = [q.astype(F32) for q in qs]
    kf = [k.astype(F32) for k in ks]
    if valid is not None:
        gs = [g * valid for g in gs]
        kf = [k * valid for k in kf]
        ks = [k.astype(BF16) for k in kf]
    e3 = [jnp.dot(msum, jnp.concatenate(_split3(g), axis=1), preferred_element_type=F32) for g in gs]
    e = [t[:, :LANE] + t[:, LANE:2 * LANE] + t[:, 2 * LANE:] for t in e3]
    c = [t[n_fine * C:] for t in e]
    x_fine = [jnp.exp(t[:n_fine * C]) for t in e]

    scores = [lax.dot_general(qs[b], ks[b], _NT, preferred_element_type=F32) * pair_ref[n_lvl] for b in n]
    for l in range(n_lvl):
        if l < n_fine:
            xl = [t[l * C:(l + 1) * C] for t in x_fine]
        else:
            half = 1 << l
            up = lax.broadcasted_iota(jnp.int32, (1, 2 * half, 1), 1) >= half
            xl = []
            for t in c:
                c3 = t.reshape(C // (2 * half), 2 * half, LANE)
                cm = c3[:, half - 1:half, :]
                xl.append(jnp.exp(jnp.where(up, c3 - cm, cm - c3)).reshape(C, LANE))
        scores = [scores[b] + lax.dot_general((qf[b] * xl[b]).astype(BF16), (kf[b] * xl[b]).astype(BF16), _NT,
                                              preferred_element_type=F32) * pair_ref[l] for b in n]
    xc = [jnp.exp(t) for t in c]
    xt = [jnp.exp(t[C - 1:C, :] - t) for t in c]
    o = [lax.dot_general((qf[b] * xc[b]).astype(BF16), sts[b].astype(BF16), _NT, preferred_element_type=F32)
         for b in n]
    o = [o[b] + jnp.dot(scores[b].astype(BF16), vs[b], preferred_element_type=F32) for b in n]
    st_new = [sts[b] * xc[b][C - 1:C, :] + lax.dot_general(vs[b], (kf[b] * xt[b]).astype(BF16), _TN,
                                                           preferred_element_type=F32) for b in n]
    og = [(_rms(o[b]) * hgains[b] * gates[b].astype(F32)).astype(BF16) for b in n]
    return og, st_new


def _hgrn_prompt_body(q_ref, gate_ref, v_ref, g_ref, k_ref, hgain_ref, msum_ref, pair_ref,
                      og_ref, so_ref, st_ref, *, n_batch, n_chunks, n_heads):
    C = CHUNK
    s = pl.program_id(1)

    def step(seqs, sts, valid):
        tiles = [(r, pl.ds(hh * LANE, LANE)) for r, hh in seqs]
        return _hgrn_chunks([q_ref[t] for t in tiles], [g_ref[t] for t in tiles], [k_ref[t] for t in tiles],
                            [v_ref[t] for t in tiles], [gate_ref[t] for t in tiles],
                            [hgain_ref[:, t[1]] for t in tiles], sts, msum_ref[...], pair_ref, valid)

    @pl.when(s == 0)
    def _():
        valid = (lax.broadcasted_iota(jnp.int32, (C, 1), 0) < N_META).astype(F32)
        seqs = [(pl.ds(C, C), hh) for hh in range(n_heads)]
        og, st_new = step(seqs, [jnp.zeros((LANE, HA_DK), F32)] * n_heads, valid)
        og_ref[...] = jnp.zeros_like(og_ref)
        for hh in range(n_heads):
            og_ref[pl.ds(C, C), pl.ds(hh * LANE, LANE)] = og[hh]
            for b in range(n_batch):
                st_ref[hh * n_batch + b] = st_new[hh]

    @pl.when(s > 0)
    def _():
        seqs = [(pl.ds(b * C, C), hh) for hh in range(n_heads) for b in range(n_batch)]
        og, st_new = step(seqs, [st_ref[i] for i in range(len(seqs))], None)
        for i, (rows, hh) in enumerate(seqs):
            og_ref[rows, pl.ds(hh * LANE, LANE)] = og[i]
            st_ref[i] = st_new[i]

    @pl.when(s == n_chunks)
    def _():
        for hh in range(n_heads):
            for b in range(n_batch):
                so_ref[b, hh] = st_ref[hh * n_batch + b].T


def _hgrn_prompt(pb, g, k, hgain, n_batch, seq, tables):
    R = pb.shape[0]
    D = pb.shape[1] // 3
    H = D // HA_DK
    C = CHUNK
    B = n_batch
    hp = HGRN_HEADS_PER_STEP
    n_chunks = seq // C
    msum, pair = tables
    assert B >= 2 and R >= n_chunks * B * C + 2 * C and H % hp == 0

    def spec(col0):
        return pl.BlockSpec((B * C, hp * LANE),
                            lambda h, s: (jnp.where(s == 0, n_chunks, s - 1), col0 // hp + h))

    return pl.pallas_call(
        functools.partial(_hgrn_prompt_body, n_batch=B, n_chunks=n_chunks, n_heads=hp),
        grid=(H // hp, 1 + n_chunks),
        in_specs=[spec(0), spec(H), spec(2 * H), spec(0), spec(0),
                  pl.BlockSpec((1, hp * LANE), lambda h, s: (0, h)),
                  pl.BlockSpec(msum.shape, lambda h, s: (0, 0)),
                  pl.BlockSpec(pair.shape, lambda h, s: (0, 0, 0))],
        out_specs=[spec(0), pl.BlockSpec((B, hp, HA_DK, LANE), lambda h, s: (0, h, 0, 0))],
        out_shape=[jax.ShapeDtypeStruct((R, D), BF16), jax.ShapeDtypeStruct((B, H, HA_DK, LANE), F32)],
        scratch_shapes=[pltpu.VMEM((hp * B, LANE, HA_DK), F32)],
        compiler_params=_cparams(2),
        name="hgrn_prompt",
    )(pb, pb, pb, g, k, hgain.reshape(1, D), msum, pair)


def _col_bcast(x):
    bs = x.shape[0]
    return jnp.concatenate([x] * (LANE // bs), axis=0).T


def _hgrn_sample_body(q_ref, gate_ref, v_ref, g_ref, k_ref, hgain_ref, s_ref, og_in_ref, sn_in_ref,
                      og_ref, sn_ref, o_scr, *, bs):
    del og_in_ref, sn_in_ref
    i = pl.program_id(1)

    @pl.when(i == 0)
    def _():
        og_ref[...] = jnp.zeros_like(og_ref)

    rows = pl.ds(pl.multiple_of(i * bs, bs), bs)
    q = q_ref[rows, :]
    v = v_ref[rows, :].astype(F32)
    ft = _col_bcast(jnp.exp(g_ref[rows, :]))
    kt = _col_bcast(k_ref[rows, :].astype(F32))
    for b in range(bs):
        sn = ft[:, b:b + 1] * s_ref[b] + kt[:, b:b + 1] * v[b:b + 1, :]
        sn_ref[b] = sn
        o_scr[b:b + 1, :] = jnp.dot(q, sn.astype(BF16), preferred_element_type=F32)[b:b + 1, :]
    og_ref[rows, :] = (_rms(o_scr[...]) * hgain_ref[...] * gate_ref[rows, :].astype(F32)).astype(BF16)


def _hgrn_sample(pb, g, k, hgain, state, og, new_state, layer, n_sample, sample_blk):
    D = pb.shape[1] // 3
    H = D // HA_DK
    bs = 32 if n_sample % 32 == 0 else 16
    assert n_sample % bs == 0 and n_sample <= LANE
    sp = LANE

    def sec(col0):
        return pl.BlockSpec((sp, LANE), lambda h, i: (sample_blk, col0 + h))

    state_spec = pl.BlockSpec((None, bs, None, HA_DK, LANE), lambda h, i: (layer, i, h, 0, 0))
    any_spec = pl.BlockSpec(memory_space=pl.ANY)
    args = [pb, pb, pb, g, k, hgain.reshape(1, D), state, og]
    in_specs = [sec(0), sec(H), sec(2 * H), sec(0), sec(0),
                pl.BlockSpec((1, LANE), lambda h, i: (0, h)), state_spec, any_spec]
    aliases = {7: 0}
    if new_state is None:
        def body(q, gt, v, g_, k_, hg, s, og_in, og_out, sn_out, o_scr):
            _hgrn_sample_body(q, gt, v, g_, k_, hg, s, og_in, None, og_out, sn_out, o_scr, bs=bs)
    else:
        args.append(new_state)
        in_specs.append(any_spec)
        aliases[8] = 1
        body = functools.partial(_hgrn_sample_body, bs=bs)
    return pl.pallas_call(
        body,
        grid=(H, n_sample // bs),
        in_specs=in_specs,
        out_specs=[pl.BlockSpec((sp, LANE), lambda h, i: (sample_blk, h)), state_spec],
        out_shape=[jax.ShapeDtypeStruct(og.shape, og.dtype), jax.ShapeDtypeStruct(state.shape, F32)],
        scratch_shapes=[pltpu.VMEM((bs, LANE), F32)],
        input_output_aliases=aliases,
        compiler_params=_cparams(2),
        name="hgrn_sample",
    )(*args)


def _rotary(x, cos, sin):
    half = x.shape[-1] // 2
    x1, x2 = x[:, :half], x[:, half:]
    return jnp.concatenate([x1 * cos - x2 * sin, x1 * sin + x2 * cos], axis=-1)


def _ret_chunks(qs, ks, vs, gates, cos, sin, lg, sts, n_valid):
    n = range(len(qs))
    C, dk = qs[0].shape
    t_col = lax.broadcasted_iota(jnp.int32, (C, 1), 0)
    t_row = lax.broadcasted_iota(jnp.int32, (1, C), 1)
    n_col = jnp.minimum(t_col + 1, n_valid).astype(F32)
    n_row = jnp.minimum(t_row + 1, n_valid).astype(F32)
    n_last = float(min(C, n_valid))
    decay = jnp.where(t_col >= t_row, jnp.exp(lg * (n_col - n_row)), 0.0)
    q_scale = jnp.exp(lg * n_col)
    k_scale = jnp.exp(lg * (n_last - n_col))
    k_norm = dk ** -0.5
    if n_valid < C:
        k_norm = k_norm * (t_col < n_valid).astype(F32)
    q = [_rotary(t.astype(F32), cos, sin) for t in qs]
    k = [_rotary(t.astype(F32), cos, sin) * k_norm for t in ks]
    scores = [lax.dot_general(q[b].astype(BF16), k[b].astype(BF16), _NT, preferred_element_type=F32) * decay
              for b in n]
    o = [jnp.dot((q[b] * q_scale).astype(BF16), sts[b].astype(BF16), preferred_element_type=F32) for b in n]
    o = [o[b] + jnp.dot(scores[b].astype(BF16), vs[b], preferred_element_type=F32) for b in n]
    st_new = [jnp.exp(lg * n_last) * sts[b] + lax.dot_general((k[b] * k_scale).astype(BF16), vs[b], _TN,
                                                              preferred_element_type=F32) for b in n]
    og = [(_rms(o[b]) * gates[b].astype(F32)).astype(BF16) for b in n]
    return og, st_new


def _ret_prompt_body(lg_ref, q_ref, k_ref, v_ref, gate_ref, cos_ref, sin_ref, og_ref, so_ref, st_ref,
                     *, n_batch, n_chunks):
    C = CHUNK
    lg = lg_ref[pl.program_id(0)]
    s = pl.program_id(1)

    def step(rows, sts, n_valid):
        return _ret_chunks([q_ref[r, :] for r in rows], [k_ref[r, :] for r in rows], [v_ref[r, :] for r in rows],
                           [gate_ref[r, :] for r in rows], cos_ref[...], sin_ref[...], lg, sts, n_valid)

    @pl.when(s == 0)
    def _():
        og, st_new = step([pl.ds(C, C)], [jnp.zeros(st_ref.shape[1:], F32)], N_META)
        og_ref[...] = jnp.zeros_like(og_ref)
        og_ref[pl.ds(C, C), :] = og[0]
        for b in range(n_batch):
            st_ref[b] = st_new[0]

    @pl.when(s > 0)
    def _():
        og, st_new = step([pl.ds(b * C, C) for b in range(n_batch)], [st_ref[b] for b in range(n_batch)], C)
        for b in range(n_batch):
            og_ref[pl.ds(b * C, C), :] = og[b]
            st_ref[b] = st_new[b]

    @pl.when(s == n_chunks)
    def _():
        so_ref[...] = st_ref[...]


def _ret_prompt(p, log_gamma, cos_tab, sin_tab, n_batch, seq):
    R = p.shape[0]
    D = p.shape[1] // 6
    H = HB_HEADS
    dk = D // H
    dv = 2 * D // H
    C = CHUNK
    B = n_batch
    n_chunks = seq // C
    qk_off = 2 * D // dk
    v_off = 4 * D // dv

    def spec(width, col0):
        return pl.BlockSpec((B * C, width), lambda h, s: (jnp.where(s == 0, n_chunks, s - 1), col0 + h))

    tab_spec = pl.BlockSpec((C, dk // 2), lambda h, s: (s, 0))
    return pl.pallas_call(
        functools.partial(_ret_prompt_body, n_batch=B, n_chunks=n_chunks),
        grid=(H, 1 + n_chunks),
        in_specs=[pl.BlockSpec(memory_space=pltpu.SMEM),
                  spec(dk, qk_off), spec(dk, qk_off + H), spec(dv, v_off), spec(dv, 0),
                  tab_spec, tab_spec],
        out_specs=[spec(dv, 0), pl.BlockSpec((B, None, dk, dv), lambda h, s: (0, h, 0, 0))],
        out_shape=[jax.ShapeDtypeStruct((R, 2 * D), BF16), jax.ShapeDtypeStruct((B, H, dk, dv), F32)],
        scratch_shapes=[pltpu.VMEM((B, dk, dv), F32)],
        compiler_params=_cparams(2),
        name="ret_prompt",
    )(log_gamma, p, p, p, p, cos_tab, sin_tab)


def _ret_sample_body(lg_ref, q_ref, k_ref, v_ref, gate_ref, cos_ref, sin_ref, s_ref, og_in_ref, sn_in_ref,
                     og_ref, sn_ref, o_scr, *, bs):
    del og_in_ref, sn_in_ref
    h = pl.program_id(0)
    i = pl.program_id(1)

    @pl.when(i == 0)
    def _():
        og_ref[...] = jnp.zeros_like(og_ref)

    rows = pl.ds(pl.multiple_of(i * bs, bs), bs)
    gamma = jnp.exp(lg_ref[h])
    cos, sin = cos_ref[...], sin_ref[...]
    dk = q_ref.shape[-1]
    q = _rotary(q_ref[rows, :].astype(F32), cos, sin)
    k = _rotary(k_ref[rows, :].astype(F32), cos, sin) * (dk ** -0.5)
    v = v_ref[rows, :].astype(F32)
    qt = jnp.concatenate([_col_bcast(q[:, :LANE]), _col_bcast(q[:, LANE:])], axis=0)
    kt = jnp.concatenate([_col_bcast(k[:, :LANE]), _col_bcast(k[:, LANE:])], axis=0)
    for b in range(bs):
        sn = gamma * s_ref[b] + kt[:, b:b + 1] * v[b:b + 1, :]
        sn_ref[b] = sn
        o_scr[b:b + 1, :] = jnp.sum(qt[:, b:b + 1] * sn, axis=0, keepdims=True)
    og_ref[rows, :] = (_rms(o_scr[...]) * gate_ref[rows, :].astype(F32)).astype(BF16)


def _ret_sample(p, log_gamma, cos_s, sin_s, state, og, new_state, layer, n_sample, sample_blk):
    D = p.shape[1] // 6
    H = HB_HEADS
    dk = D // H
    dv = 2 * D // H
    bs = 8
    assert n_sample % bs == 0 and n_sample <= LANE and dk == 2 * LANE
    sp = LANE
    qk_off = 2 * D // dk
    v_off = 4 * D // dv
    state_spec = pl.BlockSpec((None, bs, None, dk, dv), lambda h, i: (layer, i, h, 0, 0))
    any_spec = pl.BlockSpec(memory_space=pl.ANY)
    args = [log_gamma, p, p, p, p, cos_s, sin_s, state, og]
    in_specs = [
        pl.BlockSpec(memory_space=pltpu.SMEM),
        pl.BlockSpec((sp, dk), lambda h, i: (sample_blk, qk_off + h)),
        pl.BlockSpec((sp, dk), lambda h, i: (sample_blk, qk_off + H + h)),
        pl.BlockSpec((sp, dv), lambda h, i: (sample_blk, v_off + h)),
        pl.BlockSpec((sp, dv), lambda h, i: (sample_blk, h)),
        pl.BlockSpec((1, dk // 2), lambda h, i: (0, 0)),
        pl.BlockSpec((1, dk // 2), lambda h, i: (0, 0)),
        state_spec, any_spec]
    aliases = {8: 0}
    if new_state is None:
        def body(lg, q, k, v, gt, cs, sn_, s, og_in, og_out, sn_out, o_scr):
            _ret_sample_body(lg, q, k, v, gt, cs, sn_, s, og_in, None, og_out, sn_out, o_scr, bs=bs)
    else:
        args.append(new_state)
        in_specs.append(any_spec)
        aliases[9] = 1
        body = functools.partial(_ret_sample_body, bs=bs)
    return pl.pallas_call(
        body,
        grid=(H, n_sample // bs),
        in_specs=in_specs,
        out_specs=[pl.BlockSpec((sp, dv), lambda h, i: (sample_blk, h)), state_spec],
        out_shape=[jax.ShapeDtypeStruct(og.shape, og.dtype), jax.ShapeDtypeStruct(state.shape, F32)],
        scratch_shapes=[pltpu.VMEM((bs, dv), F32)],
        input_output_aliases=aliases,
        compiler_params=_cparams(2),
        name="ret_sample",
    )(*args)


def _rope_tables(pos, half):
    inv = ROPE_BASE ** (-jnp.linspace(0.0, 1.0, half, dtype=F32))
    ang = pos.astype(F32)[:, None] * inv[None, :]
    return jnp.cos(ang), jnp.sin(ang)


def kernel(x_prompt, x_sample, state_hgrn, state_ret, meta_tokens, norm_ffn, ffn_w_gate, ffn_w_up, ffn_w_down, norm_mix, hg_wq, hg_wf, hg_wi, hg_wg, hg_wo, hg_norm, hg_lb_logits, rt_wq, rt_wk, rt_wv, rt_wg, rt_wo, norm_final):
    B, L, D = x_prompt.shape
    NS = x_sample.shape[0]
    depth = norm_mix.shape[0]
    n_meta = meta_tokens.shape[0]
    C = CHUNK
    assert n_meta == N_META and x_sample.shape[1] == 1 and L % C == 0 and NS <= LANE
    n_chunks = L // C

    tp = B * L
    sp = LANE
    rows0 = tp + sp + C
    tm = next(t for t in ROW_TILES if _round_up(rows0, t) - rows0 < LANE) if rows0 >= ROW_TILES[0] else rows0
    R = _round_up(rows0, tm)
    x = jnp.concatenate([
        x_prompt.reshape(B, n_chunks, C, D).transpose(1, 0, 2, 3).reshape(tp, D),
        x_sample.reshape(NS, D), jnp.zeros((sp - NS, D), F32),
        meta_tokens.astype(F32), jnp.zeros((R - tp - sp - n_meta, D), F32)], axis=0)
    sample_blk = tp // LANE

    wg = _cast_ffn_in(ffn_w_gate)
    wu = _cast_ffn_in(ffn_w_up)
    wd = _cast_ffn_out(ffn_w_down)

    half = D // HB_HEADS // 2
    cos_p, sin_p = _rope_tables(jnp.arange(n_meta + L, dtype=jnp.int32), half)
    cos_s, sin_s = _rope_tables(PAST_LEN + jnp.arange(1, dtype=jnp.int32), half)

    def chunk_table(t):
        meta = jnp.concatenate([t[:n_meta], jnp.zeros((C - n_meta, half), F32)], axis=0)
        return jnp.concatenate([meta, t[n_meta:]], axis=0)

    cos_tab, sin_tab = chunk_table(cos_p), chunk_table(sin_p)
    log_gamma = jnp.log1p(-jnp.exp2(-5.0 - jnp.arange(HB_HEADS, dtype=F32)))
    tables = _hgrn_tables(C)

    new_hgrn_p, new_ret_p = [], []
    new_hgrn_s = new_ret_s = None
    for i in range(depth):
        x = _ffn(x, norm_ffn[i, 0], wg, wu, wd, i, 0, None, tm)
        j = i // 2
        if i % 2 == 0:
            w_in = jnp.concatenate([hg_wq[j].astype(BF16), hg_wg[j].astype(BF16), hg_wi[j].astype(BF16)], axis=1)
            pb = _proj(x, norm_mix[i], w_in, 2 * D, tm)
            g, k = _gate_proj(x, norm_mix[i], hg_wf[j].astype(BF16), hg_lb_logits, j, tm)
            og, s_p = _hgrn_prompt(pb, g, k, hg_norm[j], B, L, tables)
            og, new_hgrn_s = _hgrn_sample(pb, g, k, hg_norm[j], state_hgrn, og, new_hgrn_s, j, NS, sample_blk)
            new_hgrn_p.append(s_p)
            x = _outproj(og, hg_wo[j].astype(BF16), x, tm)
        else:
            w_in = jnp.concatenate([rt_wg[j].astype(BF16), rt_wq[j].astype(BF16),
                                    rt_wk[j].astype(BF16), rt_wv[j].astype(BF16)], axis=1)
            pb = _proj(x, norm_mix[i], w_in, 2 * D, tm)
            og, s_p = _ret_prompt(pb, log_gamma, cos_tab, sin_tab, B, L)
            og, new_ret_s = _ret_sample(pb, log_gamma, cos_s, sin_s, state_ret, og, new_ret_s, j, NS, sample_blk)
            new_ret_p.append(s_p)
            x = _outproj(og, rt_wo[j].astype(BF16), x, tm)
        x = _ffn(x, norm_ffn[i, 1], wg, wu, wd, i, 1, norm_final if i == depth - 1 else None, tm)

    y_prompt = x[:tp].reshape(n_chunks, B, C, D).transpose(1, 0, 2, 3).reshape(B, L, D)
    y_sample = x[tp:tp + NS].reshape(NS, 1, D)
    return (y_prompt, y_sample, jnp.stack(new_hgrn_p), new_hgrn_s, jnp.stack(new_ret_p), new_ret_s)
```

```python
import functools
import math
from typing import NamedTuple

import numpy as np
import jax
import jax.numpy as jnp
from jax import lax
from jax.experimental import pallas as pl
from jax.experimental.pallas import tpu as pltpu

F32 = jnp.float32
BF16 = jnp.bfloat16

EPS = 1e-6
LB_FLOOR = 1e-30
ROPE_BASE = 10000.0
N_META = 16
PAST_LEN = 16384
HA_DK = 128
HB_HEADS = 8

LANE = 128
SUBLANE = 8
CHUNK = 128
HGRN_HEADS_PER_STEP = 2
SAMPLES_PER_STEP = SUBLANE
FF_TILE = 512
CAST_TILE = 256
PROJ_TILE = 2048
GATE_TILE = 1024
OUT_TILE = 1024
ROW_TILES = (768, 512, 256, 128)
VMEM_LIMIT = 56 * 1024 * 1024

_NT = (((1,), (1,)), ((), ()))
_TN = (((0,), (0,)), ((), ()))


def _round_up(a, m):
    return (a + m - 1) // m * m


def _cparams(n_axes):
    return pltpu.CompilerParams(dimension_semantics=("arbitrary",) * n_axes,
                                vmem_limit_bytes=VMEM_LIMIT)


def _rms(x):
    return x * lax.rsqrt(jnp.mean(x * x, axis=-1, keepdims=True) + EPS)


def _sigmoid(x):
    return 1.0 / (1.0 + jnp.exp(-x))


def _silu(x):
    return x * _sigmoid(x)


def _cast_tile(w, axis, tile, n_tiles, d_ff):
    idx = lax.broadcasted_iota(jnp.int32, w.shape, axis)
    limit = jnp.where(tile == n_tiles - 1, d_ff - (n_tiles - 1) * CAST_TILE, CAST_TILE)
    return jnp.where(idx < limit, w, 0.0).astype(BF16)


def _cast_first_body(wg_ref, wu_ref, wd_ref, og_ref, ou_ref, od_ref, *, d_ff):
    j = pl.program_id(0)
    n = pl.num_programs(0)
    og_ref[...] = _cast_tile(wg_ref[...], 1, j, n, d_ff)
    ou_ref[...] = _cast_tile(wu_ref[...], 1, j, n, d_ff)
    od_ref[...] = _cast_tile(wd_ref[...], 0, j, n, d_ff)


def _cast_specs(D, index):
    per_ff = FF_TILE // CAST_TILE

    def src(kind):
        def f(*ids):
            l, i, t = index(kind, *ids)
            return (l, i, t, 0) if kind == 2 else (l, i, 0, t)
        shape = (None, None, CAST_TILE, D) if kind == 2 else (None, None, D, CAST_TILE)
        return pl.BlockSpec(shape, f)

    def dst(kind):
        def f(*ids):
            l, i, t = index(kind, *ids)
            return (l, i, t, 0) if kind == 2 else (l, i, t // per_ff, 0, t % per_ff)
        shape = (None, None, CAST_TILE, D) if kind == 2 else (None, None, None, D, CAST_TILE)
        return pl.BlockSpec(shape, f)

    return [src(k) for k in range(3)], [dst(k) for k in range(3)]


def _cast_first(w_f32, layer, idx):
    wg, wu, wd = w_f32
    n_l, n_i, D, F = wg.shape
    f_pad = _round_up(F, FF_TILE)
    n_tiles = f_pad // CAST_TILE
    assert 0 < F - (n_tiles - 1) * CAST_TILE <= CAST_TILE
    srcs, dsts = _cast_specs(D, lambda kind, j: (layer, idx, j))
    return pl.pallas_call(
        functools.partial(_cast_first_body, d_ff=F),
        grid=(n_tiles,),
        in_specs=srcs,
        out_specs=dsts,
        out_shape=[jax.ShapeDtypeStruct((n_l, n_i, f_pad // FF_TILE, D, FF_TILE), BF16)] * 2 + [
            jax.ShapeDtypeStruct((n_l, n_i, f_pad, D), BF16)],
        compiler_params=_cparams(1),
        name="cast_first",
    )(wg, wu, wd)


class CastJob(NamedTuple):
    targets: tuple
    d_ff: int
    n_tiles: int

    @property
    def span(self):
        return len(self.targets) * self.n_tiles

    def index(self, kind, t):
        tt = jnp.clip(t - kind * self.span, 0, self.span - 1)
        which = tt // self.n_tiles
        layer, idx = self.targets[0]
        for n, (l, i) in enumerate(self.targets[1:], 1):
            layer = jnp.where(which == n, l, layer)
            idx = jnp.where(which == n, i, idx)
        return layer, idx, tt % self.n_tiles


def _cast_step(job, t, src_refs, dst_refs):
    for kind in range(3):
        def convert(kind=kind):
            tile = (t - kind * job.span) % job.n_tiles
            dst_refs[kind][...] = _cast_tile(src_refs[kind][...], 0 if kind == 2 else 1, tile,
                                             job.n_tiles, job.d_ff)
        pl.when((t >= kind * job.span) & (t < (kind + 1) * job.span))(convert)


def _ffn_body(x_ref, gain_ref, wg_ref, wu_ref, wd_ref, fgain_ref, o_ref, h_ref, *, final_norm):
    j = pl.program_id(1)

    @pl.when(j == 0)
    def _():
        x = x_ref[...]
        h_ref[...] = (_rms(x) * gain_ref[...]).astype(BF16)
        o_ref[...] = x

    h = h_ref[...]
    g = jnp.dot(h, wg_ref[...], preferred_element_type=F32)
    u = jnp.dot(h, wu_ref[...], preferred_element_type=F32)
    a = (0.5 * _silu(g) * u).astype(BF16)
    o_ref[...] += jnp.dot(a, wd_ref[...], preferred_element_type=F32)

    if final_norm:
        @pl.when(j == pl.num_programs(1) - 1)
        def _():
            o_ref[...] = _rms(o_ref[...]) * fgain_ref[...]


def _ffn(x, gain, w_bf, layer, idx, fgain, tm):
    wg, wu, wd = w_bf
    R, D = x.shape
    nj = wg.shape[2]
    final_norm = fgain is not None
    if fgain is None:
        fgain = gain
    return pl.pallas_call(
        functools.partial(_ffn_body, final_norm=final_norm),
        grid=(R // tm, nj),
        in_specs=[
            pl.BlockSpec((tm, D), lambda i, j: (i, 0)),
            pl.BlockSpec((1, D), lambda i, j: (0, 0)),
            pl.BlockSpec((None, None, None, D, FF_TILE), lambda i, j: (layer, idx, j, 0, 0)),
            pl.BlockSpec((None, None, None, D, FF_TILE), lambda i, j: (layer, idx, j, 0, 0)),
            pl.BlockSpec((None, None, FF_TILE, D), lambda i, j: (layer, idx, j, 0)),
            pl.BlockSpec((1, D), lambda i, j: (0, 0)),
        ],
        out_specs=pl.BlockSpec((tm, D), lambda i, j: (i, 0)),
        out_shape=jax.ShapeDtypeStruct((R, D), F32),
        scratch_shapes=[pltpu.VMEM((tm, D), BF16)],
        compiler_params=_cparams(2),
        name="ffn",
    )(x, gain.reshape(1, D), wg, wu, wd, fgain.reshape(1, D))


def _proj_body(x_ref, gain_ref, w_ref, o_ref, h_ref, *, n_silu):
    j = pl.program_id(1)

    @pl.when(j == 0)
    def _():
        h_ref[...] = (_rms(x_ref[...]) * gain_ref[...]).astype(BF16)

    y = jnp.dot(h_ref[...], w_ref[...], preferred_element_type=F32)
    if n_silu > 0:
        y = jnp.where(j < n_silu, _silu(y), y)
    o_ref[...] = y.astype(BF16)


def _proj(x, gain, w, n_silu_cols, tm):
    R, D = x.shape
    N = w.shape[1]
    return pl.pallas_call(
        functools.partial(_proj_body, n_silu=n_silu_cols // PROJ_TILE),
        grid=(R // tm, N // PROJ_TILE),
        in_specs=[
            pl.BlockSpec((tm, D), lambda i, j: (i, 0)),
            pl.BlockSpec((1, D), lambda i, j: (0, 0)),
            pl.BlockSpec((D, PROJ_TILE), lambda i, j: (0, j)),
        ],
        out_specs=pl.BlockSpec((tm, PROJ_TILE), lambda i, j: (i, j)),
        out_shape=jax.ShapeDtypeStruct((R, N), BF16),
        scratch_shapes=[pltpu.VMEM((tm, D), BF16)],
        compiler_params=_cparams(2),
        name="proj",
    )(x, gain.reshape(1, D), w)


def _lower_bound(logits, layer):
    e = jnp.exp(logits - jnp.max(logits, axis=0, keepdims=True))
    probs = e / jnp.sum(e, axis=0, keepdims=True)
    cs = probs[0:1]
    for i in range(1, layer + 1):
        cs = cs + probs[i:i + 1]
    return cs - probs[0:1]


def _gate_proj_body(x_ref, gain_ref, w_ref, logit_ref, g_ref, k_ref, h_ref, *, layer):
    j = pl.program_id(1)

    @pl.when(j == 0)
    def _():
        h_ref[...] = (_rms(x_ref[...]) * gain_ref[...]).astype(BF16)

    z = jnp.dot(h_ref[...], w_ref[...], preferred_element_type=F32)
    lb = _lower_bound(logit_ref[...], layer)
    t = jnp.exp(-jnp.abs(z))
    r = 1.0 / (1.0 + t)
    tr = t * r
    pos = z >= 0.0
    g_ref[...] = jnp.log(jnp.maximum(lb, LB_FLOOR) + (1.0 - lb) * jnp.where(pos, r, tr))
    k_ref[...] = ((1.0 - lb) * jnp.where(pos, tr, r)).astype(BF16)


def _gate_proj(x, gain, w, logits, layer, tm):
    R, D = x.shape
    N = w.shape[1]
    spec = pl.BlockSpec((tm, GATE_TILE), lambda i, j: (i, j))
    return pl.pallas_call(
        functools.partial(_gate_proj_body, layer=layer),
        grid=(R // tm, N // GATE_TILE),
        in_specs=[
            pl.BlockSpec((tm, D), lambda i, j: (i, 0)),
            pl.BlockSpec((1, D), lambda i, j: (0, 0)),
            pl.BlockSpec((D, GATE_TILE), lambda i, j: (0, j)),
            pl.BlockSpec((logits.shape[0], GATE_TILE), lambda i, j: (0, j)),
        ],
        out_specs=[spec, spec],
        out_shape=[jax.ShapeDtypeStruct((R, N), F32), jax.ShapeDtypeStruct((R, N), BF16)],
        scratch_shapes=[pltpu.VMEM((tm, D), BF16)],
        compiler_params=_cparams(2),
        name="gate_proj",
    )(x, gain.reshape(1, D), w, logits)


def _outproj_body(a_ref, w_ref, x_ref, o_ref):
    o_ref[...] = x_ref[...] + jnp.dot(a_ref[...], w_ref[...], preferred_element_type=F32)


def _outproj(a, w, x, tm):
    R, K = a.shape
    D = w.shape[1]
    return pl.pallas_call(
        _outproj_body,
        grid=(R // tm, D // OUT_TILE),
        in_specs=[
            pl.BlockSpec((tm, K), lambda i, j: (i, 0)),
            pl.BlockSpec((K, OUT_TILE), lambda i, j: (0, j)),
            pl.BlockSpec((tm, OUT_TILE), lambda i, j: (i, j)),
        ],
        out_specs=pl.BlockSpec((tm, OUT_TILE), lambda i, j: (i, j)),
        out_shape=jax.ShapeDtypeStruct((R, D), F32),
        compiler_params=_cparams(2),
        name="outproj",
    )(a, w, x)


def _col_bcast(x):
    bs = x.shape[0]
    return jnp.concatenate([x] * (LANE // bs), axis=0).T


def _sample_rows(s):
    grp = s - 1
    n = SAMPLES_PER_STEP
    rows = pl.ds(pl.multiple_of(grp * n, n), n)
    rows2 = pl.ds(pl.multiple_of((grp // 2) * 2 * n, 2 * n), 2 * n)
    return rows, rows2, (grp % 2) == 1


def _half(x2, second):
    n = SAMPLES_PER_STEP
    return jnp.where(second, x2[n:], x2[:n])


def _mixer_call(body, name, grid, in_specs, args, out_specs, out_shape, scratch_shapes, state, new_state,
                w_f32, w_bf, job, D):
    n_s = grid[1]
    srcs, dsts = _cast_specs(D, lambda kind, h, s: job.index(kind, h * n_s + s))
    any_spec = pl.BlockSpec(memory_space=pl.ANY)
    in_specs = list(in_specs) + srcs
    args = list(args) + list(w_f32)
    out_specs = list(out_specs) + dsts
    out_shape = list(out_shape) + [jax.ShapeDtypeStruct(w.shape, w.dtype) for w in w_bf]
    n_out = len(out_shape)
    aliases = {}
    if new_state is not None:
        aliases[len(args)] = n_out - 4
        in_specs.append(any_spec)
        args.append(new_state)
    for n, w in enumerate(w_bf):
        aliases[len(args)] = n_out - 3 + n
        in_specs.append(any_spec)
        args.append(w)
    return pl.pallas_call(
        functools.partial(body, n_alias=len(aliases)),
        grid=grid, in_specs=in_specs, out_specs=out_specs, out_shape=out_shape,
        scratch_shapes=scratch_shapes, input_output_aliases=aliases,
        compiler_params=_cparams(2), name=name,
    )(*args)


def _hgrn_tables(C):
    n_lvl = int(math.log2(C))
    assert 1 << n_lvl == C
    n_fine = min(n_lvl, 3)
    r = np.arange(C)
    msum = np.zeros((n_fine + 1, C, C), np.float32)
    pair = np.zeros((n_lvl + 1, C, C), np.float32)
    u = r[None, :]
    for l in range(n_lvl):
        half = 1 << l
        blk = 2 * half
        m = r - r % blk + half - 1
        up = (r % blk) >= half
        if l < n_fine:
            msum[l] = np.where(up[:, None], (u > m[:, None]) & (u <= r[:, None]),
                               (u > r[:, None]) & (u <= m[:, None]))
        pair[l] = ((r[:, None] // blk) == (r[None, :] // blk)) & up[:, None] & ~up[None, :]
    pair[n_lvl] = np.eye(C)
    msum[n_fine] = u <= r[:, None]
    return jnp.asarray(msum.reshape(-1, C), BF16), jnp.asarray(pair, F32)


def _split3(g):
    hi = g.astype(BF16)
    r1 = g - hi.astype(F32)
    mid = r1.astype(BF16)
    lo = (r1 - mid.astype(F32)).astype(BF16)
    return hi, mid, lo


def _hgrn_chunks(qs, gs, ks, vs, gates, hgains, sts, msum, pair_ref, valid, side=()):
    side = list(side)
    n_lvl = pair_ref.shape[0] - 1
    per_phase = -(-len(side) // (n_lvl + 1))

    def run_side():
        for _ in range(min(per_phase, len(side))):
            side.pop(0)()

    n = range(len(qs))
    C = qs[0].shape[0]
    n_fine = msum.shape[0] // C - 1
    qf = [q.astype(F32) for q in qs]
    kf = [k.astype(F32) for k in ks]
    if valid is not None:
        gs = [g * valid for g in gs]
        kf = [k * valid for k in kf]
        ks = [k.astype(BF16) for k in kf]
    e3 = [jnp.dot(msum, jnp.concatenate(_split3(g), axis=1), preferred_element_type=F32) for g in gs]
    e = [t[:, :LANE] + t[:, LANE:2 * LANE] + t[:, 2 * LANE:] for t in e3]
    c = [t[n_fine * C:] for t in e]
    x_fine = [jnp.exp(t[:n_fine * C]) for t in e]
    run_side()

    scores = [lax.dot_general(qs[b], ks[b], _NT, preferred_element_type=F32) * pair_ref[n_lvl] for b in n]
    for l in range(n_lvl):
        if l < n_fine:
            xl = [t[l * C:(l + 1) * C] for t in x_fine]
        else:
            half = 1 << l
            up = lax.broadcasted_iota(jnp.int32, (1, 2 * half, 1), 1) >= half
            xl = []
            for t in c:
                c3 = t.reshape(C // (2 * half), 2 * half, LANE)
                cm = c3[:, half - 1:half, :]
                xl.append(jnp.exp(jnp.where(up, c3 - cm, cm - c3)).reshape(C, LANE))
        scores = [scores[b] + lax.dot_general((qf[b] * xl[b]).astype(BF16), (kf[b] * xl[b]).astype(BF16), _NT,
                                              preferred_element_type=F32) * pair_ref[l] for b in n]
        run_side()
    while side:
        side.pop(0)()
    xc = [jnp.exp(t) for t in c]
    xt = [jnp.exp(t[C - 1:C, :] - t) for t in c]
    o = [lax.dot_general((qf[b] * xc[b]).astype(BF16), sts[b].astype(BF16), _NT, preferred_element_type=F32)
         for b in n]
    o = [o[b] + jnp.dot(scores[b].astype(BF16), vs[b], preferred_element_type=F32) for b in n]
    st_new = [sts[b] * xc[b][C - 1:C, :] + lax.dot_general(vs[b], (kf[b] * xt[b]).astype(BF16), _TN,
                                                           preferred_element_type=F32) for b in n]
    og = [(_rms(o[b]) * hgains[b] * gates[b].astype(F32)).astype(BF16) for b in n]
    return og, st_new


def _hgrn_mixer_body(*refs, n_batch, n_chunks, n_heads, job, n_alias):
    (q_ref, gate_ref, v_ref, g_ref, k_ref, hgain_ref, msum_ref, pair_ref,
     qs_ref, gates_ref, vs_ref, gs_ref, ks_ref, s_ref) = refs[:14]
    w_src = refs[14:17]
    og_ref, so_ref, ogs_ref, sn_ref = refs[17 + n_alias:21 + n_alias]
    w_dst = refs[21 + n_alias:24 + n_alias]
    st_ref, o_scr = refs[24 + n_alias:]
    C = CHUNK
    s = pl.program_id(1)
    _cast_step(job, pl.program_id(0) * pl.num_programs(1) + s, w_src, w_dst)

    def step(seqs, sts, valid, side=()):
        tiles = [(r, pl.ds(hh * LANE, LANE)) for r, hh in seqs]
        return _hgrn_chunks([q_ref[t] for t in tiles], [g_ref[t] for t in tiles], [k_ref[t] for t in tiles],
                            [v_ref[t] for t in tiles], [gate_ref[t] for t in tiles],
                            [hgain_ref[:, t[1]] for t in tiles], sts, msum_ref[...], pair_ref, valid, side)

    @pl.when(s == 0)
    def _():
        valid = (lax.broadcasted_iota(jnp.int32, (C, 1), 0) < N_META).astype(F32)
        seqs = [(pl.ds(C, C), hh) for hh in range(n_heads)]
        og, st_new = step(seqs, [jnp.zeros((LANE, HA_DK), F32)] * n_heads, valid)
        og_ref[...] = jnp.zeros_like(og_ref)
        for hh in range(n_heads):
            og_ref[pl.ds(C, C), pl.ds(hh * LANE, LANE)] = og[hh]
            for b in range(n_batch):
                st_ref[hh * n_batch + b] = st_new[hh]

    @pl.when(s > 0)
    def _():
        rows, rows2, second = _sample_rows(s)
        side = []
        for hh in range(n_heads):
            lanes = pl.ds(hh * LANE, LANE)
            q2 = qs_ref[rows2, lanes]
            v1 = _half(vs_ref[rows2, lanes].astype(F32), second)
            ft = _col_bcast(jnp.exp(gs_ref[rows, lanes]))
            kt = _col_bcast(_half(ks_ref[rows2, lanes].astype(F32), second))

            def update(b, hh=hh, q2=q2, v1=v1, ft=ft, kt=kt):
                sn = ft[:, b:b + 1] * s_ref[b, hh] + kt[:, b:b + 1] * v1[b:b + 1, :]
                sn_ref[b, hh] = sn
                res = jnp.dot(q2, sn.astype(BF16), preferred_element_type=F32)
                o_scr[hh, b:b + 1, :] = _half(res, second)[b:b + 1, :]

            def finish(hh=hh, lanes=lanes):
                gate1 = _half(gates_ref[rows2, lanes].astype(F32), second)
                ogs_ref[rows, lanes] = _rms(o_scr[hh]) * hgain_ref[:, lanes] * gate1

            side += [functools.partial(update, b) for b in range(SAMPLES_PER_STEP)] + [finish]

        seqs = [(pl.ds(b * C, C), hh) for hh in range(n_heads) for b in range(n_batch)]
        og, st_new = step(seqs, [st_ref[i] for i in range(len(seqs))], None, side)
        for i, (r, hh) in enumerate(seqs):
            og_ref[r, pl.ds(hh * LANE, LANE)] = og[i]
            st_ref[i] = st_new[i]

    @pl.when(s == n_chunks)
    def _():
        for hh in range(n_heads):
            for b in range(n_batch):
                so_ref[b, hh] = st_ref[hh * n_batch + b].T


def _hgrn_mixer(pb, g, k, hgain, state, new_state, layer, n_batch, seq, tables, sample_blk, w_f32, w_bf, job):
    R = pb.shape[0]
    D = pb.shape[1] // 3
    H = D // HA_DK
    C = CHUNK
    B = n_batch
    hp = HGRN_HEADS_PER_STEP
    n_chunks = seq // C
    n_sample = state.shape[1]
    msum, pair = tables
    grid = (H // hp, 1 + n_chunks)
    assert B >= 2 and R >= n_chunks * B * C + 2 * C and H % hp == 0
    assert n_sample == SAMPLES_PER_STEP * n_chunks and 3 * job.span <= grid[0] * grid[1]

    def spec(col0):
        return pl.BlockSpec((B * C, hp * LANE),
                            lambda h, s: (jnp.where(s == 0, n_chunks, s - 1), col0 // hp + h))

    def sspec(col0):
        return pl.BlockSpec((LANE, hp * LANE), lambda h, s: (sample_blk, col0 // hp + h))

    state_spec = pl.BlockSpec((None, SAMPLES_PER_STEP, hp, HA_DK, LANE),
                              lambda h, s: (layer, jnp.maximum(s - 1, 0), h, 0, 0))
    out = _mixer_call(
        functools.partial(_hgrn_mixer_body, n_batch=B, n_chunks=n_chunks, n_heads=hp, job=job),
        "hgrn_mixer", grid,
        in_specs=[spec(0), spec(H), spec(2 * H), spec(0), spec(0),
                  pl.BlockSpec((1, hp * LANE), lambda h, s: (0, h)),
                  pl.BlockSpec(msum.shape, lambda h, s: (0, 0)),
                  pl.BlockSpec(pair.shape, lambda h, s: (0, 0, 0)),
                  sspec(0), sspec(H), sspec(2 * H), sspec(0), sspec(0), state_spec],
        args=[pb, pb, pb, g, k, hgain.reshape(1, D), msum, pair, pb, pb, pb, g, k, state],
        out_specs=[spec(0), pl.BlockSpec((B, hp, HA_DK, LANE), lambda h, s: (0, h, 0, 0)),
                   pl.BlockSpec((LANE, hp * LANE), lambda h, s: (0, h)), state_spec],
        out_shape=[jax.ShapeDtypeStruct((R, D), BF16), jax.ShapeDtypeStruct((B, H, HA_DK, LANE), F32),
                   jax.ShapeDtypeStruct((LANE, D), F32), jax.ShapeDtypeStruct(state.shape, F32)],
        scratch_shapes=[pltpu.VMEM((hp * B, LANE, HA_DK), F32), pltpu.VMEM((hp, SAMPLES_PER_STEP, LANE), F32)],
        state=state, new_state=new_state, w_f32=w_f32, w_bf=w_bf, job=job, D=D)
    return out[0], out[1], out[2], out[3], tuple(out[4:])


def _rotary(x, cos, sin):
    half = x.shape[-1] // 2
    x1, x2 = x[:, :half], x[:, half:]
    return jnp.concatenate([x1 * cos - x2 * sin, x1 * sin + x2 * cos], axis=-1)


def _ret_chunks(qs, ks, vs, gates, cos, sin, lg, sts, n_valid):
    n = range(len(qs))
    C, dk = qs[0].shape
    t_col = lax.broadcasted_iota(jnp.int32, (C, 1), 0)
    t_row = lax.broadcasted_iota(jnp.int32, (1, C), 1)
    n_col = jnp.minimum(t_col + 1, n_valid).astype(F32)
    n_row = jnp.minimum(t_row + 1, n_valid).astype(F32)
    n_last = float(min(C, n_valid))
    decay = jnp.where(t_col >= t_row, jnp.exp(lg * (n_col - n_row)), 0.0)
    q_scale = jnp.exp(lg * n_col)
    k_scale = jnp.exp(lg * (n_last - n_col))
    k_norm = dk ** -0.5
    if n_valid < C:
        k_norm = k_norm * (t_col < n_valid).astype(F32)
    q = [_rotary(t.astype(F32), cos, sin) for t in qs]
    k = [_rotary(t.astype(F32), cos, sin) * k_norm for t in ks]
    scores = [lax.dot_general(q[b].astype(BF16), k[b].astype(BF16), _NT, preferred_element_type=F32) * decay
              for b in n]
    o = [jnp.dot((q[b] * q_scale).astype(BF16), sts[b].astype(BF16), preferred_element_type=F32) for b in n]
    o = [o[b] + jnp.dot(scores[b].astype(BF16), vs[b], preferred_element_type=F32) for b in n]
    st_new = [jnp.exp(lg * n_last) * sts[b] + lax.dot_general((k[b] * k_scale).astype(BF16), vs[b], _TN,
                                                              preferred_element_type=F32) for b in n]
    og = [(_rms(o[b]) * gates[b].astype(F32)).astype(BF16) for b in n]
    return og, st_new


def _ret_mixer_body(*refs, n_batch, n_chunks, job, n_alias):
    (lg_ref, q_ref, k_ref, v_ref, gate_ref, cos_ref, sin_ref,
     qs_ref, ks_ref, vs_ref, gates_ref, coss_ref, sins_ref, s_ref) = refs[:14]
    w_src = refs[14:17]
    og_ref, so_ref, ogs_ref, sn_ref = refs[17 + n_alias:21 + n_alias]
    w_dst = refs[21 + n_alias:24 + n_alias]
    st_ref, o_scr = refs[24 + n_alias:]
    C = CHUNK
    lg = lg_ref[pl.program_id(0)]
    s = pl.program_id(1)
    _cast_step(job, pl.program_id(0) * pl.num_programs(1) + s, w_src, w_dst)

    def step(rows, sts, n_valid):
        return _ret_chunks([q_ref[r, :] for r in rows], [k_ref[r, :] for r in rows], [v_ref[r, :] for r in rows],
                           [gate_ref[r, :] for r in rows], cos_ref[...], sin_ref[...], lg, sts, n_valid)

    @pl.when(s == 0)
    def _():
        og, st_new = step([pl.ds(C, C)], [jnp.zeros(st_ref.shape[1:], F32)], N_META)
        og_ref[...] = jnp.zeros_like(og_ref)
        og_ref[pl.ds(C, C), :] = og[0]
        for b in range(n_batch):
            st_ref[b] = st_new[0]

    @pl.when(s > 0)
    def _():
        rows, rows2, second = _sample_rows(s)
        dk = qs_ref.shape[-1]
        gamma = jnp.exp(lg)
        cos, sin = coss_ref[...], sins_ref[...]
        q = _rotary(_half(qs_ref[rows2, :].astype(F32), second), cos, sin)
        k = _rotary(_half(ks_ref[rows2, :].astype(F32), second), cos, sin) * (dk ** -0.5)
        v = _half(vs_ref[rows2, :].astype(F32), second)
        qt = jnp.concatenate([_col_bcast(q[:, :LANE]), _col_bcast(q[:, LANE:])], axis=0)
        kt = jnp.concatenate([_col_bcast(k[:, :LANE]), _col_bcast(k[:, LANE:])], axis=0)
        for b in range(SAMPLES_PER_STEP):
            sn = gamma * s_ref[b] + kt[:, b:b + 1] * v[b:b + 1, :]
            sn_ref[b] = sn
            o_scr[b:b + 1, :] = jnp.sum(qt[:, b:b + 1] * sn, axis=0, keepdims=True)
        ogs_ref[rows, :] = _rms(o_scr[...]) * _half(gates_ref[rows2, :].astype(F32), second)

        og, st_new = step([pl.ds(b * C, C) for b in range(n_batch)], [st_ref[b] for b in range(n_batch)], C)
        for b in range(n_batch):
            og_ref[pl.ds(b * C, C), :] = og[b]
            st_ref[b] = st_new[b]

    @pl.when(s == n_chunks)
    def _():
        so_ref[...] = st_ref[...]


def _ret_mixer(p, log_gamma, cos_tab, sin_tab, cos_s, sin_s, state, new_state, layer, n_batch, seq,
               sample_blk, w_f32, w_bf, job):
    R = p.shape[0]
    D = p.shape[1] // 6
    H = HB_HEADS
    dk = D // H
    dv = 2 * D // H
    C = CHUNK
    B = n_batch
    n_chunks = seq // C
    n_sample = state.shape[1]
    qk_off = 2 * D // dk
    v_off = 4 * D // dv
    grid = (H, 1 + n_chunks)
    assert B >= 2 and R >= n_chunks * B * C + 2 * C and dk == 2 * LANE
    assert n_sample == SAMPLES_PER_STEP * n_chunks and 3 * job.span <= grid[0] * grid[1]

    def spec(width, col0):
        return pl.BlockSpec((B * C, width), lambda h, s: (jnp.where(s == 0, n_chunks, s - 1), col0 + h))

    def sspec(width, col0):
        return pl.BlockSpec((LANE, width), lambda h, s: (sample_blk, col0 + h))

    tab_spec = pl.BlockSpec((C, dk // 2), lambda h, s: (s, 0))
    pos_spec = pl.BlockSpec((1, dk // 2), lambda h, s: (0, 0))
    state_spec = pl.BlockSpec((None, SAMPLES_PER_STEP, None, dk, dv),
                              lambda h, s: (layer, jnp.maximum(s - 1, 0), h, 0, 0))
    out = _mixer_call(
        functools.partial(_ret_mixer_body, n_batch=B, n_chunks=n_chunks, job=job),
        "ret_mixer", grid,
        in_specs=[pl.BlockSpec(memory_space=pltpu.SMEM),
                  spec(dk, qk_off), spec(dk, qk_off + H), spec(dv, v_off), spec(dv, 0), tab_spec, tab_spec,
                  sspec(dk, qk_off), sspec(dk, qk_off + H), sspec(dv, v_off), sspec(dv, 0), pos_spec, pos_spec,
                  state_spec],
        args=[log_gamma, p, p, p, p, cos_tab, sin_tab, p, p, p, p, cos_s, sin_s, state],
        out_specs=[spec(dv, 0), pl.BlockSpec((B, None, dk, dv), lambda h, s: (0, h, 0, 0)),
                   pl.BlockSpec((LANE, dv), lambda h, s: (0, h)), state_spec],
        out_shape=[jax.ShapeDtypeStruct((R, 2 * D), BF16), jax.ShapeDtypeStruct((B, H, dk, dv), F32),
                   jax.ShapeDtypeStruct((LANE, 2 * D), F32), jax.ShapeDtypeStruct(state.shape, F32)],
        scratch_shapes=[pltpu.VMEM((B, dk, dv), F32), pltpu.VMEM((SAMPLES_PER_STEP, dv), F32)],
        state=state, new_state=new_state, w_f32=w_f32, w_bf=w_bf, job=job, D=D)
    return out[0], out[1], out[2], out[3], tuple(out[4:])


def _rope_tables(pos, half):
    inv = ROPE_BASE ** (-jnp.linspace(0.0, 1.0, half, dtype=F32))
    ang = pos.astype(F32)[:, None] * inv[None, :]
    return jnp.cos(ang), jnp.sin(ang)


def kernel(x_prompt, x_sample, state_hgrn, state_ret, meta_tokens, norm_ffn, ffn_w_gate, ffn_w_up, ffn_w_down, norm_mix, hg_wq, hg_wf, hg_wi, hg_wg, hg_wo, hg_norm, hg_lb_logits, rt_wq, rt_wk, rt_wv, rt_wg, rt_wo, norm_final):
    B, L, D = x_prompt.shape
    NS = x_sample.shape[0]
    depth = norm_mix.shape[0]
    n_meta = meta_tokens.shape[0]
    C = CHUNK
    assert n_meta == N_META and x_sample.shape[1] == 1 and L % C == 0 and NS <= LANE
    n_chunks = L // C

    tp = B * L
    sp = LANE
    rows0 = tp + sp + C
    tm = next(t for t in ROW_TILES if _round_up(rows0, t) - rows0 < LANE) if rows0 >= ROW_TILES[0] else rows0
    R = _round_up(rows0, tm)
    x = jnp.concatenate([
        x_prompt.reshape(B, n_chunks, C, D).transpose(1, 0, 2, 3).reshape(tp, D),
        x_sample.reshape(NS, D), jnp.zeros((sp - NS, D), F32),
        meta_tokens.astype(F32), jnp.zeros((R - tp - sp - n_meta, D), F32)], axis=0)
    sample_blk = tp // LANE

    w_f32 = (ffn_w_gate, ffn_w_up, ffn_w_down)
    d_ff = ffn_w_gate.shape[-1]
    n_cast_tiles = _round_up(d_ff, FF_TILE) // CAST_TILE
    w_bf = _cast_first(w_f32, 0, 0)

    half = D // HB_HEADS // 2
    cos_p, sin_p = _rope_tables(jnp.arange(n_meta + L, dtype=jnp.int32), half)
    cos_s, sin_s = _rope_tables(PAST_LEN + jnp.arange(1, dtype=jnp.int32), half)

    def chunk_table(t):
        meta = jnp.concatenate([t[:n_meta], jnp.zeros((C - n_meta, half), F32)], axis=0)
        return jnp.concatenate([meta, t[n_meta:]], axis=0)

    cos_tab, sin_tab = chunk_table(cos_p), chunk_table(sin_p)
    log_gamma = jnp.log1p(-jnp.exp2(-5.0 - jnp.arange(HB_HEADS, dtype=F32)))
    tables = _hgrn_tables(C)

    new_hgrn_p, new_ret_p = [], []
    new_hgrn_s = new_ret_s = None
    for i in range(depth):
        x = _ffn(x, norm_ffn[i, 0], w_bf, i, 0, None, tm)
        j = i // 2
        job = CastJob(((i, 1),) + (((i + 1, 0),) if i + 1 < depth else ()), d_ff, n_cast_tiles)
        if i % 2 == 0:
            w_in = jnp.concatenate([hg_wq[j].astype(BF16), hg_wg[j].astype(BF16), hg_wi[j].astype(BF16)], axis=1)
            pb = _proj(x, norm_mix[i], w_in, 2 * D, tm)
            g, k = _gate_proj(x, norm_mix[i], hg_wf[j].astype(BF16), hg_lb_logits, j, tm)
            og, s_p, og_s, new_hgrn_s, w_bf = _hgrn_mixer(
                pb, g, k, hg_norm[j], state_hgrn, new_hgrn_s, j, B, L, tables, sample_blk, w_f32, w_bf, job)
            new_hgrn_p.append(s_p)
            w_out = hg_wo[j]
        else:
            w_in = jnp.concatenate([rt_wg[j].astype(BF16), rt_wq[j].astype(BF16),
                                    rt_wk[j].astype(BF16), rt_wv[j].astype(BF16)], axis=1)
            pb = _proj(x, norm_mix[i], w_in, 2 * D, tm)
            og, s_p, og_s, new_ret_s, w_bf = _ret_mixer(
                pb, log_gamma, cos_tab, sin_tab, cos_s, sin_s, state_ret, new_ret_s, j, B, L,
                sample_blk, w_f32, w_bf, job)
            new_ret_p.append(s_p)
            w_out = rt_wo[j]
        og = lax.dynamic_update_slice(og, og_s[:NS].astype(BF16), (tp, 0))
        x = _outproj(og, w_out.astype(BF16), x, tm)
        x = _ffn(x, norm_ffn[i, 1], w_bf, i, 1, norm_final if i == depth - 1 else None, tm)

    y_prompt = x[:tp].reshape(n_chunks, B, C, D).transpose(1, 0, 2, 3).reshape(B, L, D)
    y_sample = x[tp:tp + NS].reshape(NS, 1, D)
    return (y_prompt, y_sample, jnp.stack(new_hgrn_p), new_hgrn_s, jnp.stack(new_ret_p), new_ret_s)
```

```python
import functools
import math
from typing import NamedTuple

import numpy as np
import jax
import jax.numpy as jnp
from jax import lax
from jax.experimental import pallas as pl
from jax.experimental.pallas import tpu as pltpu

F32 = jnp.float32
BF16 = jnp.bfloat16

EPS = 1e-6
LB_FLOOR = 1e-30
ROPE_BASE = 10000.0
N_META = 16
PAST_LEN = 16384
HA_DK = 128
HB_HEADS = 8

LANE = 128
SUBLANE = 8
CHUNK = 128
HGRN_HEADS_PER_STEP = 2
SAMPLES_PER_STEP = SUBLANE
FF_TILE = 512
CAST_TILE = 256
CAST_BLOCK_ELEMS = 1 << 20
PROJ_TILE = 2048
GATE_TILE = 1024
OUT_TILE = 1024
ROW_TILES = (768, 512, 256, 128)
VMEM_LIMIT = 56 * 1024 * 1024

_NT = (((1,), (1,)), ((), ()))
_TN = (((0,), (0,)), ((), ()))


def _round_up(a, m):
    return (a + m - 1) // m * m


def _cparams(n_axes):
    return pltpu.CompilerParams(dimension_semantics=("arbitrary",) * n_axes,
                                vmem_limit_bytes=VMEM_LIMIT)


def _rms(x):
    return x * lax.rsqrt(jnp.mean(x * x, axis=-1, keepdims=True) + EPS)


def _sigmoid(x):
    return 1.0 / (1.0 + jnp.exp(-x))


def _silu(x):
    return x * _sigmoid(x)


def _cast_tile(w, axis, tile, n_tiles, d_ff):
    idx = lax.broadcasted_iota(jnp.int32, w.shape, axis)
    limit = jnp.where(tile == n_tiles - 1, d_ff - (n_tiles - 1) * CAST_TILE, CAST_TILE)
    return jnp.where(idx < limit, w, 0.0).astype(BF16)


def _cast_first_body(wg_ref, wu_ref, wd_ref, og_ref, ou_ref, od_ref, *, d_ff):
    j = pl.program_id(0)
    n = pl.num_programs(0)
    og_ref[...] = _cast_tile(wg_ref[...], 1, j, n, d_ff)
    ou_ref[...] = _cast_tile(wu_ref[...], 1, j, n, d_ff)
    od_ref[...] = _cast_tile(wd_ref[...], 0, j, n, d_ff)


def _cast_specs(D, index):
    per_ff = FF_TILE // CAST_TILE

    def src(kind):
        def f(*ids):
            l, i, t = index(kind, *ids)
            return (l, i, t, 0) if kind == 2 else (l, i, 0, t)
        shape = (None, None, CAST_TILE, D) if kind == 2 else (None, None, D, CAST_TILE)
        return pl.BlockSpec(shape, f)

    def dst(kind):
        def f(*ids):
            l, i, t = index(kind, *ids)
            return (l, i, t, 0) if kind == 2 else (l, i, t // per_ff, 0, t % per_ff)
        shape = (None, None, CAST_TILE, D) if kind == 2 else (None, None, None, D, CAST_TILE)
        return pl.BlockSpec(shape, f)

    return [src(k) for k in range(3)], [dst(k) for k in range(3)]


def _cast_first(w_f32, layer, idx):
    wg, wu, wd = w_f32
    n_l, n_i, D, F = wg.shape
    f_pad = _round_up(F, FF_TILE)
    n_tiles = f_pad // CAST_TILE
    assert 0 < F - (n_tiles - 1) * CAST_TILE <= CAST_TILE
    srcs, dsts = _cast_specs(D, lambda kind, j: (layer, idx, j))
    return pl.pallas_call(
        functools.partial(_cast_first_body, d_ff=F),
        grid=(n_tiles,),
        in_specs=srcs,
        out_specs=dsts,
        out_shape=[jax.ShapeDtypeStruct((n_l, n_i, f_pad // FF_TILE, D, FF_TILE), BF16)] * 2 + [
            jax.ShapeDtypeStruct((n_l, n_i, f_pad, D), BF16)],
        compiler_params=_cparams(1),
        name="cast_first",
    )(wg, wu, wd)


def _cast_into_body(w_ref, *rest):
    rest[-1][...] = w_ref[...].astype(BF16)


def _cast_into(w, out, n_cols, col0):
    n, K, F = w.shape
    ct = max(LANE, min(F, CAST_BLOCK_ELEMS // K))
    assert F % ct == 0 and col0 % ct == 0
    in_specs = [pl.BlockSpec((None, K, ct), lambda l, j: (l, 0, j))]
    args = [w]
    if out is not None:
        in_specs.append(pl.BlockSpec(memory_space=pl.ANY))
        args.append(out)
    return pl.pallas_call(
        _cast_into_body,
        grid=(n, F // ct),
        in_specs=in_specs,
        out_specs=pl.BlockSpec((None, K, ct), lambda l, j: (l, 0, col0 // ct + j)),
        out_shape=jax.ShapeDtypeStruct((n, K, n_cols), BF16),
        input_output_aliases={1: 0} if out is not None else {},
        compiler_params=_cparams(2),
        name="cast_into",
    )(*args)


def _cast_side_by_side(ws):
    n_cols = sum(w.shape[2] for w in ws)
    out, col0 = None, 0
    for w in ws:
        out = _cast_into(w, out, n_cols, col0)
        col0 += w.shape[2]
    return out


class CastJob(NamedTuple):
    targets: tuple
    d_ff: int
    n_tiles: int

    @property
    def span(self):
        return len(self.targets) * self.n_tiles

    def index(self, kind, t):
        tt = jnp.clip(t - kind * self.span, 0, self.span - 1)
        which = tt // self.n_tiles
        layer, idx = self.targets[0]
        for n, (l, i) in enumerate(self.targets[1:], 1):
            layer = jnp.where(which == n, l, layer)
            idx = jnp.where(which == n, i, idx)
        return layer, idx, tt % self.n_tiles


def _cast_step(job, t, src_refs, dst_refs):
    for kind in range(3):
        def convert(kind=kind):
            tile = (t - kind * job.span) % job.n_tiles
            dst_refs[kind][...] = _cast_tile(src_refs[kind][...], 0 if kind == 2 else 1, tile,
                                             job.n_tiles, job.d_ff)
        pl.when((t >= kind * job.span) & (t < (kind + 1) * job.span))(convert)


def _ffn_body(*refs, mode, n_chunks, n_tail):
    refs = list(refs)
    x_refs = [refs.pop(0) for _ in range(2 if mode == "first" else 1)]
    gain_ref, wg_ref, wu_ref, wd_ref, fgain_ref = refs[:5]
    if mode == "last":
        y4_ref, yt_ref, h_ref, acc_ref = refs[5:]
    else:
        acc_ref, h_ref = refs[5:]
    i = pl.program_id(0)
    j = pl.program_id(1)

    def init(x):
        h_ref[...] = (_rms(x) * gain_ref[...]).astype(BF16)
        acc_ref[...] = x

    if mode == "first":
        pl.when((j == 0) & (i < n_chunks))(lambda: init(x_refs[0][...].reshape(acc_ref.shape)))
        pl.when((j == 0) & (i == n_chunks))(lambda: init(x_refs[1][...]))
    elif mode == "last":
        @pl.when(j == 0)
        def _():
            rows = lax.broadcasted_iota(jnp.int32, acc_ref.shape, 0)
            init(jnp.where(rows < jnp.where(i == n_chunks, n_tail, acc_ref.shape[0]), x_refs[0][...], 0.0))
    else:
        pl.when(j == 0)(lambda: init(x_refs[0][...]))

    h = h_ref[...]
    g = jnp.dot(h, wg_ref[...], preferred_element_type=F32)
    u = jnp.dot(h, wu_ref[...], preferred_element_type=F32)
    a = (0.5 * _silu(g) * u).astype(BF16)
    acc_ref[...] += jnp.dot(a, wd_ref[...], preferred_element_type=F32)

    if mode == "last":
        last_j = j == pl.num_programs(1) - 1

        @pl.when(last_j & (i < n_chunks))
        def _():
            y4_ref[...] = (_rms(acc_ref[...]) * fgain_ref[...]).reshape(y4_ref.shape)

        @pl.when(last_j & (i == n_chunks))
        def _():
            yt_ref[...] = _rms(acc_ref[...]) * fgain_ref[...]


def _ffn(x, gain, w_bf, layer, idx, tm, mode="mid", fgain=None, n_rows=None, home=None):
    wg, wu, wd = w_bf
    nj = wg.shape[2]
    D = wg.shape[3]
    w_specs = [
        pl.BlockSpec((None, None, None, D, FF_TILE), lambda i, j: (layer, idx, j, 0, 0)),
        pl.BlockSpec((None, None, None, D, FF_TILE), lambda i, j: (layer, idx, j, 0, 0)),
        pl.BlockSpec((None, None, FF_TILE, D), lambda i, j: (layer, idx, j, 0)),
    ]
    vec_spec = pl.BlockSpec((1, D), lambda i, j: (0, 0))
    if fgain is None:
        fgain = gain
    n_chunks = n_tail = 0
    if mode == "mid":
        R = x.shape[0]
        grid = (R // tm, nj)
        x_args, x_specs = [x], [pl.BlockSpec((tm, D), lambda i, j: (i, 0))]
        out_specs = pl.BlockSpec((tm, D), lambda i, j: (i, 0))
        out_shape = jax.ShapeDtypeStruct((R, D), F32)
        scratch = [pltpu.VMEM((tm, D), BF16)]
    else:
        if mode == "first":
            home, R = x[0].shape, n_rows
        else:
            R = x.shape[0]
        B, n_chunks, C, _ = home
        tm = B * C
        n_tail = R - n_chunks * tm
        assert 0 < n_tail <= tm
        grid = (n_chunks + 1, nj)
        home_spec = pl.BlockSpec((B, None, C, D), lambda i, j: (0, jnp.minimum(i, n_chunks - 1), 0, 0))
        tail_spec = pl.BlockSpec((tm, D), lambda i, j: (0, 0))
        rows_spec = pl.BlockSpec((tm, D), lambda i, j: (i, 0))
        if mode == "first":
            x_args, x_specs = list(x), [home_spec, tail_spec]
            out_specs, out_shape = rows_spec, jax.ShapeDtypeStruct((R, D), F32)
            scratch = [pltpu.VMEM((tm, D), BF16)]
        else:
            x_args, x_specs = [x], [rows_spec]
            out_specs = [home_spec, tail_spec]
            out_shape = [jax.ShapeDtypeStruct(home, F32), jax.ShapeDtypeStruct((tm, D), F32)]
            scratch = [pltpu.VMEM((tm, D), BF16), pltpu.VMEM((tm, D), F32)]
    return pl.pallas_call(
        functools.partial(_ffn_body, mode=mode, n_chunks=n_chunks, n_tail=n_tail),
        grid=grid,
        in_specs=x_specs + [vec_spec] + w_specs + [vec_spec],
        out_specs=out_specs,
        out_shape=out_shape,
        scratch_shapes=scratch,
        compiler_params=_cparams(2),
        name="ffn_" + mode,
    )(*x_args, gain.reshape(1, D), wg, wu, wd, fgain.reshape(1, D))


def _proj_body(x_ref, gain_ref, w_ref, o_ref, h_ref, *, n_silu):
    j = pl.program_id(1)

    @pl.when(j == 0)
    def _():
        h_ref[...] = (_rms(x_ref[...]) * gain_ref[...]).astype(BF16)

    y = jnp.dot(h_ref[...], w_ref[...], preferred_element_type=F32)
    if n_silu > 0:
        y = jnp.where(j < n_silu, _silu(y), y)
    o_ref[...] = y.astype(BF16)


def _proj(x, gain, w, layer, n_silu_cols, tm):
    R, D = x.shape
    N = w.shape[2]
    return pl.pallas_call(
        functools.partial(_proj_body, n_silu=n_silu_cols // PROJ_TILE),
        grid=(R // tm, N // PROJ_TILE),
        in_specs=[
            pl.BlockSpec((tm, D), lambda i, j: (i, 0)),
            pl.BlockSpec((1, D), lambda i, j: (0, 0)),
            pl.BlockSpec((None, D, PROJ_TILE), lambda i, j: (layer, 0, j)),
        ],
        out_specs=pl.BlockSpec((tm, PROJ_TILE), lambda i, j: (i, j)),
        out_shape=jax.ShapeDtypeStruct((R, N), BF16),
        scratch_shapes=[pltpu.VMEM((tm, D), BF16)],
        compiler_params=_cparams(2),
        name="proj",
    )(x, gain.reshape(1, D), w)


def _lower_bound(logits, layer):
    e = jnp.exp(logits - jnp.max(logits, axis=0, keepdims=True))
    probs = e / jnp.sum(e, axis=0, keepdims=True)
    cs = probs[0:1]
    for i in range(1, layer + 1):
        cs = cs + probs[i:i + 1]
    return cs - probs[0:1]


def _gate_proj_body(x_ref, gain_ref, w_ref, logit_ref, g_ref, k_ref, h_ref, *, layer):
    j = pl.program_id(1)

    @pl.when(j == 0)
    def _():
        h_ref[...] = (_rms(x_ref[...]) * gain_ref[...]).astype(BF16)

    z = jnp.dot(h_ref[...], w_ref[...], preferred_element_type=F32)
    lb = _lower_bound(logit_ref[...], layer)
    t = jnp.exp(-jnp.abs(z))
    r = 1.0 / (1.0 + t)
    tr = t * r
    pos = z >= 0.0
    g_ref[...] = jnp.log(jnp.maximum(lb, LB_FLOOR) + (1.0 - lb) * jnp.where(pos, r, tr))
    k_ref[...] = ((1.0 - lb) * jnp.where(pos, tr, r)).astype(BF16)


def _gate_proj(x, gain, w, logits, layer, tm):
    R, D = x.shape
    N = w.shape[2]
    spec = pl.BlockSpec((tm, GATE_TILE), lambda i, j: (i, j))
    return pl.pallas_call(
        functools.partial(_gate_proj_body, layer=layer),
        grid=(R // tm, N // GATE_TILE),
        in_specs=[
            pl.BlockSpec((tm, D), lambda i, j: (i, 0)),
            pl.BlockSpec((1, D), lambda i, j: (0, 0)),
            pl.BlockSpec((None, D, GATE_TILE), lambda i, j: (layer, 0, j)),
            pl.BlockSpec((logits.shape[0], GATE_TILE), lambda i, j: (0, j)),
        ],
        out_specs=[spec, spec],
        out_shape=[jax.ShapeDtypeStruct((R, N), F32), jax.ShapeDtypeStruct((R, N), BF16)],
        scratch_shapes=[pltpu.VMEM((tm, D), BF16)],
        compiler_params=_cparams(2),
        name="gate_proj",
    )(x, gain.reshape(1, D), w, logits)


def _outproj_body(a_ref, w_ref, x_ref, o_ref):
    o_ref[...] = x_ref[...] + jnp.dot(a_ref[...], w_ref[...], preferred_element_type=F32)


def _outproj(a, w, layer, x, tm):
    R, K = a.shape
    D = w.shape[2]
    return pl.pallas_call(
        _outproj_body,
        grid=(R // tm, D // OUT_TILE),
        in_specs=[
            pl.BlockSpec((tm, K), lambda i, j: (i, 0)),
            pl.BlockSpec((None, K, OUT_TILE), lambda i, j: (layer, 0, j)),
            pl.BlockSpec((tm, OUT_TILE), lambda i, j: (i, j)),
        ],
        out_specs=pl.BlockSpec((tm, OUT_TILE), lambda i, j: (i, j)),
        out_shape=jax.ShapeDtypeStruct((R, D), F32),
        compiler_params=_cparams(2),
        name="outproj",
    )(a, w, x)


def _col_bcast(x):
    bs = x.shape[0]
    return jnp.concatenate([x] * (LANE // bs), axis=0).T


def _sample_rows(s):
    grp = s - 1
    n = SAMPLES_PER_STEP
    rows = pl.ds(pl.multiple_of(grp * n, n), n)
    rows2 = pl.ds(pl.multiple_of((grp // 2) * 2 * n, 2 * n), 2 * n)
    return rows, rows2, (grp % 2) == 1


def _half(x2, second):
    n = SAMPLES_PER_STEP
    return jnp.where(second, x2[n:], x2[:n])


def _mixer_call(body, name, grid, in_specs, args, out_specs, out_shape, scratch_shapes, state, new_state,
                w_f32, w_bf, job, D):
    n_s = grid[1]
    srcs, dsts = _cast_specs(D, lambda kind, h, s: job.index(kind, h * n_s + s))
    any_spec = pl.BlockSpec(memory_space=pl.ANY)
    in_specs = list(in_specs) + srcs
    args = list(args) + list(w_f32)
    out_specs = list(out_specs) + dsts
    out_shape = list(out_shape) + [jax.ShapeDtypeStruct(w.shape, w.dtype) for w in w_bf]
    n_out = len(out_shape)
    aliases = {}
    if new_state is not None:
        aliases[len(args)] = n_out - 4
        in_specs.append(any_spec)
        args.append(new_state)
    for n, w in enumerate(w_bf):
        aliases[len(args)] = n_out - 3 + n
        in_specs.append(any_spec)
        args.append(w)
    return pl.pallas_call(
        functools.partial(body, n_alias=len(aliases)),
        grid=grid, in_specs=in_specs, out_specs=out_specs, out_shape=out_shape,
        scratch_shapes=scratch_shapes, input_output_aliases=aliases,
        compiler_params=_cparams(2), name=name,
    )(*args)


def _hgrn_tables(C):
    n_lvl = int(math.log2(C))
    assert 1 << n_lvl == C
    n_fine = min(n_lvl, 3)
    r = np.arange(C)
    msum = np.zeros((n_fine + 1, C, C), np.float32)
    pair = np.zeros((n_lvl + 1, C, C), np.float32)
    u = r[None, :]
    for l in range(n_lvl):
        half = 1 << l
        blk = 2 * half
        m = r - r % blk + half - 1
        up = (r % blk) >= half
        if l < n_fine:
            msum[l] = np.where(up[:, None], (u > m[:, None]) & (u <= r[:, None]),
                               (u > r[:, None]) & (u <= m[:, None]))
        pair[l] = ((r[:, None] // blk) == (r[None, :] // blk)) & up[:, None] & ~up[None, :]
    pair[n_lvl] = np.eye(C)
    msum[n_fine] = u <= r[:, None]
    return jnp.asarray(msum.reshape(-1, C), BF16), jnp.asarray(pair, F32)


def _split3(g):
    hi = g.astype(BF16)
    r1 = g - hi.astype(F32)
    mid = r1.astype(BF16)
    lo = (r1 - mid.astype(F32)).astype(BF16)
    return hi, mid, lo


def _hgrn_chunks(qs, gs, ks, vs, gates, hgains, sts, msum, pair_ref, valid, side=()):
    side = list(side)
    n_lvl = pair_ref.shape[0] - 1
    per_phase = -(-len(side) // (n_lvl + 1))

    def run_side():
        for _ in range(min(per_phase, len(side))):
            side.pop(0)()

    n = range(len(qs))
    C = qs[0].shape[0]
    n_fine = msum.shape[0] // C - 1
    qf = [q.astype(F32) for q in qs]
    kf = [k.astype(F32) for k in ks]
    if valid is not None:
        gs = [g * valid for g in gs]
        kf = [k * valid for k in kf]
        ks = [k.astype(BF16) for k in kf]
    e3 = [jnp.dot(msum, jnp.concatenate(_split3(g), axis=1), preferred_element_type=F32) for g in gs]
    e = [t[:, :LANE] + t[:, LANE:2 * LANE] + t[:, 2 * LANE:] for t in e3]
    c = [t[n_fine * C:] for t in e]
    x_fine = [jnp.exp(t[:n_fine * C]) for t in e]
    run_side()

    scores = [lax.dot_general(qs[b], ks[b], _NT, preferred_element_type=F32) * pair_ref[n_lvl] for b in n]
    for l in range(n_lvl):
        if l < n_fine:
            xl = [t[l * C:(l + 1) * C] for t in x_fine]
        else:
            half = 1 << l
            up = lax.broadcasted_iota(jnp.int32, (1, 2 * half, 1), 1) >= half
            xl = []
            for t in c:
                c3 = t.reshape(C // (2 * half), 2 * half, LANE)
                cm = c3[:, half - 1:half, :]
                xl.append(jnp.exp(jnp.where(up, c3 - cm, cm - c3)).reshape(C, LANE))
        scores = [scores[b] + lax.dot_general((qf[b] * xl[b]).astype(BF16), (kf[b] * xl[b]).astype(BF16), _NT,
                                              preferred_element_type=F32) * pair_ref[l] for b in n]
        run_side()
    while side:
        side.pop(0)()
    xc = [jnp.exp(t) for t in c]
    xt = [jnp.exp(t[C - 1:C, :] - t) for t in c]
    o = [lax.dot_general((qf[b] * xc[b]).astype(BF16), sts[b].astype(BF16), _NT, preferred_element_type=F32)
         for b in n]
    o = [o[b] + jnp.dot(scores[b].astype(BF16), vs[b], preferred_element_type=F32) for b in n]
    st_new = [sts[b] * xc[b][C - 1:C, :] + lax.dot_general(vs[b], (kf[b] * xt[b]).astype(BF16), _TN,
                                                           preferred_element_type=F32) for b in n]
    og = [(_rms(o[b]) * hgains[b] * gates[b].astype(F32)).astype(BF16) for b in n]
    return og, st_new


def _hgrn_mixer_body(*refs, n_batch, n_chunks, n_heads, job, n_alias):
    (q_ref, gate_ref, v_ref, g_ref, k_ref, hgain_ref, msum_ref, pair_ref,
     qs_ref, gates_ref, vs_ref, gs_ref, ks_ref, s_ref) = refs[:14]
    w_src = refs[14:17]
    og_ref, so_ref, ogs_ref, sn_ref = refs[17 + n_alias:21 + n_alias]
    w_dst = refs[21 + n_alias:24 + n_alias]
    st_ref, o_scr = refs[24 + n_alias:]
    C = CHUNK
    s = pl.program_id(1)
    _cast_step(job, pl.program_id(0) * pl.num_programs(1) + s, w_src, w_dst)

    def step(seqs, sts, valid, side=()):
        tiles = [(r, pl.ds(hh * LANE, LANE)) for r, hh in seqs]
        return _hgrn_chunks([q_ref[t] for t in tiles], [g_ref[t] for t in tiles], [k_ref[t] for t in tiles],
                            [v_ref[t] for t in tiles], [gate_ref[t] for t in tiles],
                            [hgain_ref[:, t[1]] for t in tiles], sts, msum_ref[...], pair_ref, valid, side)

    @pl.when(s == 0)
    def _():
        valid = (lax.broadcasted_iota(jnp.int32, (C, 1), 0) < N_META).astype(F32)
        seqs = [(pl.ds(C, C), hh) for hh in range(n_heads)]
        og, st_new = step(seqs, [jnp.zeros((LANE, HA_DK), F32)] * n_heads, valid)
        og_ref[...] = jnp.zeros_like(og_ref)
        for hh in range(n_heads):
            og_ref[pl.ds(C, C), pl.ds(hh * LANE, LANE)] = og[hh]
            for b in range(n_batch):
                st_ref[hh * n_batch + b] = st_new[hh]

    @pl.when(s > 0)
    def _():
        rows, rows2, second = _sample_rows(s)
        side = []
        for hh in range(n_heads):
            lanes = pl.ds(hh * LANE, LANE)
            q2 = qs_ref[rows2, lanes]
            v1 = _half(vs_ref[rows2, lanes].astype(F32), second)
            ft = _col_bcast(jnp.exp(gs_ref[rows, lanes]))
            kt = _col_bcast(_half(ks_ref[rows2, lanes].astype(F32), second))

            def update(b, hh=hh, q2=q2, v1=v1, ft=ft, kt=kt):
                sn = ft[:, b:b + 1] * s_ref[b, hh] + kt[:, b:b + 1] * v1[b:b + 1, :]
                sn_ref[b, hh] = sn
                res = jnp.dot(q2, sn.astype(BF16), preferred_element_type=F32)
                o_scr[hh, b:b + 1, :] = _half(res, second)[b:b + 1, :]

            def finish(hh=hh, lanes=lanes):
                gate1 = _half(gates_ref[rows2, lanes].astype(F32), second)
                ogs_ref[rows, lanes] = _rms(o_scr[hh]) * hgain_ref[:, lanes] * gate1

            side += [functools.partial(update, b) for b in range(SAMPLES_PER_STEP)] + [finish]

        seqs = [(pl.ds(b * C, C), hh) for hh in range(n_heads) for b in range(n_batch)]
        og, st_new = step(seqs, [st_ref[i] for i in range(len(seqs))], None, side)
        for i, (r, hh) in enumerate(seqs):
            og_ref[r, pl.ds(hh * LANE, LANE)] = og[i]
            st_ref[i] = st_new[i]

    @pl.when(s == n_chunks)
    def _():
        for hh in range(n_heads):
            for b in range(n_batch):
                so_ref[b, hh] = st_ref[hh * n_batch + b].T


def _hgrn_mixer(pb, g, k, hgain, state, new_state, layer, n_batch, seq, tables, sample_blk, w_f32, w_bf, job):
    R = pb.shape[0]
    D = pb.shape[1] // 3
    H = D // HA_DK
    C = CHUNK
    B = n_batch
    hp = HGRN_HEADS_PER_STEP
    n_chunks = seq // C
    n_sample = state.shape[1]
    msum, pair = tables
    grid = (H // hp, 1 + n_chunks)
    assert B >= 2 and R >= n_chunks * B * C + 2 * C and H % hp == 0
    assert n_sample == SAMPLES_PER_STEP * n_chunks and 3 * job.span <= grid[0] * grid[1]

    def spec(col0):
        return pl.BlockSpec((B * C, hp * LANE),
                            lambda h, s: (jnp.where(s == 0, n_chunks, s - 1), col0 // hp + h))

    def sspec(col0):
        return pl.BlockSpec((LANE, hp * LANE), lambda h, s: (sample_blk, col0 // hp + h))

    state_spec = pl.BlockSpec((None, SAMPLES_PER_STEP, hp, HA_DK, LANE),
                              lambda h, s: (layer, jnp.maximum(s - 1, 0), h, 0, 0))
    out = _mixer_call(
        functools.partial(_hgrn_mixer_body, n_batch=B, n_chunks=n_chunks, n_heads=hp, job=job),
        "hgrn_mixer", grid,
        in_specs=[spec(0), spec(H), spec(2 * H), spec(0), spec(0),
                  pl.BlockSpec((1, hp * LANE), lambda h, s: (0, h)),
                  pl.BlockSpec(msum.shape, lambda h, s: (0, 0)),
                  pl.BlockSpec(pair.shape, lambda h, s: (0, 0, 0)),
                  sspec(0), sspec(H), sspec(2 * H), sspec(0), sspec(0), state_spec],
        args=[pb, pb, pb, g, k, hgain.reshape(1, D), msum, pair, pb, pb, pb, g, k, state],
        out_specs=[spec(0), pl.BlockSpec((B, hp, HA_DK, LANE), lambda h, s: (0, h, 0, 0)),
                   pl.BlockSpec((LANE, hp * LANE), lambda h, s: (0, h)), state_spec],
        out_shape=[jax.ShapeDtypeStruct((R, D), BF16), jax.ShapeDtypeStruct((B, H, HA_DK, LANE), F32),
                   jax.ShapeDtypeStruct((LANE, D), F32), jax.ShapeDtypeStruct(state.shape, F32)],
        scratch_shapes=[pltpu.VMEM((hp * B, LANE, HA_DK), F32), pltpu.VMEM((hp, SAMPLES_PER_STEP, LANE), F32)],
        state=state, new_state=new_state, w_f32=w_f32, w_bf=w_bf, job=job, D=D)
    return out[0], out[1], out[2], out[3], tuple(out[4:])


def _rotary(x, cos, sin):
    half = x.shape[-1] // 2
    x1, x2 = x[:, :half], x[:, half:]
    return jnp.concatenate([x1 * cos - x2 * sin, x1 * sin + x2 * cos], axis=-1)


def _ret_chunks(qs, ks, vs, gates, cos, sin, lg, sts, n_valid):
    n = range(len(qs))
    C, dk = qs[0].shape
    t_col = lax.broadcasted_iota(jnp.int32, (C, 1), 0)
    t_row = lax.broadcasted_iota(jnp.int32, (1, C), 1)
    n_col = jnp.minimum(t_col + 1, n_valid).astype(F32)
    n_row = jnp.minimum(t_row + 1, n_valid).astype(F32)
    n_last = float(min(C, n_valid))
    decay = jnp.where(t_col >= t_row, jnp.exp(lg * (n_col - n_row)), 0.0)
    q_scale = jnp.exp(lg * n_col)
    k_scale = jnp.exp(lg * (n_last - n_col))
    k_norm = dk ** -0.5
    if n_valid < C:
        k_norm = k_norm * (t_col < n_valid).astype(F32)
    q = [_rotary(t.astype(F32), cos, sin) for t in qs]
    k = [_rotary(t.astype(F32), cos, sin) * k_norm for t in ks]
    scores = [lax.dot_general(q[b].astype(BF16), k[b].astype(BF16), _NT, preferred_element_type=F32) * decay
              for b in n]
    o = [jnp.dot((q[b] * q_scale).astype(BF16), sts[b].astype(BF16), preferred_element_type=F32) for b in n]
    o = [o[b] + jnp.dot(scores[b].astype(BF16), vs[b], preferred_element_type=F32) for b in n]
    st_new = [jnp.exp(lg * n_last) * sts[b] + lax.dot_general((k[b] * k_scale).astype(BF16), vs[b], _TN,
                                                              preferred_element_type=F32) for b in n]
    og = [(_rms(o[b]) * gates[b].astype(F32)).astype(BF16) for b in n]
    return og, st_new


def _ret_mixer_body(*refs, n_batch, n_chunks, job, n_alias):
    (lg_ref, q_ref, k_ref, v_ref, gate_ref, cos_ref, sin_ref,
     qs_ref, ks_ref, vs_ref, gates_ref, coss_ref, sins_ref, s_ref) = refs[:14]
    w_src = refs[14:17]
    og_ref, so_ref, ogs_ref, sn_ref = refs[17 + n_alias:21 + n_alias]
    w_dst = refs[21 + n_alias:24 + n_alias]
    st_ref, o_scr = refs[24 + n_alias:]
    C = CHUNK
    lg = lg_ref[pl.program_id(0)]
    s = pl.program_id(1)
    _cast_step(job, pl.program_id(0) * pl.num_programs(1) + s, w_src, w_dst)

    def step(rows, sts, n_valid):
        return _ret_chunks([q_ref[r, :] for r in rows], [k_ref[r, :] for r in rows], [v_ref[r, :] for r in rows],
                           [gate_ref[r, :] for r in rows], cos_ref[...], sin_ref[...], lg, sts, n_valid)

    @pl.when(s == 0)
    def _():
        og, st_new = step([pl.ds(C, C)], [jnp.zeros(st_ref.shape[1:], F32)], N_META)
        og_ref[...] = jnp.zeros_like(og_ref)
        og_ref[pl.ds(C, C), :] = og[0]
        for b in range(n_batch):
            st_ref[b] = st_new[0]

    @pl.when(s > 0)
    def _():
        rows, rows2, second = _sample_rows(s)
        dk = qs_ref.shape[-1]
        gamma = jnp.exp(lg)
        cos, sin = coss_ref[...], sins_ref[...]
        q = _rotary(_half(qs_ref[rows2, :].astype(F32), second), cos, sin)
        k = _rotary(_half(ks_ref[rows2, :].astype(F32), second), cos, sin) * (dk ** -0.5)
        v = _half(vs_ref[rows2, :].astype(F32), second)
        qt = jnp.concatenate([_col_bcast(q[:, :LANE]), _col_bcast(q[:, LANE:])], axis=0)
        kt = jnp.concatenate([_col_bcast(k[:, :LANE]), _col_bcast(k[:, LANE:])], axis=0)
        for b in range(SAMPLES_PER_STEP):
            sn = gamma * s_ref[b] + kt[:, b:b + 1] * v[b:b + 1, :]
            sn_ref[b] = sn
            o_scr[b:b + 1, :] = jnp.sum(qt[:, b:b + 1] * sn, axis=0, keepdims=True)
        ogs_ref[rows, :] = _rms(o_scr[...]) * _half(gates_ref[rows2, :].astype(F32), second)

        og, st_new = step([pl.ds(b * C, C) for b in range(n_batch)], [st_ref[b] for b in range(n_batch)], C)
        for b in range(n_batch):
            og_ref[pl.ds(b * C, C), :] = og[b]
            st_ref[b] = st_new[b]

    @pl.when(s == n_chunks)
    def _():
        so_ref[...] = st_ref[...]


def _ret_mixer(p, log_gamma, cos_tab, sin_tab, cos_s, sin_s, state, new_state, layer, n_batch, seq,
               sample_blk, w_f32, w_bf, job):
    R = p.shape[0]
    D = p.shape[1] // 6
    H = HB_HEADS
    dk = D // H
    dv = 2 * D // H
    C = CHUNK
    B = n_batch
    n_chunks = seq // C
    n_sample = state.shape[1]
    qk_off = 2 * D // dk
    v_off = 4 * D // dv
    grid = (H, 1 + n_chunks)
    assert B >= 2 and R >= n_chunks * B * C + 2 * C and dk == 2 * LANE
    assert n_sample == SAMPLES_PER_STEP * n_chunks and 3 * job.span <= grid[0] * grid[1]

    def spec(width, col0):
        return pl.BlockSpec((B * C, width), lambda h, s: (jnp.where(s == 0, n_chunks, s - 1), col0 + h))

    def sspec(width, col0):
        return pl.BlockSpec((LANE, width), lambda h, s: (sample_blk, col0 + h))

    tab_spec = pl.BlockSpec((C, dk // 2), lambda h, s: (s, 0))
    pos_spec = pl.BlockSpec((1, dk // 2), lambda h, s: (0, 0))
    state_spec = pl.BlockSpec((None, SAMPLES_PER_STEP, None, dk, dv),
                              lambda h, s: (layer, jnp.maximum(s - 1, 0), h, 0, 0))
    out = _mixer_call(
        functools.partial(_ret_mixer_body, n_batch=B, n_chunks=n_chunks, job=job),
        "ret_mixer", grid,
        in_specs=[pl.BlockSpec(memory_space=pltpu.SMEM),
                  spec(dk, qk_off), spec(dk, qk_off + H), spec(dv, v_off), spec(dv, 0), tab_spec, tab_spec,
                  sspec(dk, qk_off), sspec(dk, qk_off + H), sspec(dv, v_off), sspec(dv, 0), pos_spec, pos_spec,
                  state_spec],
        args=[log_gamma, p, p, p, p, cos_tab, sin_tab, p, p, p, p, cos_s, sin_s, state],
        out_specs=[spec(dv, 0), pl.BlockSpec((B, None, dk, dv), lambda h, s: (0, h, 0, 0)),
                   pl.BlockSpec((LANE, dv), lambda h, s: (0, h)), state_spec],
        out_shape=[jax.ShapeDtypeStruct((R, 2 * D), BF16), jax.ShapeDtypeStruct((B, H, dk, dv), F32),
                   jax.ShapeDtypeStruct((LANE, 2 * D), F32), jax.ShapeDtypeStruct(state.shape, F32)],
        scratch_shapes=[pltpu.VMEM((B, dk, dv), F32), pltpu.VMEM((SAMPLES_PER_STEP, dv), F32)],
        state=state, new_state=new_state, w_f32=w_f32, w_bf=w_bf, job=job, D=D)
    return out[0], out[1], out[2], out[3], tuple(out[4:])


def _rope_tables(pos, half):
    inv = ROPE_BASE ** (-jnp.linspace(0.0, 1.0, half, dtype=F32))
    ang = pos.astype(F32)[:, None] * inv[None, :]
    return jnp.cos(ang), jnp.sin(ang)


def kernel(x_prompt, x_sample, state_hgrn, state_ret, meta_tokens, norm_ffn, ffn_w_gate, ffn_w_up, ffn_w_down, norm_mix, hg_wq, hg_wf, hg_wi, hg_wg, hg_wo, hg_norm, hg_lb_logits, rt_wq, rt_wk, rt_wv, rt_wg, rt_wo, norm_final):
    B, L, D = x_prompt.shape
    NS = x_sample.shape[0]
    depth = norm_mix.shape[0]
    n_meta = meta_tokens.shape[0]
    C = CHUNK
    assert n_meta == N_META and x_sample.shape[1] == 1 and L % C == 0 and NS <= LANE
    n_chunks = L // C

    tp = B * L
    sp = LANE
    rows0 = tp + sp + C
    tm = next(t for t in ROW_TILES if _round_up(rows0, t) - rows0 < LANE) if rows0 >= ROW_TILES[0] else rows0
    R = _round_up(rows0, tm)
    home = (B, n_chunks, C, D)
    x_tail = jnp.concatenate([
        x_sample.reshape(NS, D), jnp.zeros((sp - NS, D), F32),
        meta_tokens.astype(F32), jnp.zeros((B * C - sp - n_meta, D), F32)], axis=0)
    sample_blk = tp // LANE

    w_f32 = (ffn_w_gate, ffn_w_up, ffn_w_down)
    d_ff = ffn_w_gate.shape[-1]
    n_cast_tiles = _round_up(d_ff, FF_TILE) // CAST_TILE
    w_bf = _cast_first(w_f32, 0, 0)

    hg_in = _cast_side_by_side([hg_wq, hg_wg, hg_wi])
    hg_f = _cast_side_by_side([hg_wf])
    hg_out = _cast_side_by_side([hg_wo])
    rt_in = _cast_side_by_side([rt_wg, rt_wq, rt_wk, rt_wv])
    rt_out = _cast_side_by_side([rt_wo])

    half = D // HB_HEADS // 2
    cos_p, sin_p = _rope_tables(jnp.arange(n_meta + L, dtype=jnp.int32), half)
    cos_s, sin_s = _rope_tables(PAST_LEN + jnp.arange(1, dtype=jnp.int32), half)

    def chunk_table(t):
        meta = jnp.concatenate([t[:n_meta], jnp.zeros((C - n_meta, half), F32)], axis=0)
        return jnp.concatenate([meta, t[n_meta:]], axis=0)

    cos_tab, sin_tab = chunk_table(cos_p), chunk_table(sin_p)
    log_gamma = jnp.log1p(-jnp.exp2(-5.0 - jnp.arange(HB_HEADS, dtype=F32)))
    tables = _hgrn_tables(C)

    new_hgrn_p, new_ret_p = [], []
    new_hgrn_s = new_ret_s = None
    for i in range(depth):
        if i == 0:
            x = _ffn((x_prompt.reshape(home), x_tail), norm_ffn[i, 0], w_bf, i, 0, tm, mode="first", n_rows=R)
        else:
            x = _ffn(x, norm_ffn[i, 0], w_bf, i, 0, tm)
        j = i // 2
        job = CastJob(((i, 1),) + (((i + 1, 0),) if i + 1 < depth else ()), d_ff, n_cast_tiles)
        if i % 2 == 0:
            pb = _proj(x, norm_mix[i], hg_in, j, 2 * D, tm)
            g, k = _gate_proj(x, norm_mix[i], hg_f, hg_lb_logits, j, tm)
            og, s_p, og_s, new_hgrn_s, w_bf = _hgrn_mixer(
                pb, g, k, hg_norm[j], state_hgrn, new_hgrn_s, j, B, L, tables, sample_blk, w_f32, w_bf, job)
            new_hgrn_p.append(s_p)
            w_out = hg_out
        else:
            pb = _proj(x, norm_mix[i], rt_in, j, 2 * D, tm)
            og, s_p, og_s, new_ret_s, w_bf = _ret_mixer(
                pb, log_gamma, cos_tab, sin_tab, cos_s, sin_s, state_ret, new_ret_s, j, B, L,
                sample_blk, w_f32, w_bf, job)
            new_ret_p.append(s_p)
            w_out = rt_out
        og = lax.dynamic_update_slice(og, og_s[:NS].astype(BF16), (tp, 0))
        x = _outproj(og, w_out, j, x, tm)
        if i < depth - 1:
            x = _ffn(x, norm_ffn[i, 1], w_bf, i, 1, tm)
    y_home, y_tail = _ffn(x, norm_ffn[depth - 1, 1], w_bf, depth - 1, 1, tm, mode="last", fgain=norm_final,
                          home=home)
    y_prompt = y_home.reshape(B, L, D)
    y_sample = y_tail[:NS].reshape(NS, 1, D)
    return (y_prompt, y_sample, jnp.stack(new_hgrn_p), new_hgrn_s, jnp.stack(new_ret_p), new_ret_s)
```

```python
import functools
import math
from typing import NamedTuple

import numpy as np
import jax
import jax.numpy as jnp
from jax import lax
from jax.experimental import pallas as pl
from jax.experimental.pallas import tpu as pltpu

F32 = jnp.float32
BF16 = jnp.bfloat16

EPS = 1e-6
LB_FLOOR = 1e-30
ROPE_BASE = 10000.0
N_META = 16
PAST_LEN = 16384
HA_DK = 128
HB_HEADS = 8

LANE = 128
SUBLANE = 8
CHUNK = 128
HGRN_HEADS_PER_STEP = 2
SAMPLES_PER_STEP = SUBLANE
FF_TILE = 512
CAST_TILE = 256
CAST_BLOCK_ELEMS = 1 << 20
PROJ_TILE = 2048
GATE_TILE = 1024
OUT_TILE = 1024
ROW_TILES = (768, 512, 256, 128)
VMEM_LIMIT = 56 * 1024 * 1024

_NT = (((1,), (1,)), ((), ()))
_TN = (((0,), (0,)), ((), ()))


def _round_up(a, m):
    return (a + m - 1) // m * m


def _cparams(n_axes):
    return pltpu.CompilerParams(dimension_semantics=("arbitrary",) * n_axes,
                                vmem_limit_bytes=VMEM_LIMIT)


def _rms(x):
    return x * lax.rsqrt(jnp.mean(x * x, axis=-1, keepdims=True) + EPS)


def _sigmoid(x):
    return 1.0 / (1.0 + jnp.exp(-x))


def _silu(x):
    return x * _sigmoid(x)


def _cast_tile(w, axis, tile, n_tiles, d_ff):
    idx = lax.broadcasted_iota(jnp.int32, w.shape, axis)
    limit = jnp.where(tile == n_tiles - 1, d_ff - (n_tiles - 1) * CAST_TILE, CAST_TILE)
    return jnp.where(idx < limit, w, 0.0).astype(BF16)


def _cast_first_body(wg_ref, wu_ref, wd_ref, og_ref, ou_ref, od_ref, *, d_ff):
    j = pl.program_id(0)
    n = pl.num_programs(0)
    og_ref[...] = _cast_tile(wg_ref[...], 1, j, n, d_ff)
    ou_ref[...] = _cast_tile(wu_ref[...], 1, j, n, d_ff)
    od_ref[...] = _cast_tile(wd_ref[...], 0, j, n, d_ff)


def _cast_specs(D, index):
    per_ff = FF_TILE // CAST_TILE

    def src(kind):
        def f(*ids):
            l, i, t = index(kind, *ids)
            return (l, i, t, 0) if kind == 2 else (l, i, 0, t)
        shape = (None, None, CAST_TILE, D) if kind == 2 else (None, None, D, CAST_TILE)
        return pl.BlockSpec(shape, f)

    def dst(kind):
        def f(*ids):
            l, i, t = index(kind, *ids)
            return (l, i, t, 0) if kind == 2 else (l, i, t // per_ff, 0, t % per_ff)
        shape = (None, None, CAST_TILE, D) if kind == 2 else (None, None, None, D, CAST_TILE)
        return pl.BlockSpec(shape, f)

    return [src(k) for k in range(3)], [dst(k) for k in range(3)]


def _cast_first(w_f32, layer, idx):
    wg, wu, wd = w_f32
    n_l, n_i, D, F = wg.shape
    f_pad = _round_up(F, FF_TILE)
    n_tiles = f_pad // CAST_TILE
    assert 0 < F - (n_tiles - 1) * CAST_TILE <= CAST_TILE
    srcs, dsts = _cast_specs(D, lambda kind, j: (layer, idx, j))
    return pl.pallas_call(
        functools.partial(_cast_first_body, d_ff=F),
        grid=(n_tiles,),
        in_specs=srcs,
        out_specs=dsts,
        out_shape=[jax.ShapeDtypeStruct((n_l, n_i, f_pad // FF_TILE, D, FF_TILE), BF16)] * 2 + [
            jax.ShapeDtypeStruct((n_l, n_i, f_pad, D), BF16)],
        compiler_params=_cparams(1),
        name="cast_first",
    )(wg, wu, wd)


def _cast_into_body(w_ref, *rest):
    rest[-1][...] = w_ref[...].astype(BF16)


def _cast_into(w, out, n_cols, col0):
    n, K, F = w.shape
    ct = max(LANE, min(F, CAST_BLOCK_ELEMS // K))
    assert F % ct == 0 and col0 % ct == 0
    in_specs = [pl.BlockSpec((None, K, ct), lambda l, j: (l, 0, j))]
    args = [w]
    if out is not None:
        in_specs.append(pl.BlockSpec(memory_space=pl.ANY))
        args.append(out)
    return pl.pallas_call(
        _cast_into_body,
        grid=(n, F // ct),
        in_specs=in_specs,
        out_specs=pl.BlockSpec((None, K, ct), lambda l, j: (l, 0, col0 // ct + j)),
        out_shape=jax.ShapeDtypeStruct((n, K, n_cols), BF16),
        input_output_aliases={1: 0} if out is not None else {},
        compiler_params=_cparams(2),
        name="cast_into",
    )(*args)


def _cast_side_by_side(ws):
    n_cols = sum(w.shape[2] for w in ws)
    out, col0 = None, 0
    for w in ws:
        out = _cast_into(w, out, n_cols, col0)
        col0 += w.shape[2]
    return out


class CastJob(NamedTuple):
    targets: tuple
    d_ff: int
    n_tiles: int

    @property
    def span(self):
        return len(self.targets) * self.n_tiles

    def index(self, kind, t):
        tt = jnp.clip(t - kind * self.span, 0, self.span - 1)
        which = tt // self.n_tiles
        layer, idx = self.targets[0]
        for n, (l, i) in enumerate(self.targets[1:], 1):
            layer = jnp.where(which == n, l, layer)
            idx = jnp.where(which == n, i, idx)
        return layer, idx, tt % self.n_tiles


def _cast_step(job, t, src_refs, dst_refs):
    for kind in range(3):
        def convert(kind=kind):
            tile = (t - kind * job.span) % job.n_tiles
            dst_refs[kind][...] = _cast_tile(src_refs[kind][...], 0 if kind == 2 else 1, tile,
                                             job.n_tiles, job.d_ff)
        pl.when((t >= kind * job.span) & (t < (kind + 1) * job.span))(convert)


def _ffn_body(*refs, mode, n_chunks, n_tail):
    refs = list(refs)
    x_refs = [refs.pop(0) for _ in range(2 if mode == "first" else 1)]
    gain_ref, wg_ref, wu_ref, wd_ref, fgain_ref = refs[:5]
    if mode == "last":
        y4_ref, yt_ref, h_ref, acc_ref = refs[5:]
    else:
        acc_ref, h_ref = refs[5:]
    i = pl.program_id(0)
    j = pl.program_id(1)

    def init(x):
        h_ref[...] = (_rms(x) * gain_ref[...]).astype(BF16)
        acc_ref[...] = x

    if mode == "first":
        pl.when((j == 0) & (i < n_chunks))(lambda: init(x_refs[0][...].reshape(acc_ref.shape)))
        pl.when((j == 0) & (i == n_chunks))(lambda: init(x_refs[1][...]))
    elif mode == "last":
        @pl.when(j == 0)
        def _():
            rows = lax.broadcasted_iota(jnp.int32, acc_ref.shape, 0)
            init(jnp.where(rows < jnp.where(i == n_chunks, n_tail, acc_ref.shape[0]), x_refs[0][...], 0.0))
    else:
        pl.when(j == 0)(lambda: init(x_refs[0][...]))

    h = h_ref[...]
    g = jnp.dot(h, wg_ref[...], preferred_element_type=F32)
    u = jnp.dot(h, wu_ref[...], preferred_element_type=F32)
    a = (0.5 * _silu(g) * u).astype(BF16)
    acc_ref[...] += jnp.dot(a, wd_ref[...], preferred_element_type=F32)

    if mode == "last":
        last_j = j == pl.num_programs(1) - 1

        @pl.when(last_j & (i < n_chunks))
        def _():
            y4_ref[...] = (_rms(acc_ref[...]) * fgain_ref[...]).reshape(y4_ref.shape)

        @pl.when(last_j & (i == n_chunks))
        def _():
            yt_ref[...] = _rms(acc_ref[...]) * fgain_ref[...]


def _ffn(x, gain, w_bf, layer, idx, tm, mode="mid", fgain=None, n_rows=None, home=None):
    wg, wu, wd = w_bf
    nj = wg.shape[2]
    D = wg.shape[3]
    w_specs = [
        pl.BlockSpec((None, None, None, D, FF_TILE), lambda i, j: (layer, idx, j, 0, 0)),
        pl.BlockSpec((None, None, None, D, FF_TILE), lambda i, j: (layer, idx, j, 0, 0)),
        pl.BlockSpec((None, None, FF_TILE, D), lambda i, j: (layer, idx, j, 0)),
    ]
    vec_spec = pl.BlockSpec((1, D), lambda i, j: (0, 0))
    if fgain is None:
        fgain = gain
    n_chunks = n_tail = 0
    if mode == "mid":
        R = x.shape[0]
        grid = (R // tm, nj)
        x_args, x_specs = [x], [pl.BlockSpec((tm, D), lambda i, j: (i, 0))]
        out_specs = pl.BlockSpec((tm, D), lambda i, j: (i, 0))
        out_shape = jax.ShapeDtypeStruct((R, D), F32)
        scratch = [pltpu.VMEM((tm, D), BF16)]
    else:
        if mode == "first":
            home, R = x[0].shape, n_rows
        else:
            R = x.shape[0]
        B, n_chunks, C, _ = home
        tm = B * C
        n_tail = R - n_chunks * tm
        assert 0 < n_tail <= tm
        grid = (n_chunks + 1, nj)
        home_spec = pl.BlockSpec((B, None, C, D), lambda i, j: (0, jnp.minimum(i, n_chunks - 1), 0, 0))
        tail_spec = pl.BlockSpec((tm, D), lambda i, j: (0, 0))
        rows_spec = pl.BlockSpec((tm, D), lambda i, j: (i, 0))
        if mode == "first":
            x_args, x_specs = list(x), [home_spec, tail_spec]
            out_specs, out_shape = rows_spec, jax.ShapeDtypeStruct((R, D), F32)
            scratch = [pltpu.VMEM((tm, D), BF16)]
        else:
            x_args, x_specs = [x], [rows_spec]
            out_specs = [home_spec, tail_spec]
            out_shape = [jax.ShapeDtypeStruct(home, F32), jax.ShapeDtypeStruct((tm, D), F32)]
            scratch = [pltpu.VMEM((tm, D), BF16), pltpu.VMEM((tm, D), F32)]
    return pl.pallas_call(
        functools.partial(_ffn_body, mode=mode, n_chunks=n_chunks, n_tail=n_tail),
        grid=grid,
        in_specs=x_specs + [vec_spec] + w_specs + [vec_spec],
        out_specs=out_specs,
        out_shape=out_shape,
        scratch_shapes=scratch,
        compiler_params=_cparams(2),
        name="ffn_" + mode,
    )(*x_args, gain.reshape(1, D), wg, wu, wd, fgain.reshape(1, D))


def _proj_body(x_ref, gain_ref, w_ref, o_ref, h_ref, *, n_silu):
    j = pl.program_id(1)

    @pl.when(j == 0)
    def _():
        h_ref[...] = (_rms(x_ref[...]) * gain_ref[...]).astype(BF16)

    y = jnp.dot(h_ref[...], w_ref[...], preferred_element_type=F32)
    if n_silu > 0:
        y = jnp.where(j < n_silu, _silu(y), y)
    o_ref[...] = y.astype(BF16)


def _proj(x, gain, w, layer, n_silu_cols, tm):
    R, D = x.shape
    N = w.shape[2]
    return pl.pallas_call(
        functools.partial(_proj_body, n_silu=n_silu_cols // PROJ_TILE),
        grid=(R // tm, N // PROJ_TILE),
        in_specs=[
            pl.BlockSpec((tm, D), lambda i, j: (i, 0)),
            pl.BlockSpec((1, D), lambda i, j: (0, 0)),
            pl.BlockSpec((None, D, PROJ_TILE), lambda i, j: (layer, 0, j)),
        ],
        out_specs=pl.BlockSpec((tm, PROJ_TILE), lambda i, j: (i, j)),
        out_shape=jax.ShapeDtypeStruct((R, N), BF16),
        scratch_shapes=[pltpu.VMEM((tm, D), BF16)],
        compiler_params=_cparams(2),
        name="proj",
    )(x, gain.reshape(1, D), w)


def _lower_bound(logits, layer):
    e = jnp.exp(logits - jnp.max(logits, axis=0, keepdims=True))
    probs = e / jnp.sum(e, axis=0, keepdims=True)
    cs = probs[0:1]
    for i in range(1, layer + 1):
        cs = cs + probs[i:i + 1]
    return cs - probs[0:1]


def _gate_proj_body(x_ref, gain_ref, w_ref, logit_ref, g_ref, k_ref, h_ref, *, layer):
    j = pl.program_id(1)

    @pl.when(j == 0)
    def _():
        h_ref[...] = (_rms(x_ref[...]) * gain_ref[...]).astype(BF16)

    z = jnp.dot(h_ref[...], w_ref[...], preferred_element_type=F32)
    lb = _lower_bound(logit_ref[...], layer)
    t = jnp.exp(-jnp.abs(z))
    r = 1.0 / (1.0 + t)
    tr = t * r
    pos = z >= 0.0
    g_ref[...] = jnp.log(jnp.maximum(lb, LB_FLOOR) + (1.0 - lb) * jnp.where(pos, r, tr))
    k_ref[...] = ((1.0 - lb) * jnp.where(pos, tr, r)).astype(BF16)


def _gate_proj(x, gain, w, logits, layer, tm):
    R, D = x.shape
    N = w.shape[2]
    spec = pl.BlockSpec((tm, GATE_TILE), lambda i, j: (i, j))
    return pl.pallas_call(
        functools.partial(_gate_proj_body, layer=layer),
        grid=(R // tm, N // GATE_TILE),
        in_specs=[
            pl.BlockSpec((tm, D), lambda i, j: (i, 0)),
            pl.BlockSpec((1, D), lambda i, j: (0, 0)),
            pl.BlockSpec((None, D, GATE_TILE), lambda i, j: (layer, 0, j)),
            pl.BlockSpec((logits.shape[0], GATE_TILE), lambda i, j: (0, j)),
        ],
        out_specs=[spec, spec],
        out_shape=[jax.ShapeDtypeStruct((R, N), F32), jax.ShapeDtypeStruct((R, N), BF16)],
        scratch_shapes=[pltpu.VMEM((tm, D), BF16)],
        compiler_params=_cparams(2),
        name="gate_proj",
    )(x, gain.reshape(1, D), w, logits)


def _outproj_body(a_ref, w_ref, x_ref, o_ref):
    o_ref[...] = x_ref[...] + jnp.dot(a_ref[...], w_ref[...], preferred_element_type=F32)


def _outproj(a, w, layer, x, tm):
    R, K = a.shape
    D = w.shape[2]
    return pl.pallas_call(
        _outproj_body,
        grid=(R // tm, D // OUT_TILE),
        in_specs=[
            pl.BlockSpec((tm, K), lambda i, j: (i, 0)),
            pl.BlockSpec((None, K, OUT_TILE), lambda i, j: (layer, 0, j)),
            pl.BlockSpec((tm, OUT_TILE), lambda i, j: (i, j)),
        ],
        out_specs=pl.BlockSpec((tm, OUT_TILE), lambda i, j: (i, j)),
        out_shape=jax.ShapeDtypeStruct((R, D), F32),
        compiler_params=_cparams(2),
        name="outproj",
    )(a, w, x)


def _col_bcast(x):
    bs = x.shape[0]
    return jnp.concatenate([x] * (LANE // bs), axis=0).T


def _sample_rows(s):
    grp = s - 1
    n = SAMPLES_PER_STEP
    rows = pl.ds(pl.multiple_of(grp * n, n), n)
    rows2 = pl.ds(pl.multiple_of((grp // 2) * 2 * n, 2 * n), 2 * n)
    return rows, rows2, (grp % 2) == 1


def _half(x2, second):
    n = SAMPLES_PER_STEP
    return jnp.where(second, x2[n:], x2[:n])


def _mixer_call(body, name, grid, in_specs, args, out_specs, out_shape, scratch_shapes, state, new_state,
                w_f32, w_bf, job, D):
    n_s = grid[1]
    srcs, dsts = _cast_specs(D, lambda kind, h, s: job.index(kind, h * n_s + s))
    any_spec = pl.BlockSpec(memory_space=pl.ANY)
    in_specs = list(in_specs) + srcs
    args = list(args) + list(w_f32)
    out_specs = list(out_specs) + dsts
    out_shape = list(out_shape) + [jax.ShapeDtypeStruct(w.shape, w.dtype) for w in w_bf]
    n_out = len(out_shape)
    aliases = {}
    if new_state is not None:
        aliases[len(args)] = n_out - 4
        in_specs.append(any_spec)
        args.append(new_state)
    for n, w in enumerate(w_bf):
        aliases[len(args)] = n_out - 3 + n
        in_specs.append(any_spec)
        args.append(w)
    return pl.pallas_call(
        functools.partial(body, n_alias=len(aliases)),
        grid=grid, in_specs=in_specs, out_specs=out_specs, out_shape=out_shape,
        scratch_shapes=scratch_shapes, input_output_aliases=aliases,
        compiler_params=_cparams(2), name=name,
    )(*args)


def _hgrn_tables(C):
    n_lvl = int(math.log2(C))
    assert 1 << n_lvl == C
    n_fine = min(n_lvl, int(math.log2(SUBLANE)) - 1)
    r = np.arange(C)
    msum = np.zeros((n_fine + 1, C, C), np.float32)
    pair = np.zeros((n_lvl + 1, C, C), np.float32)
    u = r[None, :]
    for l in range(n_lvl):
        half = 1 << l
        blk = 2 * half
        m = r - r % blk + half - 1
        up = (r % blk) >= half
        if l < n_fine:
            msum[l] = np.where(up[:, None], (u > m[:, None]) & (u <= r[:, None]),
                               (u > r[:, None]) & (u <= m[:, None]))
        pair[l] = ((r[:, None] // blk) == (r[None, :] // blk)) & up[:, None] & ~up[None, :]
    pair[n_lvl] = np.eye(C)
    msum[n_fine] = u <= r[:, None]
    return jnp.asarray(msum.reshape(-1, C), BF16), jnp.asarray(pair, F32)


def _split3(g):
    hi = g.astype(BF16)
    r1 = g - hi.astype(F32)
    mid = r1.astype(BF16)
    lo = (r1 - mid.astype(F32)).astype(BF16)
    return hi, mid, lo


def _hgrn_chunks(qs, gs, ks, vs, gates, hgains, sts, msum, pair_ref, valid, side=()):
    side = list(side)
    n_lvl = pair_ref.shape[0] - 1
    per_phase = -(-len(side) // (n_lvl + 1))

    def run_side():
        for _ in range(min(per_phase, len(side))):
            side.pop(0)()

    n = range(len(qs))
    C = qs[0].shape[0]
    n_fine = msum.shape[0] // C - 1
    qf = [q.astype(F32) for q in qs]
    kf = [k.astype(F32) for k in ks]
    if valid is not None:
        gs = [g * valid for g in gs]
        kf = [k * valid for k in kf]
        ks = [k.astype(BF16) for k in kf]
    pieces = [_split3(g) for g in gs]
    e2 = [jnp.dot(msum, jnp.concatenate(p[:2], axis=1), preferred_element_type=F32) for p in pieces]
    e = [t[:, :LANE] + t[:, LANE:] for t in e2]
    c = [e[b][n_fine * C:] + jnp.dot(msum[n_fine * C:], pieces[b][2], preferred_element_type=F32)
         for b in n]
    x_fine = [jnp.exp(t[:n_fine * C]) for t in e]
    run_side()

    scores = [lax.dot_general(qs[b], ks[b], _NT, preferred_element_type=F32) * pair_ref[n_lvl] for b in n]
    for l in range(n_lvl):
        if l < n_fine:
            xl = [t[l * C:(l + 1) * C] for t in x_fine]
        else:
            half = 1 << l
            xl = []
            for t in c:
                c3 = t.reshape(C // (2 * half), 2 * half, LANE)
                cm = c3[:, half - 1:half, :]
                xl.append(jnp.exp(-jnp.abs(c3 - cm)).reshape(C, LANE))
        scores = [scores[b] + lax.dot_general((qf[b] * xl[b]).astype(BF16), (kf[b] * xl[b]).astype(BF16), _NT,
                                              preferred_element_type=F32) * pair_ref[l] for b in n]
        run_side()
    while side:
        side.pop(0)()
    xc = [jnp.exp(t) for t in c]
    xt = [jnp.exp(t[C - 1:C, :] - t) for t in c]
    o = [lax.dot_general((qf[b] * xc[b]).astype(BF16), sts[b].astype(BF16), _NT, preferred_element_type=F32)
         for b in n]
    o = [o[b] + jnp.dot(scores[b].astype(BF16), vs[b], preferred_element_type=F32) for b in n]
    st_new = [sts[b] * xc[b][C - 1:C, :] + lax.dot_general(vs[b], (kf[b] * xt[b]).astype(BF16), _TN,
                                                           preferred_element_type=F32) for b in n]
    og = [(_rms(o[b]) * hgains[b] * gates[b].astype(F32)).astype(BF16) for b in n]
    return og, st_new


def _hgrn_mixer_body(*refs, n_batch, n_chunks, n_heads, job, n_alias):
    (q_ref, gate_ref, v_ref, g_ref, k_ref, hgain_ref, msum_ref, pair_ref,
     qs_ref, gates_ref, vs_ref, gs_ref, ks_ref, s_ref) = refs[:14]
    w_src = refs[14:17]
    og_ref, so_ref, ogs_ref, sn_ref = refs[17 + n_alias:21 + n_alias]
    w_dst = refs[21 + n_alias:24 + n_alias]
    st_ref, o_scr = refs[24 + n_alias:]
    C = CHUNK
    s = pl.program_id(1)
    _cast_step(job, pl.program_id(0) * pl.num_programs(1) + s, w_src, w_dst)

    def step(seqs, sts, valid, side=()):
        tiles = [(r, pl.ds(hh * LANE, LANE)) for r, hh in seqs]
        return _hgrn_chunks([q_ref[t] for t in tiles], [g_ref[t] for t in tiles], [k_ref[t] for t in tiles],
                            [v_ref[t] for t in tiles], [gate_ref[t] for t in tiles],
                            [hgain_ref[:, t[1]] for t in tiles], sts, msum_ref[...], pair_ref, valid, side)

    @pl.when(s == 0)
    def _():
        valid = (lax.broadcasted_iota(jnp.int32, (C, 1), 0) < N_META).astype(F32)
        seqs = [(pl.ds(C, C), hh) for hh in range(n_heads)]
        og, st_new = step(seqs, [jnp.zeros((LANE, HA_DK), F32)] * n_heads, valid)
        og_ref[...] = jnp.zeros_like(og_ref)
        for hh in range(n_heads):
            og_ref[pl.ds(C, C), pl.ds(hh * LANE, LANE)] = og[hh]
            for b in range(n_batch):
                st_ref[hh * n_batch + b] = st_new[hh]

    @pl.when(s > 0)
    def _():
        rows, rows2, second = _sample_rows(s)
        side = []
        for hh in range(n_heads):
            lanes = pl.ds(hh * LANE, LANE)
            q2 = qs_ref[rows2, lanes]
            v1 = _half(vs_ref[rows2, lanes].astype(F32), second)
            ft = _col_bcast(jnp.exp(gs_ref[rows, lanes]))
            kt = _col_bcast(_half(ks_ref[rows2, lanes].astype(F32), second))

            def update(b, hh=hh, q2=q2, v1=v1, ft=ft, kt=kt):
                sn = ft[:, b:b + 1] * s_ref[b, hh] + kt[:, b:b + 1] * v1[b:b + 1, :]
                sn_ref[b, hh] = sn
                res = jnp.dot(q2, sn.astype(BF16), preferred_element_type=F32)
                o_scr[hh, b:b + 1, :] = _half(res, second)[b:b + 1, :]

            def finish(hh=hh, lanes=lanes):
                gate1 = _half(gates_ref[rows2, lanes].astype(F32), second)
                ogs_ref[rows, lanes] = _rms(o_scr[hh]) * hgain_ref[:, lanes] * gate1

            side += [functools.partial(update, b) for b in range(SAMPLES_PER_STEP)] + [finish]

        seqs = [(pl.ds(b * C, C), hh) for hh in range(n_heads) for b in range(n_batch)]
        og, st_new = step(seqs, [st_ref[i] for i in range(len(seqs))], None, side)
        for i, (r, hh) in enumerate(seqs):
            og_ref[r, pl.ds(hh * LANE, LANE)] = og[i]
            st_ref[i] = st_new[i]

    @pl.when(s == n_chunks)
    def _():
        for hh in range(n_heads):
            for b in range(n_batch):
                so_ref[b, hh] = st_ref[hh * n_batch + b].T


def _hgrn_mixer(pb, g, k, hgain, state, new_state, layer, n_batch, seq, tables, sample_blk, w_f32, w_bf, job):
    R = pb.shape[0]
    D = pb.shape[1] // 3
    H = D // HA_DK
    C = CHUNK
    B = n_batch
    hp = HGRN_HEADS_PER_STEP
    n_chunks = seq // C
    n_sample = state.shape[1]
    msum, pair = tables
    grid = (H // hp, 1 + n_chunks)
    assert B >= 2 and R >= n_chunks * B * C + 2 * C and H % hp == 0
    assert n_sample == SAMPLES_PER_STEP * n_chunks and 3 * job.span <= grid[0] * grid[1]

    def spec(col0):
        return pl.BlockSpec((B * C, hp * LANE),
                            lambda h, s: (jnp.where(s == 0, n_chunks, s - 1), col0 // hp + h))

    def sspec(col0):
        return pl.BlockSpec((LANE, hp * LANE), lambda h, s: (sample_blk, col0 // hp + h))

    state_spec = pl.BlockSpec((None, SAMPLES_PER_STEP, hp, HA_DK, LANE),
                              lambda h, s: (layer, jnp.maximum(s - 1, 0), h, 0, 0))
    out = _mixer_call(
        functools.partial(_hgrn_mixer_body, n_batch=B, n_chunks=n_chunks, n_heads=hp, job=job),
        "hgrn_mixer", grid,
        in_specs=[spec(0), spec(H), spec(2 * H), spec(0), spec(0),
                  pl.BlockSpec((1, hp * LANE), lambda h, s: (0, h)),
                  pl.BlockSpec(msum.shape, lambda h, s: (0, 0)),
                  pl.BlockSpec(pair.shape, lambda h, s: (0, 0, 0)),
                  sspec(0), sspec(H), sspec(2 * H), sspec(0), sspec(0), state_spec],
        args=[pb, pb, pb, g, k, hgain.reshape(1, D), msum, pair, pb, pb, pb, g, k, state],
        out_specs=[spec(0), pl.BlockSpec((B, hp, HA_DK, LANE), lambda h, s: (0, h, 0, 0)),
                   pl.BlockSpec((LANE, hp * LANE), lambda h, s: (0, h)), state_spec],
        out_shape=[jax.ShapeDtypeStruct((R, D), BF16), jax.ShapeDtypeStruct((B, H, HA_DK, LANE), F32),
                   jax.ShapeDtypeStruct((LANE, D), F32), jax.ShapeDtypeStruct(state.shape, F32)],
        scratch_shapes=[pltpu.VMEM((hp * B, LANE, HA_DK), F32), pltpu.VMEM((hp, SAMPLES_PER_STEP, LANE), F32)],
        state=state, new_state=new_state, w_f32=w_f32, w_bf=w_bf, job=job, D=D)
    return out[0], out[1], out[2], out[3], tuple(out[4:])


def _rotary(x, cos, sin):
    half = x.shape[-1] // 2
    x1, x2 = x[:, :half], x[:, half:]
    return jnp.concatenate([x1 * cos - x2 * sin, x1 * sin + x2 * cos], axis=-1)


def _ret_chunks(qs, ks, vs, gates, cos, sin, lg, sts, n_valid):
    n = range(len(qs))
    C, dk = qs[0].shape
    t_col = lax.broadcasted_iota(jnp.int32, (C, 1), 0)
    t_row = lax.broadcasted_iota(jnp.int32, (1, C), 1)
    n_col = jnp.minimum(t_col + 1, n_valid).astype(F32)
    n_row = jnp.minimum(t_row + 1, n_valid).astype(F32)
    n_last = float(min(C, n_valid))
    decay = jnp.where(t_col >= t_row, jnp.exp(lg * (n_col - n_row)), 0.0)
    q_scale = jnp.exp(lg * n_col)
    k_scale = jnp.exp(lg * (n_last - n_col))
    k_norm = dk ** -0.5
    if n_valid < C:
        k_norm = k_norm * (t_col < n_valid).astype(F32)
    q = [_rotary(t.astype(F32), cos, sin) for t in qs]
    k = [_rotary(t.astype(F32), cos, sin) * k_norm for t in ks]
    scores = [lax.dot_general(q[b].astype(BF16), k[b].astype(BF16), _NT, preferred_element_type=F32) * decay
              for b in n]
    o = [jnp.dot((q[b] * q_scale).astype(BF16), sts[b].astype(BF16), preferred_element_type=F32) for b in n]
    o = [o[b] + jnp.dot(scores[b].astype(BF16), vs[b], preferred_element_type=F32) for b in n]
    st_new = [jnp.exp(lg * n_last) * sts[b] + lax.dot_general((k[b] * k_scale).astype(BF16), vs[b], _TN,
                                                              preferred_element_type=F32) for b in n]
    og = [(_rms(o[b]) * gates[b].astype(F32)).astype(BF16) for b in n]
    return og, st_new


def _ret_mixer_body(*refs, n_batch, n_chunks, job, n_alias):
    (lg_ref, q_ref, k_ref, v_ref, gate_ref, cos_ref, sin_ref,
     qs_ref, ks_ref, vs_ref, gates_ref, coss_ref, sins_ref, s_ref) = refs[:14]
    w_src = refs[14:17]
    og_ref, so_ref, ogs_ref, sn_ref = refs[17 + n_alias:21 + n_alias]
    w_dst = refs[21 + n_alias:24 + n_alias]
    st_ref, o_scr = refs[24 + n_alias:]
    C = CHUNK
    lg = lg_ref[pl.program_id(0)]
    s = pl.program_id(1)
    _cast_step(job, pl.program_id(0) * pl.num_programs(1) + s, w_src, w_dst)

    def step(rows, sts, n_valid):
        return _ret_chunks([q_ref[r, :] for r in rows], [k_ref[r, :] for r in rows], [v_ref[r, :] for r in rows],
                           [gate_ref[r, :] for r in rows], cos_ref[...], sin_ref[...], lg, sts, n_valid)

    @pl.when(s == 0)
    def _():
        og, st_new = step([pl.ds(C, C)], [jnp.zeros(st_ref.shape[1:], F32)], N_META)
        og_ref[...] = jnp.zeros_like(og_ref)
        og_ref[pl.ds(C, C), :] = og[0]
        for b in range(n_batch):
            st_ref[b] = st_new[0]

    @pl.when(s > 0)
    def _():
        rows, rows2, second = _sample_rows(s)
        dk = qs_ref.shape[-1]
        gamma = jnp.exp(lg)
        cos, sin = coss_ref[...], sins_ref[...]
        q = _rotary(_half(qs_ref[rows2, :].astype(F32), second), cos, sin)
        k = _rotary(_half(ks_ref[rows2, :].astype(F32), second), cos, sin) * (dk ** -0.5)
        v = _half(vs_ref[rows2, :].astype(F32), second)
        qt = jnp.concatenate([_col_bcast(q[:, :LANE]), _col_bcast(q[:, LANE:])], axis=0)
        kt = jnp.concatenate([_col_bcast(k[:, :LANE]), _col_bcast(k[:, LANE:])], axis=0)
        for b in range(SAMPLES_PER_STEP):
            sn = gamma * s_ref[b] + kt[:, b:b + 1] * v[b:b + 1, :]
            sn_ref[b] = sn
            o_scr[b:b + 1, :] = jnp.sum(qt[:, b:b + 1] * sn, axis=0, keepdims=True)
        ogs_ref[rows, :] = _rms(o_scr[...]) * _half(gates_ref[rows2, :].astype(F32), second)

        og, st_new = step([pl.ds(b * C, C) for b in range(n_batch)], [st_ref[b] for b in range(n_batch)], C)
        for b in range(n_batch):
            og_ref[pl.ds(b * C, C), :] = og[b]
            st_ref[b] = st_new[b]

    @pl.when(s == n_chunks)
    def _():
        so_ref[...] = st_ref[...]


def _ret_mixer(p, log_gamma, cos_tab, sin_tab, cos_s, sin_s, state, new_state, layer, n_batch, seq,
               sample_blk, w_f32, w_bf, job):
    R = p.shape[0]
    D = p.shape[1] // 6
    H = HB_HEADS
    dk = D // H
    dv = 2 * D // H
    C = CHUNK
    B = n_batch
    n_chunks = seq // C
    n_sample = state.shape[1]
    qk_off = 2 * D // dk
    v_off = 4 * D // dv
    grid = (H, 1 + n_chunks)
    assert B >= 2 and R >= n_chunks * B * C + 2 * C and dk == 2 * LANE
    assert n_sample == SAMPLES_PER_STEP * n_chunks and 3 * job.span <= grid[0] * grid[1]

    def spec(width, col0):
        return pl.BlockSpec((B * C, width), lambda h, s: (jnp.where(s == 0, n_chunks, s - 1), col0 + h))

    def sspec(width, col0):
        return pl.BlockSpec((LANE, width), lambda h, s: (sample_blk, col0 + h))

    tab_spec = pl.BlockSpec((C, dk // 2), lambda h, s: (s, 0))
    pos_spec = pl.BlockSpec((1, dk // 2), lambda h, s: (0, 0))
    state_spec = pl.BlockSpec((None, SAMPLES_PER_STEP, None, dk, dv),
                              lambda h, s: (layer, jnp.maximum(s - 1, 0), h, 0, 0))
    out = _mixer_call(
        functools.partial(_ret_mixer_body, n_batch=B, n_chunks=n_chunks, job=job),
        "ret_mixer", grid,
        in_specs=[pl.BlockSpec(memory_space=pltpu.SMEM),
                  spec(dk, qk_off), spec(dk, qk_off + H), spec(dv, v_off), spec(dv, 0), tab_spec, tab_spec,
                  sspec(dk, qk_off), sspec(dk, qk_off + H), sspec(dv, v_off), sspec(dv, 0), pos_spec, pos_spec,
                  state_spec],
        args=[log_gamma, p, p, p, p, cos_tab, sin_tab, p, p, p, p, cos_s, sin_s, state],
        out_specs=[spec(dv, 0), pl.BlockSpec((B, None, dk, dv), lambda h, s: (0, h, 0, 0)),
                   pl.BlockSpec((LANE, dv), lambda h, s: (0, h)), state_spec],
        out_shape=[jax.ShapeDtypeStruct((R, 2 * D), BF16), jax.ShapeDtypeStruct((B, H, dk, dv), F32),
                   jax.ShapeDtypeStruct((LANE, 2 * D), F32), jax.ShapeDtypeStruct(state.shape, F32)],
        scratch_shapes=[pltpu.VMEM((B, dk, dv), F32), pltpu.VMEM((SAMPLES_PER_STEP, dv), F32)],
        state=state, new_state=new_state, w_f32=w_f32, w_bf=w_bf, job=job, D=D)
    return out[0], out[1], out[2], out[3], tuple(out[4:])


def _rope_tables(pos, half):
    inv = ROPE_BASE ** (-jnp.linspace(0.0, 1.0, half, dtype=F32))
    ang = pos.astype(F32)[:, None] * inv[None, :]
    return jnp.cos(ang), jnp.sin(ang)


def kernel(x_prompt, x_sample, state_hgrn, state_ret, meta_tokens, norm_ffn, ffn_w_gate, ffn_w_up, ffn_w_down, norm_mix, hg_wq, hg_wf, hg_wi, hg_wg, hg_wo, hg_norm, hg_lb_logits, rt_wq, rt_wk, rt_wv, rt_wg, rt_wo, norm_final):
    B, L, D = x_prompt.shape
    NS = x_sample.shape[0]
    depth = norm_mix.shape[0]
    n_meta = meta_tokens.shape[0]
    C = CHUNK
    assert n_meta == N_META and x_sample.shape[1] == 1 and L % C == 0 and NS <= LANE
    n_chunks = L // C

    tp = B * L
    sp = LANE
    rows0 = tp + sp + C
    tm = next(t for t in ROW_TILES if _round_up(rows0, t) - rows0 < LANE) if rows0 >= ROW_TILES[0] else rows0
    R = _round_up(rows0, tm)
    home = (B, n_chunks, C, D)
    x_tail = jnp.concatenate([
        x_sample.reshape(NS, D), jnp.zeros((sp - NS, D), F32),
        meta_tokens.astype(F32), jnp.zeros((B * C - sp - n_meta, D), F32)], axis=0)
    sample_blk = tp // LANE

    w_f32 = (ffn_w_gate, ffn_w_up, ffn_w_down)
    d_ff = ffn_w_gate.shape[-1]
    n_cast_tiles = _round_up(d_ff, FF_TILE) // CAST_TILE
    w_bf = _cast_first(w_f32, 0, 0)

    hg_in = _cast_side_by_side([hg_wq, hg_wg, hg_wi])
    hg_f = _cast_side_by_side([hg_wf])
    hg_out = _cast_side_by_side([hg_wo])
    rt_in = _cast_side_by_side([rt_wg, rt_wq, rt_wk, rt_wv])
    rt_out = _cast_side_by_side([rt_wo])

    half = D // HB_HEADS // 2
    cos_p, sin_p = _rope_tables(jnp.arange(n_meta + L, dtype=jnp.int32), half)
    cos_s, sin_s = _rope_tables(PAST_LEN + jnp.arange(1, dtype=jnp.int32), half)

    def chunk_table(t):
        meta = jnp.concatenate([t[:n_meta], jnp.zeros((C - n_meta, half), F32)], axis=0)
        return jnp.concatenate([meta, t[n_meta:]], axis=0)

    cos_tab, sin_tab = chunk_table(cos_p), chunk_table(sin_p)
    log_gamma = jnp.log1p(-jnp.exp2(-5.0 - jnp.arange(HB_HEADS, dtype=F32)))
    tables = _hgrn_tables(C)

    new_hgrn_p, new_ret_p = [], []
    new_hgrn_s = new_ret_s = None
    for i in range(depth):
        if i == 0:
            x = _ffn((x_prompt.reshape(home), x_tail), norm_ffn[i, 0], w_bf, i, 0, tm, mode="first", n_rows=R)
        else:
            x = _ffn(x, norm_ffn[i, 0], w_bf, i, 0, tm)
        j = i // 2
        job = CastJob(((i, 1),) + (((i + 1, 0),) if i + 1 < depth else ()), d_ff, n_cast_tiles)
        if i % 2 == 0:
            pb = _proj(x, norm_mix[i], hg_in, j, 2 * D, tm)
            g, k = _gate_proj(x, norm_mix[i], hg_f, hg_lb_logits, j, tm)
            og, s_p, og_s, new_hgrn_s, w_bf = _hgrn_mixer(
                pb, g, k, hg_norm[j], state_hgrn, new_hgrn_s, j, B, L, tables, sample_blk, w_f32, w_bf, job)
            new_hgrn_p.append(s_p)
            w_out = hg_out
        else:
            pb = _proj(x, norm_mix[i], rt_in, j, 2 * D, tm)
            og, s_p, og_s, new_ret_s, w_bf = _ret_mixer(
                pb, log_gamma, cos_tab, sin_tab, cos_s, sin_s, state_ret, new_ret_s, j, B, L,
                sample_blk, w_f32, w_bf, job)
            new_ret_p.append(s_p)
            w_out = rt_out
        og = lax.dynamic_update_slice(og, og_s[:NS].astype(BF16), (tp, 0))
        x = _outproj(og, w_out, j, x, tm)
        if i < depth - 1:
            x = _ffn(x, norm_ffn[i, 1], w_bf, i, 1, tm)
    y_home, y_tail = _ffn(x, norm_ffn[depth - 1, 1], w_bf, depth - 1, 1, tm, mode="last", fgain=norm_final,
                          home=home)
    y_prompt = y_home.reshape(B, L, D)
    y_sample = y_tail[:NS].reshape(NS, 1, D)
    return (y_prompt, y_sample, jnp.stack(new_hgrn_p), new_hgrn_s, jnp.stack(new_ret_p), new_ret_s)
```

```python
import functools
import math
from typing import NamedTuple

import numpy as np
import jax
import jax.numpy as jnp
from jax import lax
from jax.experimental import pallas as pl
from jax.experimental.pallas import tpu as pltpu

F32 = jnp.float32
BF16 = jnp.bfloat16

EPS = 1e-6
LB_FLOOR = 1e-30
ROPE_BASE = 10000.0
N_META = 16
PAST_LEN = 16384
HA_DK = 128
HB_HEADS = 8

LANE = 128
SUBLANE = 8
CHUNK = 128
HGRN_HEADS_PER_STEP = 2
SAMPLES_PER_STEP = SUBLANE
FF_TILE = 512
CAST_TILE = 512
CAST_BLOCK_ELEMS = 1 << 20
PROJ_TILE = 2048
GATE_TILE = 1024
OUT_TILE = 1024
ROW_TILES = (768, 512, 256, 128)
VMEM_LIMIT = 56 * 1024 * 1024

_NT = (((1,), (1,)), ((), ()))
_TN = (((0,), (0,)), ((), ()))


def _round_up(a, m):
    return (a + m - 1) // m * m


def _cparams(n_axes):
    return pltpu.CompilerParams(dimension_semantics=("arbitrary",) * n_axes,
                                vmem_limit_bytes=VMEM_LIMIT)


def _rms(x):
    return x * lax.rsqrt(jnp.mean(x * x, axis=-1, keepdims=True) + EPS)


def _sigmoid(x):
    return 1.0 / (1.0 + jnp.exp(-x))


def _silu(x):
    return x * _sigmoid(x)


def _cast_slab(src_ref, dst_ref, axis, active, tile, n_tiles, d_ff):
    last = tile == n_tiles - 1

    @pl.when(active & jnp.logical_not(last))
    def _():
        dst_ref[...] = src_ref[...].astype(BF16)

    @pl.when(active & last)
    def _():
        w = src_ref[...]
        idx = lax.broadcasted_iota(jnp.int32, w.shape, axis)
        dst_ref[...] = jnp.where(idx < d_ff - (n_tiles - 1) * CAST_TILE, w, 0.0).astype(BF16)


def _cast_first_body(wg_ref, wu_ref, wd_ref, og_ref, ou_ref, od_ref, *, d_ff):
    j = pl.program_id(0)
    n = pl.num_programs(0)
    every_step = j >= 0
    _cast_slab(wg_ref, og_ref, 1, every_step, j, n, d_ff)
    _cast_slab(wu_ref, ou_ref, 1, every_step, j, n, d_ff)
    _cast_slab(wd_ref, od_ref, 0, every_step, j, n, d_ff)


def _cast_specs(D, index):
    per_ff = FF_TILE // CAST_TILE

    def src(kind):
        def f(*ids):
            l, i, t = index(kind, *ids)
            return (l, i, t, 0) if kind == 2 else (l, i, 0, t)
        shape = (None, None, CAST_TILE, D) if kind == 2 else (None, None, D, CAST_TILE)
        return pl.BlockSpec(shape, f)

    def dst(kind):
        def f(*ids):
            l, i, t = index(kind, *ids)
            return (l, i, t, 0) if kind == 2 else (l, i, t // per_ff, 0, t % per_ff)
        shape = (None, None, CAST_TILE, D) if kind == 2 else (None, None, None, D, CAST_TILE)
        return pl.BlockSpec(shape, f)

    return [src(k) for k in range(3)], [dst(k) for k in range(3)]


def _cast_first(w_f32, layer, idx):
    wg, wu, wd = w_f32
    n_l, n_i, D, F = wg.shape
    f_pad = _round_up(F, FF_TILE)
    n_tiles = f_pad // CAST_TILE
    assert 0 < F - (n_tiles - 1) * CAST_TILE <= CAST_TILE
    srcs, dsts = _cast_specs(D, lambda kind, j: (layer, idx, j))
    return pl.pallas_call(
        functools.partial(_cast_first_body, d_ff=F),
        grid=(n_tiles,),
        in_specs=srcs,
        out_specs=dsts,
        out_shape=[jax.ShapeDtypeStruct((n_l, n_i, f_pad // FF_TILE, D, FF_TILE), BF16)] * 2 + [
            jax.ShapeDtypeStruct((n_l, n_i, f_pad, D), BF16)],
        compiler_params=_cparams(1),
        name="cast_first",
    )(wg, wu, wd)


def _cast_into_body(w_ref, *rest):
    rest[-1][...] = w_ref[...].astype(BF16)


def _cast_into(w, out, n_cols, col0):
    n, K, F = w.shape
    ct = max(LANE, min(F, CAST_BLOCK_ELEMS // K))
    assert F % ct == 0 and col0 % ct == 0
    in_specs = [pl.BlockSpec((None, K, ct), lambda l, j: (l, 0, j))]
    args = [w]
    if out is not None:
        in_specs.append(pl.BlockSpec(memory_space=pl.ANY))
        args.append(out)
    return pl.pallas_call(
        _cast_into_body,
        grid=(n, F // ct),
        in_specs=in_specs,
        out_specs=pl.BlockSpec((None, K, ct), lambda l, j: (l, 0, col0 // ct + j)),
        out_shape=jax.ShapeDtypeStruct((n, K, n_cols), BF16),
        input_output_aliases={1: 0} if out is not None else {},
        compiler_params=_cparams(2),
        name="cast_into",
    )(*args)


def _cast_side_by_side(ws):
    n_cols = sum(w.shape[2] for w in ws)
    out, col0 = None, 0
    for w in ws:
        out = _cast_into(w, out, n_cols, col0)
        col0 += w.shape[2]
    return out


class CastJob(NamedTuple):
    targets: tuple
    d_ff: int
    n_tiles: int

    @property
    def span(self):
        return len(self.targets) * self.n_tiles

    def index(self, kind, t):
        tt = jnp.clip(t - kind * self.span, 0, self.span - 1)
        which = tt // self.n_tiles
        layer, idx = self.targets[0]
        for n, (l, i) in enumerate(self.targets[1:], 1):
            layer = jnp.where(which == n, l, layer)
            idx = jnp.where(which == n, i, idx)
        return layer, idx, tt % self.n_tiles


def _cast_step(job, t, src_refs, dst_refs):
    for kind in range(3):
        active = (t >= kind * job.span) & (t < (kind + 1) * job.span)
        tile = (t - kind * job.span) % job.n_tiles
        _cast_slab(src_refs[kind], dst_refs[kind], 0 if kind == 2 else 1, active, tile, job.n_tiles, job.d_ff)


def _ffn_body(*refs, mode, n_chunks, n_tail):
    refs = list(refs)
    x_refs = [refs.pop(0) for _ in range(2 if mode == "first" else 1)]
    gain_ref, wg_ref, wu_ref, wd_ref, fgain_ref = refs[:5]
    if mode == "last":
        y4_ref, yt_ref, h_ref, acc_ref = refs[5:]
    else:
        acc_ref, h_ref = refs[5:]
    i = pl.program_id(0)
    j = pl.program_id(1)

    def init(x):
        h_ref[...] = (_rms(x) * gain_ref[...]).astype(BF16)
        acc_ref[...] = x

    if mode == "first":
        pl.when((j == 0) & (i < n_chunks))(lambda: init(x_refs[0][...].reshape(acc_ref.shape)))
        pl.when((j == 0) & (i == n_chunks))(lambda: init(x_refs[1][...]))
    elif mode == "last":
        @pl.when(j == 0)
        def _():
            rows = lax.broadcasted_iota(jnp.int32, acc_ref.shape, 0)
            init(jnp.where(rows < jnp.where(i == n_chunks, n_tail, acc_ref.shape[0]), x_refs[0][...], 0.0))
    else:
        pl.when(j == 0)(lambda: init(x_refs[0][...]))

    h = h_ref[...]
    g = jnp.dot(h, wg_ref[...], preferred_element_type=F32)
    u = jnp.dot(h, wu_ref[...], preferred_element_type=F32)
    a = (0.5 * _silu(g) * u).astype(BF16)
    acc_ref[...] += jnp.dot(a, wd_ref[...], preferred_element_type=F32)

    if mode == "last":
        last_j = j == pl.num_programs(1) - 1

        @pl.when(last_j & (i < n_chunks))
        def _():
            y4_ref[...] = (_rms(acc_ref[...]) * fgain_ref[...]).reshape(y4_ref.shape)

        @pl.when(last_j & (i == n_chunks))
        def _():
            yt_ref[...] = _rms(acc_ref[...]) * fgain_ref[...]


def _ffn(x, gain, w_bf, layer, idx, tm, mode="mid", fgain=None, n_rows=None, home=None):
    wg, wu, wd = w_bf
    nj = wg.shape[2]
    D = wg.shape[3]
    w_specs = [
        pl.BlockSpec((None, None, None, D, FF_TILE), lambda i, j: (layer, idx, j, 0, 0)),
        pl.BlockSpec((None, None, None, D, FF_TILE), lambda i, j: (layer, idx, j, 0, 0)),
        pl.BlockSpec((None, None, FF_TILE, D), lambda i, j: (layer, idx, j, 0)),
    ]
    vec_spec = pl.BlockSpec((1, D), lambda i, j: (0, 0))
    if fgain is None:
        fgain = gain
    n_chunks = n_tail = 0
    if mode == "mid":
        R = x.shape[0]
        grid = (R // tm, nj)
        x_args, x_specs = [x], [pl.BlockSpec((tm, D), lambda i, j: (i, 0))]
        out_specs = pl.BlockSpec((tm, D), lambda i, j: (i, 0))
        out_shape = jax.ShapeDtypeStruct((R, D), F32)
        scratch = [pltpu.VMEM((tm, D), BF16)]
    else:
        if mode == "first":
            home, R = x[0].shape, n_rows
        else:
            R = x.shape[0]
        B, n_chunks, C, _ = home
        tm = B * C
        n_tail = R - n_chunks * tm
        assert 0 < n_tail <= tm
        grid = (n_chunks + 1, nj)
        home_spec = pl.BlockSpec((B, None, C, D), lambda i, j: (0, jnp.minimum(i, n_chunks - 1), 0, 0))
        tail_spec = pl.BlockSpec((tm, D), lambda i, j: (0, 0))
        rows_spec = pl.BlockSpec((tm, D), lambda i, j: (i, 0))
        if mode == "first":
            x_args, x_specs = list(x), [home_spec, tail_spec]
            out_specs, out_shape = rows_spec, jax.ShapeDtypeStruct((R, D), F32)
            scratch = [pltpu.VMEM((tm, D), BF16)]
        else:
            x_args, x_specs = [x], [rows_spec]
            out_specs = [home_spec, tail_spec]
            out_shape = [jax.ShapeDtypeStruct(home, F32), jax.ShapeDtypeStruct((tm, D), F32)]
            scratch = [pltpu.VMEM((tm, D), BF16), pltpu.VMEM((tm, D), F32)]
    return pl.pallas_call(
        functools.partial(_ffn_body, mode=mode, n_chunks=n_chunks, n_tail=n_tail),
        grid=grid,
        in_specs=x_specs + [vec_spec] + w_specs + [vec_spec],
        out_specs=out_specs,
        out_shape=out_shape,
        scratch_shapes=scratch,
        compiler_params=_cparams(2),
        name="ffn_" + mode,
    )(*x_args, gain.reshape(1, D), wg, wu, wd, fgain.reshape(1, D))


def _proj_body(x_ref, gain_ref, w_ref, o_ref, h_ref, *, n_silu):
    j = pl.program_id(1)

    @pl.when(j == 0)
    def _():
        h_ref[...] = (_rms(x_ref[...]) * gain_ref[...]).astype(BF16)

    y = jnp.dot(h_ref[...], w_ref[...], preferred_element_type=F32)
    if n_silu > 0:
        y = jnp.where(j < n_silu, _silu(y), y)
    o_ref[...] = y.astype(BF16)


def _proj(x, gain, w, layer, n_silu_cols, tm):
    R, D = x.shape
    N = w.shape[2]
    return pl.pallas_call(
        functools.partial(_proj_body, n_silu=n_silu_cols // PROJ_TILE),
        grid=(R // tm, N // PROJ_TILE),
        in_specs=[
            pl.BlockSpec((tm, D), lambda i, j: (i, 0)),
            pl.BlockSpec((1, D), lambda i, j: (0, 0)),
            pl.BlockSpec((None, D, PROJ_TILE), lambda i, j: (layer, 0, j)),
        ],
        out_specs=pl.BlockSpec((tm, PROJ_TILE), lambda i, j: (i, j)),
        out_shape=jax.ShapeDtypeStruct((R, N), BF16),
        scratch_shapes=[pltpu.VMEM((tm, D), BF16)],
        compiler_params=_cparams(2),
        name="proj",
    )(x, gain.reshape(1, D), w)


def _lower_bound(logits, layer):
    e = jnp.exp(logits - jnp.max(logits, axis=0, keepdims=True))
    probs = e / jnp.sum(e, axis=0, keepdims=True)
    cs = probs[0:1]
    for i in range(1, layer + 1):
        cs = cs + probs[i:i + 1]
    return cs - probs[0:1]


def _gate_proj_body(x_ref, gain_ref, w_ref, logit_ref, g_ref, k_ref, h_ref, *, layer):
    j = pl.program_id(1)

    @pl.when(j == 0)
    def _():
        h_ref[...] = (_rms(x_ref[...]) * gain_ref[...]).astype(BF16)

    z = jnp.dot(h_ref[...], w_ref[...], preferred_element_type=F32)
    lb = _lower_bound(logit_ref[...], layer)
    t = jnp.exp(-jnp.abs(z))
    r = 1.0 / (1.0 + t)
    tr = t * r
    pos = z >= 0.0
    g_ref[...] = jnp.log(jnp.maximum(lb, LB_FLOOR) + (1.0 - lb) * jnp.where(pos, r, tr))
    k_ref[...] = ((1.0 - lb) * jnp.where(pos, tr, r)).astype(BF16)


def _gate_proj(x, gain, w, logits, layer, tm):
    R, D = x.shape
    N = w.shape[2]
    spec = pl.BlockSpec((tm, GATE_TILE), lambda i, j: (i, j))
    return pl.pallas_call(
        functools.partial(_gate_proj_body, layer=layer),
        grid=(R // tm, N // GATE_TILE),
        in_specs=[
            pl.BlockSpec((tm, D), lambda i, j: (i, 0)),
            pl.BlockSpec((1, D), lambda i, j: (0, 0)),
            pl.BlockSpec((None, D, GATE_TILE), lambda i, j: (layer, 0, j)),
            pl.BlockSpec((logits.shape[0], GATE_TILE), lambda i, j: (0, j)),
        ],
        out_specs=[spec, spec],
        out_shape=[jax.ShapeDtypeStruct((R, N), F32), jax.ShapeDtypeStruct((R, N), BF16)],
        scratch_shapes=[pltpu.VMEM((tm, D), BF16)],
        compiler_params=_cparams(2),
        name="gate_proj",
    )(x, gain.reshape(1, D), w, logits)


def _outproj_body(a_ref, w_ref, x_ref, o_ref):
    o_ref[...] = x_ref[...] + jnp.dot(a_ref[...], w_ref[...], preferred_element_type=F32)


def _outproj(a, w, layer, x, tm):
    R, K = a.shape
    D = w.shape[2]
    return pl.pallas_call(
        _outproj_body,
        grid=(R // tm, D // OUT_TILE),
        in_specs=[
            pl.BlockSpec((tm, K), lambda i, j: (i, 0)),
            pl.BlockSpec((None, K, OUT_TILE), lambda i, j: (layer, 0, j)),
            pl.BlockSpec((tm, OUT_TILE), lambda i, j: (i, j)),
        ],
        out_specs=pl.BlockSpec((tm, OUT_TILE), lambda i, j: (i, j)),
        out_shape=jax.ShapeDtypeStruct((R, D), F32),
        compiler_params=_cparams(2),
        name="outproj",
    )(a, w, x)


def _col_bcast(x):
    bs = x.shape[0]
    return jnp.concatenate([x] * (LANE // bs), axis=0).T


def _sample_rows(s):
    grp = s - 1
    n = SAMPLES_PER_STEP
    rows = pl.ds(pl.multiple_of(grp * n, n), n)
    rows2 = pl.ds(pl.multiple_of((grp // 2) * 2 * n, 2 * n), 2 * n)
    return rows, rows2, (grp % 2) == 1


def _half(x2, second):
    n = SAMPLES_PER_STEP
    return jnp.where(second, x2[n:], x2[:n])


def _mixer_call(body, name, grid, in_specs, args, out_specs, out_shape, scratch_shapes, state, new_state,
                w_f32, w_bf, job, D):
    n_s = grid[1]
    any_spec = pl.BlockSpec(memory_space=pl.ANY)
    in_specs, args, out_specs, out_shape = list(in_specs), list(args), list(out_specs), list(out_shape)
    i_state = len(out_shape) - 1
    if job is None:
        w_bf = ()
    else:
        srcs, dsts = _cast_specs(D, lambda kind, h, s: job.index(kind, h * n_s + s))
        in_specs += srcs
        args += list(w_f32)
        out_specs += dsts
        out_shape += [jax.ShapeDtypeStruct(w.shape, w.dtype) for w in w_bf]
    aliases = {}
    if new_state is not None:
        aliases[len(args)] = i_state
        in_specs.append(any_spec)
        args.append(new_state)
    for n, w in enumerate(w_bf):
        aliases[len(args)] = i_state + 1 + n
        in_specs.append(any_spec)
        args.append(w)
    return pl.pallas_call(
        functools.partial(body, n_alias=len(aliases)),
        grid=grid, in_specs=in_specs, out_specs=out_specs, out_shape=out_shape,
        scratch_shapes=scratch_shapes, input_output_aliases=aliases,
        compiler_params=_cparams(2), name=name,
    )(*args)


def _hgrn_tables(C):
    n_lvl = int(math.log2(C))
    assert 1 << n_lvl == C
    n_fine = min(n_lvl, int(math.log2(SUBLANE)) - 1)
    r = np.arange(C)
    msum = np.zeros((n_fine + 1, C, C), np.float32)
    pair = np.zeros((n_lvl + 1, C, C), np.float32)
    u = r[None, :]
    for l in range(n_lvl):
        half = 1 << l
        blk = 2 * half
        m = r - r % blk + half - 1
        up = (r % blk) >= half
        if l < n_fine:
            msum[l] = np.where(up[:, None], (u > m[:, None]) & (u <= r[:, None]),
                               (u > r[:, None]) & (u <= m[:, None]))
        pair[l] = ((r[:, None] // blk) == (r[None, :] // blk)) & up[:, None] & ~up[None, :]
    pair[n_lvl] = np.eye(C)
    msum[n_fine] = u <= r[:, None]
    return jnp.asarray(msum.reshape(-1, C), BF16), jnp.asarray(pair, F32)


def _split3(g):
    hi = g.astype(BF16)
    r1 = g - hi.astype(F32)
    mid = r1.astype(BF16)
    lo = (r1 - mid.astype(F32)).astype(BF16)
    return hi, mid, lo


def _hgrn_chunks(qs, gs, ks, vs, gates, hgains, sts, msum, pair_ref, valid, side=()):
    side = list(side)
    n_lvl = pair_ref.shape[0] - 1
    per_phase = -(-len(side) // (n_lvl + 1))

    def run_side():
        for _ in range(min(per_phase, len(side))):
            side.pop(0)()

    n = range(len(qs))
    C = qs[0].shape[0]
    n_fine = msum.shape[0] // C - 1
    qf = [q.astype(F32) for q in qs]
    kf = [k.astype(F32) for k in ks]
    if valid is not None:
        gs = [g * valid for g in gs]
        kf = [k * valid for k in kf]
        ks = [k.astype(BF16) for k in kf]
    pieces = [_split3(g) for g in gs]
    e2 = [jnp.dot(msum, jnp.concatenate(p[:2], axis=1), preferred_element_type=F32) for p in pieces]
    e = [t[:, :LANE] + t[:, LANE:] for t in e2]
    c = [e[b][n_fine * C:] + jnp.dot(msum[n_fine * C:], pieces[b][2], preferred_element_type=F32)
         for b in n]
    x_fine = [jnp.exp(t[:n_fine * C]) for t in e]
    run_side()

    scores = [lax.dot_general(qs[b], ks[b], _NT, preferred_element_type=F32) * pair_ref[n_lvl] for b in n]
    for l in range(n_lvl):
        if l < n_fine:
            xl = [t[l * C:(l + 1) * C] for t in x_fine]
        else:
            half = 1 << l
            xl = []
            for t in c:
                c3 = t.reshape(C // (2 * half), 2 * half, LANE)
                cm = c3[:, half - 1:half, :]
                xl.append(jnp.exp(-jnp.abs(c3 - cm)).reshape(C, LANE))
        scores = [scores[b] + lax.dot_general((qf[b] * xl[b]).astype(BF16), (kf[b] * xl[b]).astype(BF16), _NT,
                                              preferred_element_type=F32) * pair_ref[l] for b in n]
        run_side()
    while side:
        side.pop(0)()
    xc = [jnp.exp(t) for t in c]
    xt = [jnp.exp(t[C - 1:C, :] - t) for t in c]
    o = [lax.dot_general((qf[b] * xc[b]).astype(BF16), sts[b].astype(BF16), _NT, preferred_element_type=F32)
         for b in n]
    o = [o[b] + jnp.dot(scores[b].astype(BF16), vs[b], preferred_element_type=F32) for b in n]
    st_new = [sts[b] * xc[b][C - 1:C, :] + lax.dot_general(vs[b], (kf[b] * xt[b]).astype(BF16), _TN,
                                                           preferred_element_type=F32) for b in n]
    og = [(_rms(o[b]) * hgains[b] * gates[b].astype(F32)).astype(BF16) for b in n]
    return og, st_new


def _hgrn_mixer_body(*refs, n_batch, n_chunks, n_heads, job, n_alias):
    (q_ref, gate_ref, v_ref, g_ref, k_ref, hgain_ref, msum_ref, pair_ref,
     qs_ref, gates_ref, vs_ref, gs_ref, ks_ref, s_ref) = refs[:14]
    n_w = 0 if job is None else 3
    w_src = refs[14:14 + n_w]
    n_in = 14 + n_w + n_alias
    og_ref, so_ref, ogs_ref, sn_ref = refs[n_in:n_in + 4]
    w_dst = refs[n_in + 4:n_in + 4 + n_w]
    st_ref, o_scr = refs[n_in + 4 + n_w:]
    C = CHUNK
    s = pl.program_id(1)
    if job is not None:
        _cast_step(job, pl.program_id(0) * pl.num_programs(1) + s, w_src, w_dst)

    def step(seqs, sts, valid, side=()):
        tiles = [(r, pl.ds(hh * LANE, LANE)) for r, hh in seqs]
        return _hgrn_chunks([q_ref[t] for t in tiles], [g_ref[t] for t in tiles], [k_ref[t] for t in tiles],
                            [v_ref[t] for t in tiles], [gate_ref[t] for t in tiles],
                            [hgain_ref[:, t[1]] for t in tiles], sts, msum_ref[...], pair_ref, valid, side)

    @pl.when(s == 0)
    def _():
        valid = (lax.broadcasted_iota(jnp.int32, (C, 1), 0) < N_META).astype(F32)
        seqs = [(pl.ds(C, C), hh) for hh in range(n_heads)]
        og, st_new = step(seqs, [jnp.zeros((LANE, HA_DK), F32)] * n_heads, valid)
        og_ref[...] = jnp.zeros_like(og_ref)
        for hh in range(n_heads):
            og_ref[pl.ds(C, C), pl.ds(hh * LANE, LANE)] = og[hh]
            for b in range(n_batch):
                st_ref[hh * n_batch + b] = st_new[hh]

    @pl.when(s > 0)
    def _():
        rows, rows2, second = _sample_rows(s)
        side = []
        for hh in range(n_heads):
            lanes = pl.ds(hh * LANE, LANE)
            q2 = qs_ref[rows2, lanes]
            v1 = _half(vs_ref[rows2, lanes].astype(F32), second)
            ft = _col_bcast(jnp.exp(gs_ref[rows, lanes]))
            kt = _col_bcast(_half(ks_ref[rows2, lanes].astype(F32), second))

            def update(b, hh=hh, q2=q2, v1=v1, ft=ft, kt=kt):
                sn = ft[:, b:b + 1] * s_ref[b, hh] + kt[:, b:b + 1] * v1[b:b + 1, :]
                sn_ref[b, hh] = sn
                res = jnp.dot(q2, sn.astype(BF16), preferred_element_type=F32)
                o_scr[hh, b:b + 1, :] = _half(res, second)[b:b + 1, :]

            def finish(hh=hh, lanes=lanes):
                gate1 = _half(gates_ref[rows2, lanes].astype(F32), second)
                ogs_ref[rows, lanes] = _rms(o_scr[hh]) * hgain_ref[:, lanes] * gate1

            side += [functools.partial(update, b) for b in range(SAMPLES_PER_STEP)] + [finish]

        seqs = [(pl.ds(b * C, C), hh) for hh in range(n_heads) for b in range(n_batch)]
        og, st_new = step(seqs, [st_ref[i] for i in range(len(seqs))], None, side)
        for i, (r, hh) in enumerate(seqs):
            og_ref[r, pl.ds(hh * LANE, LANE)] = og[i]
            st_ref[i] = st_new[i]

    @pl.when(s == n_chunks)
    def _():
        for hh in range(n_heads):
            for b in range(n_batch):
                so_ref[b, hh] = st_ref[hh * n_batch + b].T


def _hgrn_mixer(pb, g, k, hgain, state, new_state, layer, n_batch, seq, tables, sample_blk, w_f32, w_bf, job):
    R = pb.shape[0]
    D = pb.shape[1] // 3
    H = D // HA_DK
    C = CHUNK
    B = n_batch
    hp = HGRN_HEADS_PER_STEP
    n_chunks = seq // C
    n_sample = state.shape[1]
    msum, pair = tables
    grid = (H // hp, 1 + n_chunks)
    assert B >= 2 and R >= n_chunks * B * C + 2 * C and H % hp == 0
    assert n_sample == SAMPLES_PER_STEP * n_chunks and (job is None or 3 * job.span <= grid[0] * grid[1])

    def spec(col0):
        return pl.BlockSpec((B * C, hp * LANE),
                            lambda h, s: (jnp.where(s == 0, n_chunks, s - 1), col0 // hp + h))

    def sspec(col0):
        return pl.BlockSpec((LANE, hp * LANE), lambda h, s: (sample_blk, col0 // hp + h))

    state_spec = pl.BlockSpec((None, SAMPLES_PER_STEP, hp, HA_DK, LANE),
                              lambda h, s: (layer, jnp.maximum(s - 1, 0), h, 0, 0))
    out = _mixer_call(
        functools.partial(_hgrn_mixer_body, n_batch=B, n_chunks=n_chunks, n_heads=hp, job=job),
        "hgrn_mixer", grid,
        in_specs=[spec(0), spec(H), spec(2 * H), spec(0), spec(0),
                  pl.BlockSpec((1, hp * LANE), lambda h, s: (0, h)),
                  pl.BlockSpec(msum.shape, lambda h, s: (0, 0)),
                  pl.BlockSpec(pair.shape, lambda h, s: (0, 0, 0)),
                  sspec(0), sspec(H), sspec(2 * H), sspec(0), sspec(0), state_spec],
        args=[pb, pb, pb, g, k, hgain.reshape(1, D), msum, pair, pb, pb, pb, g, k, state],
        out_specs=[spec(0), pl.BlockSpec((B, hp, HA_DK, LANE), lambda h, s: (0, h, 0, 0)),
                   pl.BlockSpec((LANE, hp * LANE), lambda h, s: (0, h)), state_spec],
        out_shape=[jax.ShapeDtypeStruct((R, D), BF16), jax.ShapeDtypeStruct((B, H, HA_DK, LANE), F32),
                   jax.ShapeDtypeStruct((LANE, D), F32), jax.ShapeDtypeStruct(state.shape, F32)],
        scratch_shapes=[pltpu.VMEM((hp * B, LANE, HA_DK), F32), pltpu.VMEM((hp, SAMPLES_PER_STEP, LANE), F32)],
        state=state, new_state=new_state, w_f32=w_f32, w_bf=w_bf, job=job, D=D)
    return out[0], out[1], out[2], out[3], (tuple(out[4:]) if job is not None else w_bf)


def _rotary(x, cos, sin):
    half = x.shape[-1] // 2
    x1, x2 = x[:, :half], x[:, half:]
    return jnp.concatenate([x1 * cos - x2 * sin, x1 * sin + x2 * cos], axis=-1)


def _ret_chunks(qs, ks, vs, gates, cos, sin, lg, sts, n_valid):
    n = range(len(qs))
    C, dk = qs[0].shape
    t_col = lax.broadcasted_iota(jnp.int32, (C, 1), 0)
    t_row = lax.broadcasted_iota(jnp.int32, (1, C), 1)
    n_col = jnp.minimum(t_col + 1, n_valid).astype(F32)
    n_row = jnp.minimum(t_row + 1, n_valid).astype(F32)
    n_last = float(min(C, n_valid))
    decay = jnp.where(t_col >= t_row, jnp.exp(lg * (n_col - n_row)), 0.0)
    q_scale = jnp.exp(lg * n_col)
    k_scale = jnp.exp(lg * (n_last - n_col))
    k_norm = dk ** -0.5
    if n_valid < C:
        k_norm = k_norm * (t_col < n_valid).astype(F32)
    q = [_rotary(t.astype(F32), cos, sin) for t in qs]
    k = [_rotary(t.astype(F32), cos, sin) * k_norm for t in ks]
    scores = [lax.dot_general(q[b].astype(BF16), k[b].astype(BF16), _NT, preferred_element_type=F32) * decay
              for b in n]
    o = [jnp.dot((q[b] * q_scale).astype(BF16), sts[b].astype(BF16), preferred_element_type=F32) for b in n]
    o = [o[b] + jnp.dot(scores[b].astype(BF16), vs[b], preferred_element_type=F32) for b in n]
    st_new = [jnp.exp(lg * n_last) * sts[b] + lax.dot_general((k[b] * k_scale).astype(BF16), vs[b], _TN,
                                                              preferred_element_type=F32) for b in n]
    og = [(_rms(o[b]) * gates[b].astype(F32)).astype(BF16) for b in n]
    return og, st_new


def _ret_mixer_body(*refs, n_batch, n_chunks, job, n_alias):
    (lg_ref, q_ref, k_ref, v_ref, gate_ref, cos_ref, sin_ref,
     qs_ref, ks_ref, vs_ref, gates_ref, coss_ref, sins_ref, s_ref) = refs[:14]
    n_w = 0 if job is None else 3
    w_src = refs[14:14 + n_w]
    n_in = 14 + n_w + n_alias
    og_ref, so_ref, ogs_ref, sn_ref = refs[n_in:n_in + 4]
    w_dst = refs[n_in + 4:n_in + 4 + n_w]
    st_ref, o_scr = refs[n_in + 4 + n_w:]
    C = CHUNK
    lg = lg_ref[pl.program_id(0)]
    s = pl.program_id(1)
    if job is not None:
        _cast_step(job, pl.program_id(0) * pl.num_programs(1) + s, w_src, w_dst)

    def step(rows, sts, n_valid):
        return _ret_chunks([q_ref[r, :] for r in rows], [k_ref[r, :] for r in rows], [v_ref[r, :] for r in rows],
                           [gate_ref[r, :] for r in rows], cos_ref[...], sin_ref[...], lg, sts, n_valid)

    @pl.when(s == 0)
    def _():
        og, st_new = step([pl.ds(C, C)], [jnp.zeros(st_ref.shape[1:], F32)], N_META)
        og_ref[...] = jnp.zeros_like(og_ref)
        og_ref[pl.ds(C, C), :] = og[0]
        for b in range(n_batch):
            st_ref[b] = st_new[0]

    @pl.when(s > 0)
    def _():
        rows, rows2, second = _sample_rows(s)
        dk = qs_ref.shape[-1]
        gamma = jnp.exp(lg)
        cos, sin = coss_ref[...], sins_ref[...]
        q = _rotary(_half(qs_ref[rows2, :].astype(F32), second), cos, sin)
        k = _rotary(_half(ks_ref[rows2, :].astype(F32), second), cos, sin) * (dk ** -0.5)
        v = _half(vs_ref[rows2, :].astype(F32), second)
        qt = jnp.concatenate([_col_bcast(q[:, :LANE]), _col_bcast(q[:, LANE:])], axis=0)
        kt = jnp.concatenate([_col_bcast(k[:, :LANE]), _col_bcast(k[:, LANE:])], axis=0)
        for b in range(SAMPLES_PER_STEP):
            sn = gamma * s_ref[b] + kt[:, b:b + 1] * v[b:b + 1, :]
            sn_ref[b] = sn
            o_scr[b:b + 1, :] = jnp.sum(qt[:, b:b + 1] * sn, axis=0, keepdims=True)
        ogs_ref[rows, :] = _rms(o_scr[...]) * _half(gates_ref[rows2, :].astype(F32), second)

        og, st_new = step([pl.ds(b * C, C) for b in range(n_batch)], [st_ref[b] for b in range(n_batch)], C)
        for b in range(n_batch):
            og_ref[pl.ds(b * C, C), :] = og[b]
            st_ref[b] = st_new[b]

    @pl.when(s == n_chunks)
    def _():
        so_ref[...] = st_ref[...]


def _ret_mixer(p, log_gamma, cos_tab, sin_tab, cos_s, sin_s, state, new_state, layer, n_batch, seq,
               sample_blk, w_f32, w_bf, job):
    R = p.shape[0]
    D = p.shape[1] // 6
    H = HB_HEADS
    dk = D // H
    dv = 2 * D // H
    C = CHUNK
    B = n_batch
    n_chunks = seq // C
    n_sample = state.shape[1]
    qk_off = 2 * D // dk
    v_off = 4 * D // dv
    grid = (H, 1 + n_chunks)
    assert B >= 2 and R >= n_chunks * B * C + 2 * C and dk == 2 * LANE
    assert n_sample == SAMPLES_PER_STEP * n_chunks and (job is None or 3 * job.span <= grid[0] * grid[1])

    def spec(width, col0):
        return pl.BlockSpec((B * C, width), lambda h, s: (jnp.where(s == 0, n_chunks, s - 1), col0 + h))

    def sspec(width, col0):
        return pl.BlockSpec((LANE, width), lambda h, s: (sample_blk, col0 + h))

    tab_spec = pl.BlockSpec((C, dk // 2), lambda h, s: (s, 0))
    pos_spec = pl.BlockSpec((1, dk // 2), lambda h, s: (0, 0))
    state_spec = pl.BlockSpec((None, SAMPLES_PER_STEP, None, dk, dv),
                              lambda h, s: (layer, jnp.maximum(s - 1, 0), h, 0, 0))
    out = _mixer_call(
        functools.partial(_ret_mixer_body, n_batch=B, n_chunks=n_chunks, job=job),
        "ret_mixer", grid,
        in_specs=[pl.BlockSpec(memory_space=pltpu.SMEM),
                  spec(dk, qk_off), spec(dk, qk_off + H), spec(dv, v_off), spec(dv, 0), tab_spec, tab_spec,
                  sspec(dk, qk_off), sspec(dk, qk_off + H), sspec(dv, v_off), sspec(dv, 0), pos_spec, pos_spec,
                  state_spec],
        args=[log_gamma, p, p, p, p, cos_tab, sin_tab, p, p, p, p, cos_s, sin_s, state],
        out_specs=[spec(dv, 0), pl.BlockSpec((B, None, dk, dv), lambda h, s: (0, h, 0, 0)),
                   pl.BlockSpec((LANE, dv), lambda h, s: (0, h)), state_spec],
        out_shape=[jax.ShapeDtypeStruct((R, 2 * D), BF16), jax.ShapeDtypeStruct((B, H, dk, dv), F32),
                   jax.ShapeDtypeStruct((LANE, 2 * D), F32), jax.ShapeDtypeStruct(state.shape, F32)],
        scratch_shapes=[pltpu.VMEM((B, dk, dv), F32), pltpu.VMEM((SAMPLES_PER_STEP, dv), F32)],
        state=state, new_state=new_state, w_f32=w_f32, w_bf=w_bf, job=job, D=D)
    return out[0], out[1], out[2], out[3], (tuple(out[4:]) if job is not None else w_bf)


def _rope_tables(pos, half):
    inv = ROPE_BASE ** (-jnp.linspace(0.0, 1.0, half, dtype=F32))
    ang = pos.astype(F32)[:, None] * inv[None, :]
    return jnp.cos(ang), jnp.sin(ang)


def kernel(x_prompt, x_sample, state_hgrn, state_ret, meta_tokens, norm_ffn, ffn_w_gate, ffn_w_up, ffn_w_down, norm_mix, hg_wq, hg_wf, hg_wi, hg_wg, hg_wo, hg_norm, hg_lb_logits, rt_wq, rt_wk, rt_wv, rt_wg, rt_wo, norm_final):
    B, L, D = x_prompt.shape
    NS = x_sample.shape[0]
    depth = norm_mix.shape[0]
    n_meta = meta_tokens.shape[0]
    C = CHUNK
    assert n_meta == N_META and x_sample.shape[1] == 1 and L % C == 0 and NS <= LANE
    n_chunks = L // C

    tp = B * L
    sp = LANE
    rows0 = tp + sp + C
    tm = next(t for t in ROW_TILES if _round_up(rows0, t) - rows0 < LANE) if rows0 >= ROW_TILES[0] else rows0
    R = _round_up(rows0, tm)
    home = (B, n_chunks, C, D)
    x_tail = jnp.concatenate([
        x_sample.reshape(NS, D), jnp.zeros((sp - NS, D), F32),
        meta_tokens.astype(F32), jnp.zeros((B * C - sp - n_meta, D), F32)], axis=0)
    sample_blk = tp // LANE

    w_f32 = (ffn_w_gate, ffn_w_up, ffn_w_down)
    d_ff = ffn_w_gate.shape[-1]
    n_cast_tiles = _round_up(d_ff, FF_TILE) // CAST_TILE
    w_bf = _cast_first(w_f32, 0, 0)

    hg_in = _cast_side_by_side([hg_wq, hg_wg, hg_wi])
    hg_f = _cast_side_by_side([hg_wf])
    hg_out = _cast_side_by_side([hg_wo])
    rt_in = _cast_side_by_side([rt_wg, rt_wq, rt_wk, rt_wv])
    rt_out = _cast_side_by_side([rt_wo])

    half = D // HB_HEADS // 2
    cos_p, sin_p = _rope_tables(jnp.arange(n_meta + L, dtype=jnp.int32), half)
    cos_s, sin_s = _rope_tables(PAST_LEN + jnp.arange(1, dtype=jnp.int32), half)

    def chunk_table(t):
        meta = jnp.concatenate([t[:n_meta], jnp.zeros((C - n_meta, half), F32)], axis=0)
        return jnp.concatenate([meta, t[n_meta:]], axis=0)

    cos_tab, sin_tab = chunk_table(cos_p), chunk_table(sin_p)
    log_gamma = jnp.log1p(-jnp.exp2(-5.0 - jnp.arange(HB_HEADS, dtype=F32)))
    tables = _hgrn_tables(C)

    new_hgrn_p, new_ret_p = [], []
    new_hgrn_s = new_ret_s = None
    for i in range(depth):
        if i == 0:
            x = _ffn((x_prompt.reshape(home), x_tail), norm_ffn[i, 0], w_bf, i, 0, tm, mode="first", n_rows=R)
        else:
            x = _ffn(x, norm_ffn[i, 0], w_bf, i, 0, tm)
        j = i // 2
        job = None
        if i % 2 == 0:
            later = [(l, n) for l in range(i, depth) for n in (0, 1)][1:]
            job = CastJob(tuple(later[:4] if i + 2 < depth else later), d_ff, n_cast_tiles)
        if i % 2 == 0:
            pb = _proj(x, norm_mix[i], hg_in, j, 2 * D, tm)
            g, k = _gate_proj(x, norm_mix[i], hg_f, hg_lb_logits, j, tm)
            og, s_p, og_s, new_hgrn_s, w_bf = _hgrn_mixer(
                pb, g, k, hg_norm[j], state_hgrn, new_hgrn_s, j, B, L, tables, sample_blk, w_f32, w_bf, job)
            new_hgrn_p.append(s_p)
            w_out = hg_out
        else:
            pb = _proj(x, norm_mix[i], rt_in, j, 2 * D, tm)
            og, s_p, og_s, new_ret_s, w_bf = _ret_mixer(
                pb, log_gamma, cos_tab, sin_tab, cos_s, sin_s, state_ret, new_ret_s, j, B, L,
                sample_blk, w_f32, w_bf, job)
            new_ret_p.append(s_p)
            w_out = rt_out
        og = lax.dynamic_update_slice(og, og_s[:NS].astype(BF16), (tp, 0))
        x = _outproj(og, w_out, j, x, tm)
        if i < depth - 1:
            x = _ffn(x, norm_ffn[i, 1], w_bf, i, 1, tm)
    y_home, y_tail = _ffn(x, norm_ffn[depth - 1, 1], w_bf, depth - 1, 1, tm, mode="last", fgain=norm_final,
                          home=home)
    y_prompt = y_home.reshape(B, L, D)
    y_sample = y_tail[:NS].reshape(NS, 1, D)
    return (y_prompt, y_sample, jnp.stack(new_hgrn_p), new_hgrn_s, jnp.stack(new_ret_p), new_ret_s)
```

```python
import functools
import math
from typing import NamedTuple

import numpy as np
import jax
import jax.numpy as jnp
from jax import lax
from jax.experimental import pallas as pl
from jax.experimental.pallas import tpu as pltpu

F32 = jnp.float32
BF16 = jnp.bfloat16

LOG2_E = math.log2(math.e)
EPS = 1e-6
LB_FLOOR = 1e-30
ROPE_BASE = 10000.0
N_META = 16
PAST_LEN = 16384
HA_DK = 128
HB_HEADS = 8

LANE = 128
SUBLANE = 8
CHUNK = 128
HGRN_HEADS_PER_STEP = 2
SAMPLES_PER_STEP = SUBLANE
FF_TILE = 512
CAST_TILE = 512
CAST_BLOCK_ELEMS = 1 << 20
PROJ_TILE = 2048
GATE_TILE = 1024
OUT_TILE = 1024
OUT_WEIGHT_VMEM = 16 * 1024 * 1024
ROW_TILES = (768, 512, 256, 128)
VMEM_LIMIT = 56 * 1024 * 1024

_NT = (((1,), (1,)), ((), ()))
_TN = (((0,), (0,)), ((), ()))


def _round_up(a, m):
    return (a + m - 1) // m * m


def _cparams(n_axes):
    return pltpu.CompilerParams(dimension_semantics=("arbitrary",) * n_axes,
                                vmem_limit_bytes=VMEM_LIMIT)


def _rms(x):
    return x * lax.rsqrt(jnp.mean(x * x, axis=-1, keepdims=True) + EPS)


def _neg_abs(x):
    bits = lax.bitcast_convert_type(x, jnp.uint32) | jnp.uint32(0x80000000)
    return lax.bitcast_convert_type(bits, F32)


def _sigmoid(x):
    return 1.0 / (1.0 + jnp.exp(-x))


def _silu(x):
    return x * _sigmoid(x)


def _cast_slab(src_ref, dst_ref, axis, active, tile, n_tiles, d_ff):
    last = tile == n_tiles - 1

    @pl.when(active & jnp.logical_not(last))
    def _():
        dst_ref[...] = src_ref[...].astype(BF16)

    @pl.when(active & last)
    def _():
        w = src_ref[...]
        idx = lax.broadcasted_iota(jnp.int32, w.shape, axis)
        dst_ref[...] = jnp.where(idx < d_ff - (n_tiles - 1) * CAST_TILE, w, 0.0).astype(BF16)


def _cast_first_body(wg_ref, wu_ref, wd_ref, og_ref, ou_ref, od_ref, *, d_ff):
    j = pl.program_id(0)
    n = pl.num_programs(0)
    every_step = j >= 0
    _cast_slab(wg_ref, og_ref, 1, every_step, j, n, d_ff)
    _cast_slab(wu_ref, ou_ref, 1, every_step, j, n, d_ff)
    _cast_slab(wd_ref, od_ref, 0, every_step, j, n, d_ff)


def _cast_specs(D, index):
    per_ff = FF_TILE // CAST_TILE

    def src(kind):
        def f(*ids):
            l, i, t = index(kind, *ids)
            return (l, i, t, 0) if kind == 2 else (l, i, 0, t)
        shape = (None, None, CAST_TILE, D) if kind == 2 else (None, None, D, CAST_TILE)
        return pl.BlockSpec(shape, f)

    def dst(kind):
        def f(*ids):
            l, i, t = index(kind, *ids)
            return (l, i, t, 0) if kind == 2 else (l, i, t // per_ff, 0, t % per_ff)
        shape = (None, None, CAST_TILE, D) if kind == 2 else (None, None, None, D, CAST_TILE)
        return pl.BlockSpec(shape, f)

    return [src(k) for k in range(3)], [dst(k) for k in range(3)]


def _cast_first(w_f32, layer, idx):
    wg, wu, wd = w_f32
    n_l, n_i, D, F = wg.shape
    f_pad = _round_up(F, FF_TILE)
    n_tiles = f_pad // CAST_TILE
    assert 0 < F - (n_tiles - 1) * CAST_TILE <= CAST_TILE
    srcs, dsts = _cast_specs(D, lambda kind, j: (layer, idx, j))
    return pl.pallas_call(
        functools.partial(_cast_first_body, d_ff=F),
        grid=(n_tiles,),
        in_specs=srcs,
        out_specs=dsts,
        out_shape=[jax.ShapeDtypeStruct((n_l, n_i, f_pad // FF_TILE, D, FF_TILE), BF16)] * 2 + [
            jax.ShapeDtypeStruct((n_l, n_i, f_pad, D), BF16)],
        compiler_params=_cparams(1),
        name="cast_first",
    )(wg, wu, wd)


def _cast_into_body(w_ref, *rest):
    rest[-1][...] = w_ref[...].astype(BF16)


def _cast_into(w, out, n_cols, col0):
    n, K, F = w.shape
    ct = max(LANE, min(F, CAST_BLOCK_ELEMS // K))
    assert F % ct == 0 and col0 % ct == 0
    in_specs = [pl.BlockSpec((None, K, ct), lambda l, j: (l, 0, j))]
    args = [w]
    if out is not None:
        in_specs.append(pl.BlockSpec(memory_space=pl.ANY))
        args.append(out)
    return pl.pallas_call(
        _cast_into_body,
        grid=(n, F // ct),
        in_specs=in_specs,
        out_specs=pl.BlockSpec((None, K, ct), lambda l, j: (l, 0, col0 // ct + j)),
        out_shape=jax.ShapeDtypeStruct((n, K, n_cols), BF16),
        input_output_aliases={1: 0} if out is not None else {},
        compiler_params=_cparams(2),
        name="cast_into",
    )(*args)


def _cast_side_by_side(ws):
    n_cols = sum(w.shape[2] for w in ws)
    out, col0 = None, 0
    for w in ws:
        out = _cast_into(w, out, n_cols, col0)
        col0 += w.shape[2]
    return out


class CastJob(NamedTuple):
    targets: tuple
    d_ff: int
    n_tiles: int

    @property
    def span(self):
        return len(self.targets) * self.n_tiles

    def index(self, kind, t):
        tt = jnp.clip(t - kind * self.span, 0, self.span - 1)
        which = tt // self.n_tiles
        layer, idx = self.targets[0]
        for n, (l, i) in enumerate(self.targets[1:], 1):
            layer = jnp.where(which == n, l, layer)
            idx = jnp.where(which == n, i, idx)
        return layer, idx, tt % self.n_tiles


def _cast_step(job, t, src_refs, dst_refs):
    for kind in range(3):
        active = (t >= kind * job.span) & (t < (kind + 1) * job.span)
        tile = (t - kind * job.span) % job.n_tiles
        _cast_slab(src_refs[kind], dst_refs[kind], 0 if kind == 2 else 1, active, tile, job.n_tiles, job.d_ff)


def _ffn_body(*refs, mode, n_chunks, n_tail):
    refs = list(refs)
    x_refs = [refs.pop(0) for _ in range(2 if mode == "first" else 1)]
    gain_ref, wg_ref, wu_ref, wd_ref, fgain_ref = refs[:5]
    if mode == "last":
        y4_ref, yt_ref, h_ref, acc_ref = refs[5:]
    else:
        acc_ref, h_ref = refs[5:]
    i = pl.program_id(0)
    j = pl.program_id(1)

    def init(x):
        h_ref[...] = (_rms(x) * gain_ref[...]).astype(BF16)
        acc_ref[...] = x

    if mode == "first":
        pl.when((j == 0) & (i < n_chunks))(lambda: init(x_refs[0][...].reshape(acc_ref.shape)))
        pl.when((j == 0) & (i == n_chunks))(lambda: init(x_refs[1][...]))
    elif mode == "last":
        @pl.when(j == 0)
        def _():
            rows = lax.broadcasted_iota(jnp.int32, acc_ref.shape, 0)
            init(jnp.where(rows < jnp.where(i == n_chunks, n_tail, acc_ref.shape[0]), x_refs[0][...], 0.0))
    else:
        pl.when(j == 0)(lambda: init(x_refs[0][...]))

    h = h_ref[...]
    g = jnp.dot(h, wg_ref[...], preferred_element_type=F32)
    u = jnp.dot(h, wu_ref[...], preferred_element_type=F32)
    a = (0.5 * _silu(g) * u).astype(BF16)
    acc_ref[...] += jnp.dot(a, wd_ref[...], preferred_element_type=F32)

    if mode == "last":
        last_j = j == pl.num_programs(1) - 1

        @pl.when(last_j & (i < n_chunks))
        def _():
            y4_ref[...] = (_rms(acc_ref[...]) * fgain_ref[...]).reshape(y4_ref.shape)

        @pl.when(last_j & (i == n_chunks))
        def _():
            yt_ref[...] = _rms(acc_ref[...]) * fgain_ref[...]


def _ffn(x, gain, w_bf, layer, idx, tm, mode="mid", fgain=None, n_rows=None, home=None):
    wg, wu, wd = w_bf
    nj = wg.shape[2]
    D = wg.shape[3]
    w_specs = [
        pl.BlockSpec((None, None, None, D, FF_TILE), lambda i, j: (layer, idx, j, 0, 0)),
        pl.BlockSpec((None, None, None, D, FF_TILE), lambda i, j: (layer, idx, j, 0, 0)),
        pl.BlockSpec((None, None, FF_TILE, D), lambda i, j: (layer, idx, j, 0)),
    ]
    vec_spec = pl.BlockSpec((1, D), lambda i, j: (0, 0))
    if fgain is None:
        fgain = gain
    n_chunks = n_tail = 0
    if mode == "mid":
        R = x.shape[0]
        grid = (R // tm, nj)
        x_args, x_specs = [x], [pl.BlockSpec((tm, D), lambda i, j: (i, 0))]
        out_specs = pl.BlockSpec((tm, D), lambda i, j: (i, 0))
        out_shape = jax.ShapeDtypeStruct((R, D), F32)
        scratch = [pltpu.VMEM((tm, D), BF16)]
    else:
        if mode == "first":
            home, R = x[0].shape, n_rows
        else:
            R = x.shape[0]
        B, n_chunks, C, _ = home
        tm = B * C
        n_tail = R - n_chunks * tm
        assert 0 < n_tail <= tm
        grid = (n_chunks + 1, nj)
        home_spec = pl.BlockSpec((B, None, C, D), lambda i, j: (0, jnp.minimum(i, n_chunks - 1), 0, 0))
        tail_spec = pl.BlockSpec((tm, D), lambda i, j: (0, 0))
        rows_spec = pl.BlockSpec((tm, D), lambda i, j: (i, 0))
        if mode == "first":
            x_args, x_specs = list(x), [home_spec, tail_spec]
            out_specs, out_shape = rows_spec, jax.ShapeDtypeStruct((R, D), F32)
            scratch = [pltpu.VMEM((tm, D), BF16)]
        else:
            x_args, x_specs = [x], [rows_spec]
            out_specs = [home_spec, tail_spec]
            out_shape = [jax.ShapeDtypeStruct(home, F32), jax.ShapeDtypeStruct((tm, D), F32)]
            scratch = [pltpu.VMEM((tm, D), BF16), pltpu.VMEM((tm, D), F32)]
    return pl.pallas_call(
        functools.partial(_ffn_body, mode=mode, n_chunks=n_chunks, n_tail=n_tail),
        grid=grid,
        in_specs=x_specs + [vec_spec] + w_specs + [vec_spec],
        out_specs=out_specs,
        out_shape=out_shape,
        scratch_shapes=scratch,
        compiler_params=_cparams(2),
        name="ffn_" + mode,
    )(*x_args, gain.reshape(1, D), wg, wu, wd, fgain.reshape(1, D))


def _proj_body(x_ref, gain_ref, w_ref, o_ref, h_ref, *, n_silu):
    j = pl.program_id(1)

    @pl.when(j == 0)
    def _():
        h_ref[...] = (_rms(x_ref[...]) * gain_ref[...]).astype(BF16)

    y = jnp.dot(h_ref[...], w_ref[...], preferred_element_type=F32)
    if n_silu > 0:
        y = jnp.where(j < n_silu, _silu(y), y)
    o_ref[...] = y.astype(BF16)


def _proj(x, gain, w, layer, n_silu_cols, tm):
    R, D = x.shape
    N = w.shape[2]
    return pl.pallas_call(
        functools.partial(_proj_body, n_silu=n_silu_cols // PROJ_TILE),
        grid=(R // tm, N // PROJ_TILE),
        in_specs=[
            pl.BlockSpec((tm, D), lambda i, j: (i, 0)),
            pl.BlockSpec((1, D), lambda i, j: (0, 0)),
            pl.BlockSpec((None, D, PROJ_TILE), lambda i, j: (layer, 0, j)),
        ],
        out_specs=pl.BlockSpec((tm, PROJ_TILE), lambda i, j: (i, j)),
        out_shape=jax.ShapeDtypeStruct((R, N), BF16),
        scratch_shapes=[pltpu.VMEM((tm, D), BF16)],
        compiler_params=_cparams(2),
        name="proj",
    )(x, gain.reshape(1, D), w)


def _lower_bound(logits, layer):
    e = jnp.exp(logits - jnp.max(logits, axis=0, keepdims=True))
    probs = e / jnp.sum(e, axis=0, keepdims=True)
    cs = probs[0:1]
    for i in range(1, layer + 1):
        cs = cs + probs[i:i + 1]
    return cs - probs[0:1]


def _gate_proj_body(x_ref, gain_ref, w_ref, logit_ref, g_ref, k_ref, h_ref, *, layer):
    j = pl.program_id(1)

    @pl.when(j == 0)
    def _():
        h_ref[...] = (_rms(x_ref[...]) * gain_ref[...]).astype(BF16)

    z = jnp.dot(h_ref[...], w_ref[...], preferred_element_type=F32)
    lb = _lower_bound(logit_ref[...], layer)
    t = jnp.exp(-jnp.abs(z))
    r = 1.0 / (1.0 + t)
    tr = t * r
    pos = z >= 0.0
    g_ref[...] = jnp.log(jnp.maximum(lb, LB_FLOOR) + (1.0 - lb) * jnp.where(pos, r, tr)) * LOG2_E
    k_ref[...] = ((1.0 - lb) * jnp.where(pos, tr, r)).astype(BF16)


def _gate_proj(x, gain, w, logits, layer, tm):
    R, D = x.shape
    N = w.shape[2]
    spec = pl.BlockSpec((tm, GATE_TILE), lambda i, j: (i, j))
    return pl.pallas_call(
        functools.partial(_gate_proj_body, layer=layer),
        grid=(R // tm, N // GATE_TILE),
        in_specs=[
            pl.BlockSpec((tm, D), lambda i, j: (i, 0)),
            pl.BlockSpec((1, D), lambda i, j: (0, 0)),
            pl.BlockSpec((None, D, GATE_TILE), lambda i, j: (layer, 0, j)),
            pl.BlockSpec((logits.shape[0], GATE_TILE), lambda i, j: (0, j)),
        ],
        out_specs=[spec, spec],
        out_shape=[jax.ShapeDtypeStruct((R, N), F32), jax.ShapeDtypeStruct((R, N), BF16)],
        scratch_shapes=[pltpu.VMEM((tm, D), BF16)],
        compiler_params=_cparams(2),
        name="gate_proj",
    )(x, gain.reshape(1, D), w, logits)


def _outproj_body(a_ref, w_ref, x_ref, o_ref):
    o_ref[...] = x_ref[...] + jnp.dot(a_ref[...], w_ref[...], preferred_element_type=F32)


def _outproj(a, w, layer, x, tm):
    R, K = a.shape
    D = w.shape[2]
    tn = D if 2 * K * D * w.dtype.itemsize <= OUT_WEIGHT_VMEM else OUT_TILE
    return pl.pallas_call(
        _outproj_body,
        grid=(R // tm, D // tn),
        in_specs=[
            pl.BlockSpec((tm, K), lambda i, j: (i, 0)),
            pl.BlockSpec((None, K, tn), lambda i, j: (layer, 0, j)),
            pl.BlockSpec((tm, tn), lambda i, j: (i, j)),
        ],
        out_specs=pl.BlockSpec((tm, tn), lambda i, j: (i, j)),
        out_shape=jax.ShapeDtypeStruct((R, D), F32),
        compiler_params=_cparams(2),
        name="outproj",
    )(a, w, x)


def _col_bcast(x):
    bs = x.shape[0]
    return jnp.concatenate([x] * (LANE // bs), axis=0).T


def _sample_rows(s):
    grp = s - 1
    n = SAMPLES_PER_STEP
    rows = pl.ds(pl.multiple_of(grp * n, n), n)
    rows2 = pl.ds(pl.multiple_of((grp // 2) * 2 * n, 2 * n), 2 * n)
    return rows, rows2, (grp % 2) == 1


def _half(x2, second):
    n = SAMPLES_PER_STEP
    return jnp.where(second, x2[n:], x2[:n])


def _mixer_call(body, name, grid, in_specs, args, out_specs, out_shape, scratch_shapes, state, new_state,
                w_f32, w_bf, job, D):
    n_s = grid[1]
    any_spec = pl.BlockSpec(memory_space=pl.ANY)
    in_specs, args, out_specs, out_shape = list(in_specs), list(args), list(out_specs), list(out_shape)
    i_state = len(out_shape) - 1
    if job is None:
        w_bf = ()
    else:
        srcs, dsts = _cast_specs(D, lambda kind, h, s: job.index(kind, h * n_s + s))
        in_specs += srcs
        args += list(w_f32)
        out_specs += dsts
        out_shape += [jax.ShapeDtypeStruct(w.shape, w.dtype) for w in w_bf]
    aliases = {}
    if new_state is not None:
        aliases[len(args)] = i_state
        in_specs.append(any_spec)
        args.append(new_state)
    for n, w in enumerate(w_bf):
        aliases[len(args)] = i_state + 1 + n
        in_specs.append(any_spec)
        args.append(w)
    return pl.pallas_call(
        functools.partial(body, n_alias=len(aliases)),
        grid=grid, in_specs=in_specs, out_specs=out_specs, out_shape=out_shape,
        scratch_shapes=scratch_shapes, input_output_aliases=aliases,
        compiler_params=_cparams(2), name=name,
    )(*args)


def _hgrn_tables(C):
    n_lvl = int(math.log2(C))
    assert 1 << n_lvl == C
    n_fine = min(n_lvl, int(math.log2(SUBLANE)) - 1)
    r = np.arange(C)
    msum = np.zeros((n_fine + 1, C, C), np.float32)
    pair = np.zeros((n_lvl + 1, C, C), np.float32)
    u = r[None, :]
    for l in range(n_lvl):
        half = 1 << l
        blk = 2 * half
        m = r - r % blk + half - 1
        up = (r % blk) >= half
        if l < n_fine:
            msum[l] = np.where(up[:, None], (u > m[:, None]) & (u <= r[:, None]),
                               (u > r[:, None]) & (u <= m[:, None]))
        pair[l] = ((r[:, None] // blk) == (r[None, :] // blk)) & up[:, None] & ~up[None, :]
    pair[n_lvl] = np.eye(C)
    msum[n_fine] = u <= r[:, None]
    return jnp.asarray(msum.reshape(-1, C), BF16), jnp.asarray(pair, F32)


def _split3(g):
    hi = g.astype(BF16)
    r1 = g - hi.astype(F32)
    mid = r1.astype(BF16)
    lo = (r1 - mid.astype(F32)).astype(BF16)
    return hi, mid, lo


def _hgrn_chunks(qs, gs, ks, vs, gates, hgains, sts, msum, pair_ref, valid, side=()):
    side = list(side)
    n_lvl = pair_ref.shape[0] - 1
    per_phase = -(-len(side) // (n_lvl + 1))

    def run_side():
        for _ in range(min(per_phase, len(side))):
            side.pop(0)()

    n = range(len(qs))
    C = qs[0].shape[0]
    n_fine = msum.shape[0] // C - 1
    qf = [q.astype(F32) for q in qs]
    kf = [k.astype(F32) for k in ks]
    if valid is not None:
        gs = [g * valid for g in gs]
        kf = [k * valid for k in kf]
        ks = [k.astype(BF16) for k in kf]
    pieces = [_split3(g) for g in gs]
    e2 = [jnp.dot(msum, jnp.concatenate(p[:2], axis=1), preferred_element_type=F32) for p in pieces]
    e = [t[:, :LANE] + t[:, LANE:] for t in e2]
    c = [e[b][n_fine * C:] + jnp.dot(msum[n_fine * C:], pieces[b][2], preferred_element_type=F32)
         for b in n]
    x_fine = [jnp.exp2(t[:n_fine * C]) for t in e]
    run_side()

    scores = [lax.dot_general(qs[b], ks[b], _NT, preferred_element_type=F32) * pair_ref[n_lvl] for b in n]
    for l in range(n_lvl):
        if l < n_fine:
            xl = [t[l * C:(l + 1) * C] for t in x_fine]
        else:
            half = 1 << l
            xl = []
            for t in c:
                c3 = t.reshape(C // (2 * half), 2 * half, LANE)
                cm = c3[:, half - 1:half, :]
                xl.append(jnp.exp2(_neg_abs(c3 - cm)).reshape(C, LANE))
        scores = [scores[b] + lax.dot_general((qf[b] * xl[b]).astype(BF16), (kf[b] * xl[b]).astype(BF16), _NT,
                                              preferred_element_type=F32) * pair_ref[l] for b in n]
        run_side()
    while side:
        side.pop(0)()
    xc = [jnp.exp2(t) for t in c]
    xt = [jnp.exp2(t[C - 1:C, :] - t) for t in c]
    o = [lax.dot_general((qf[b] * xc[b]).astype(BF16), sts[b].astype(BF16), _NT, preferred_element_type=F32)
         for b in n]
    o = [o[b] + jnp.dot(scores[b].astype(BF16), vs[b], preferred_element_type=F32) for b in n]
    st_new = [sts[b] * xc[b][C - 1:C, :] + lax.dot_general(vs[b], (kf[b] * xt[b]).astype(BF16), _TN,
                                                           preferred_element_type=F32) for b in n]
    og = [(_rms(o[b]) * hgains[b] * gates[b].astype(F32)).astype(BF16) for b in n]
    return og, st_new


def _hgrn_mixer_body(*refs, n_batch, n_chunks, n_heads, job, n_alias):
    (q_ref, gate_ref, v_ref, g_ref, k_ref, hgain_ref, msum_ref, pair_ref,
     qs_ref, gates_ref, vs_ref, gs_ref, ks_ref, s_ref) = refs[:14]
    n_w = 0 if job is None else 3
    w_src = refs[14:14 + n_w]
    n_in = 14 + n_w + n_alias
    og_ref, so_ref, ogs_ref, sn_ref = refs[n_in:n_in + 4]
    w_dst = refs[n_in + 4:n_in + 4 + n_w]
    st_ref, o_scr = refs[n_in + 4 + n_w:]
    C = CHUNK
    s = pl.program_id(1)
    if job is not None:
        _cast_step(job, pl.program_id(0) * pl.num_programs(1) + s, w_src, w_dst)

    def step(seqs, sts, valid, side=()):
        tiles = [(r, pl.ds(hh * LANE, LANE)) for r, hh in seqs]
        return _hgrn_chunks([q_ref[t] for t in tiles], [g_ref[t] for t in tiles], [k_ref[t] for t in tiles],
                            [v_ref[t] for t in tiles], [gate_ref[t] for t in tiles],
                            [hgain_ref[:, t[1]] for t in tiles], sts, msum_ref[...], pair_ref, valid, side)

    @pl.when(s == 0)
    def _():
        valid = (lax.broadcasted_iota(jnp.int32, (C, 1), 0) < N_META).astype(F32)
        seqs = [(pl.ds(C, C), hh) for hh in range(n_heads)]
        og, st_new = step(seqs, [jnp.zeros((LANE, HA_DK), F32)] * n_heads, valid)
        og_ref[...] = jnp.zeros_like(og_ref)
        for hh in range(n_heads):
            og_ref[pl.ds(C, C), pl.ds(hh * LANE, LANE)] = og[hh]
            for b in range(n_batch):
                st_ref[hh * n_batch + b] = st_new[hh]

    @pl.when(s > 0)
    def _():
        rows, rows2, second = _sample_rows(s)
        side = []
        for hh in range(n_heads):
            lanes = pl.ds(hh * LANE, LANE)
            q2 = qs_ref[rows2, lanes]
            v1 = _half(vs_ref[rows2, lanes].astype(F32), second)
            ft = _col_bcast(jnp.exp2(gs_ref[rows, lanes]))
            kt = _col_bcast(_half(ks_ref[rows2, lanes].astype(F32), second))

            def update(b, hh=hh, q2=q2, v1=v1, ft=ft, kt=kt):
                sn = ft[:, b:b + 1] * s_ref[b, hh] + kt[:, b:b + 1] * v1[b:b + 1, :]
                sn_ref[b, hh] = sn
                res = jnp.dot(q2, sn.astype(BF16), preferred_element_type=F32)
                o_scr[hh, b:b + 1, :] = _half(res, second)[b:b + 1, :]

            def finish(hh=hh, lanes=lanes):
                gate1 = _half(gates_ref[rows2, lanes].astype(F32), second)
                ogs_ref[rows, lanes] = _rms(o_scr[hh]) * hgain_ref[:, lanes] * gate1

            side += [functools.partial(update, b) for b in range(SAMPLES_PER_STEP)] + [finish]

        seqs = [(pl.ds(b * C, C), hh) for hh in range(n_heads) for b in range(n_batch)]
        og, st_new = step(seqs, [st_ref[i] for i in range(len(seqs))], None, side)
        for i, (r, hh) in enumerate(seqs):
            og_ref[r, pl.ds(hh * LANE, LANE)] = og[i]
            st_ref[i] = st_new[i]

    @pl.when(s == n_chunks)
    def _():
        for hh in range(n_heads):
            for b in range(n_batch):
                so_ref[b, hh] = st_ref[hh * n_batch + b].T


def _hgrn_mixer(pb, g, k, hgain, state, new_state, layer, n_batch, seq, tables, sample_blk, w_f32, w_bf, job):
    R = pb.shape[0]
    D = pb.shape[1] // 3
    H = D // HA_DK
    C = CHUNK
    B = n_batch
    hp = HGRN_HEADS_PER_STEP
    n_chunks = seq // C
    n_sample = state.shape[1]
    msum, pair = tables
    grid = (H // hp, 1 + n_chunks)
    assert B >= 2 and R >= n_chunks * B * C + 2 * C and H % hp == 0
    assert n_sample == SAMPLES_PER_STEP * n_chunks and (job is None or 3 * job.span <= grid[0] * grid[1])

    def spec(col0):
        return pl.BlockSpec((B * C, hp * LANE),
                            lambda h, s: (jnp.where(s == 0, n_chunks, s - 1), col0 // hp + h))

    def sspec(col0):
        return pl.BlockSpec((LANE, hp * LANE), lambda h, s: (sample_blk, col0 // hp + h))

    state_spec = pl.BlockSpec((None, SAMPLES_PER_STEP, hp, HA_DK, LANE),
                              lambda h, s: (layer, jnp.maximum(s - 1, 0), h, 0, 0))
    out = _mixer_call(
        functools.partial(_hgrn_mixer_body, n_batch=B, n_chunks=n_chunks, n_heads=hp, job=job),
        "hgrn_mixer", grid,
        in_specs=[spec(0), spec(H), spec(2 * H), spec(0), spec(0),
                  pl.BlockSpec((1, hp * LANE), lambda h, s: (0, h)),
                  pl.BlockSpec(msum.shape, lambda h, s: (0, 0)),
                  pl.BlockSpec(pair.shape, lambda h, s: (0, 0, 0)),
                  sspec(0), sspec(H), sspec(2 * H), sspec(0), sspec(0), state_spec],
        args=[pb, pb, pb, g, k, hgain.reshape(1, D), msum, pair, pb, pb, pb, g, k, state],
        out_specs=[spec(0), pl.BlockSpec((B, hp, HA_DK, LANE), lambda h, s: (0, h, 0, 0)),
                   pl.BlockSpec((LANE, hp * LANE), lambda h, s: (0, h)), state_spec],
        out_shape=[jax.ShapeDtypeStruct((R, D), BF16), jax.ShapeDtypeStruct((B, H, HA_DK, LANE), F32),
                   jax.ShapeDtypeStruct((LANE, D), F32), jax.ShapeDtypeStruct(state.shape, F32)],
        scratch_shapes=[pltpu.VMEM((hp * B, LANE, HA_DK), F32), pltpu.VMEM((hp, SAMPLES_PER_STEP, LANE), F32)],
        state=state, new_state=new_state, w_f32=w_f32, w_bf=w_bf, job=job, D=D)
    return out[0], out[1], out[2], out[3], (tuple(out[4:]) if job is not None else w_bf)


def _rotary(x, cos, sin):
    half = x.shape[-1] // 2
    x1, x2 = x[:, :half], x[:, half:]
    return jnp.concatenate([x1 * cos - x2 * sin, x1 * sin + x2 * cos], axis=-1)


def _ret_chunks(qs, ks, vs, gates, cos, sin, lg, sts, n_valid):
    n = range(len(qs))
    C, dk = qs[0].shape
    t_col = lax.broadcasted_iota(jnp.int32, (C, 1), 0)
    t_row = lax.broadcasted_iota(jnp.int32, (1, C), 1)
    n_col = jnp.minimum(t_col + 1, n_valid).astype(F32)
    n_row = jnp.minimum(t_row + 1, n_valid).astype(F32)
    n_last = float(min(C, n_valid))
    decay = jnp.where(t_col >= t_row, jnp.exp(lg * (n_col - n_row)), 0.0)
    q_scale = jnp.exp(lg * n_col)
    k_scale = jnp.exp(lg * (n_last - n_col))
    k_norm = dk ** -0.5
    if n_valid < C:
        k_norm = k_norm * (t_col < n_valid).astype(F32)
    q = [_rotary(t.astype(F32), cos, sin) for t in qs]
    k = [_rotary(t.astype(F32), cos, sin) * k_norm for t in ks]
    scores = [lax.dot_general(q[b].astype(BF16), k[b].astype(BF16), _NT, preferred_element_type=F32) * decay
              for b in n]
    o = [jnp.dot((q[b] * q_scale).astype(BF16), sts[b].astype(BF16), preferred_element_type=F32) for b in n]
    o = [o[b] + jnp.dot(scores[b].astype(BF16), vs[b], preferred_element_type=F32) for b in n]
    st_new = [jnp.exp(lg * n_last) * sts[b] + lax.dot_general((k[b] * k_scale).astype(BF16), vs[b], _TN,
                                                              preferred_element_type=F32) for b in n]
    og = [(_rms(o[b]) * gates[b].astype(F32)).astype(BF16) for b in n]
    return og, st_new


def _ret_mixer_body(*refs, n_batch, n_chunks, job, n_alias):
    (lg_ref, q_ref, k_ref, v_ref, gate_ref, cos_ref, sin_ref,
     qs_ref, ks_ref, vs_ref, gates_ref, coss_ref, sins_ref, s_ref) = refs[:14]
    n_w = 0 if job is None else 3
    w_src = refs[14:14 + n_w]
    n_in = 14 + n_w + n_alias
    og_ref, so_ref, ogs_ref, sn_ref = refs[n_in:n_in + 4]
    w_dst = refs[n_in + 4:n_in + 4 + n_w]
    st_ref, o_scr = refs[n_in + 4 + n_w:]
    C = CHUNK
    lg = lg_ref[pl.program_id(0)]
    s = pl.program_id(1)
    if job is not None:
        _cast_step(job, pl.program_id(0) * pl.num_programs(1) + s, w_src, w_dst)

    def step(rows, sts, n_valid):
        return _ret_chunks([q_ref[r, :] for r in rows], [k_ref[r, :] for r in rows], [v_ref[r, :] for r in rows],
                           [gate_ref[r, :] for r in rows], cos_ref[...], sin_ref[...], lg, sts, n_valid)

    @pl.when(s == 0)
    def _():
        og, st_new = step([pl.ds(C, C)], [jnp.zeros(st_ref.shape[1:], F32)], N_META)
        og_ref[...] = jnp.zeros_like(og_ref)
        og_ref[pl.ds(C, C), :] = og[0]
        for b in range(n_batch):
            st_ref[b] = st_new[0]

    @pl.when(s > 0)
    def _():
        rows, rows2, second = _sample_rows(s)
        dk = qs_ref.shape[-1]
        gamma = jnp.exp(lg)
        cos, sin = coss_ref[...], sins_ref[...]
        q = _rotary(_half(qs_ref[rows2, :].astype(F32), second), cos, sin)
        k = _rotary(_half(ks_ref[rows2, :].astype(F32), second), cos, sin) * (dk ** -0.5)
        v = _half(vs_ref[rows2, :].astype(F32), second)
        qt = jnp.concatenate([_col_bcast(q[:, :LANE]), _col_bcast(q[:, LANE:])], axis=0)
        kt = jnp.concatenate([_col_bcast(k[:, :LANE]), _col_bcast(k[:, LANE:])], axis=0)
        for b in range(SAMPLES_PER_STEP):
            sn = gamma * s_ref[b] + kt[:, b:b + 1] * v[b:b + 1, :]
            sn_ref[b] = sn
            o_scr[b:b + 1, :] = jnp.sum(qt[:, b:b + 1] * sn, axis=0, keepdims=True)
        ogs_ref[rows, :] = _rms(o_scr[...]) * _half(gates_ref[rows2, :].astype(F32), second)

        og, st_new = step([pl.ds(b * C, C) for b in range(n_batch)], [st_ref[b] for b in range(n_batch)], C)
        for b in range(n_batch):
            og_ref[pl.ds(b * C, C), :] = og[b]
            st_ref[b] = st_new[b]

    @pl.when(s == n_chunks)
    def _():
        so_ref[...] = st_ref[...]


def _ret_mixer(p, log_gamma, cos_tab, sin_tab, cos_s, sin_s, state, new_state, layer, n_batch, seq,
               sample_blk, w_f32, w_bf, job):
    R = p.shape[0]
    D = p.shape[1] // 6
    H = HB_HEADS
    dk = D // H
    dv = 2 * D // H
    C = CHUNK
    B = n_batch
    n_chunks = seq // C
    n_sample = state.shape[1]
    qk_off = 2 * D // dk
    v_off = 4 * D // dv
    grid = (H, 1 + n_chunks)
    assert B >= 2 and R >= n_chunks * B * C + 2 * C and dk == 2 * LANE
    assert n_sample == SAMPLES_PER_STEP * n_chunks and (job is None or 3 * job.span <= grid[0] * grid[1])

    def spec(width, col0):
        return pl.BlockSpec((B * C, width), lambda h, s: (jnp.where(s == 0, n_chunks, s - 1), col0 + h))

    def sspec(width, col0):
        return pl.BlockSpec((LANE, width), lambda h, s: (sample_blk, col0 + h))

    tab_spec = pl.BlockSpec((C, dk // 2), lambda h, s: (s, 0))
    pos_spec = pl.BlockSpec((1, dk // 2), lambda h, s: (0, 0))
    state_spec = pl.BlockSpec((None, SAMPLES_PER_STEP, None, dk, dv),
                              lambda h, s: (layer, jnp.maximum(s - 1, 0), h, 0, 0))
    out = _mixer_call(
        functools.partial(_ret_mixer_body, n_batch=B, n_chunks=n_chunks, job=job),
        "ret_mixer", grid,
        in_specs=[pl.BlockSpec(memory_space=pltpu.SMEM),
                  spec(dk, qk_off), spec(dk, qk_off + H), spec(dv, v_off), spec(dv, 0), tab_spec, tab_spec,
                  sspec(dk, qk_off), sspec(dk, qk_off + H), sspec(dv, v_off), sspec(dv, 0), pos_spec, pos_spec,
                  state_spec],
        args=[log_gamma, p, p, p, p, cos_tab, sin_tab, p, p, p, p, cos_s, sin_s, state],
        out_specs=[spec(dv, 0), pl.BlockSpec((B, None, dk, dv), lambda h, s: (0, h, 0, 0)),
                   pl.BlockSpec((LANE, dv), lambda h, s: (0, h)), state_spec],
        out_shape=[jax.ShapeDtypeStruct((R, 2 * D), BF16), jax.ShapeDtypeStruct((B, H, dk, dv), F32),
                   jax.ShapeDtypeStruct((LANE, 2 * D), F32), jax.ShapeDtypeStruct(state.shape, F32)],
        scratch_shapes=[pltpu.VMEM((B, dk, dv), F32), pltpu.VMEM((SAMPLES_PER_STEP, dv), F32)],
        state=state, new_state=new_state, w_f32=w_f32, w_bf=w_bf, job=job, D=D)
    return out[0], out[1], out[2], out[3], (tuple(out[4:]) if job is not None else w_bf)


def _rope_tables(pos, half):
    inv = ROPE_BASE ** (-jnp.linspace(0.0, 1.0, half, dtype=F32))
    ang = pos.astype(F32)[:, None] * inv[None, :]
    return jnp.cos(ang), jnp.sin(ang)


def kernel(x_prompt, x_sample, state_hgrn, state_ret, meta_tokens, norm_ffn, ffn_w_gate, ffn_w_up, ffn_w_down, norm_mix, hg_wq, hg_wf, hg_wi, hg_wg, hg_wo, hg_norm, hg_lb_logits, rt_wq, rt_wk, rt_wv, rt_wg, rt_wo, norm_final):
    B, L, D = x_prompt.shape
    NS = x_sample.shape[0]
    depth = norm_mix.shape[0]
    n_meta = meta_tokens.shape[0]
    C = CHUNK
    assert n_meta == N_META and x_sample.shape[1] == 1 and L % C == 0 and NS <= LANE
    n_chunks = L // C

    tp = B * L
    sp = LANE
    rows0 = tp + sp + C
    tm = next(t for t in ROW_TILES if _round_up(rows0, t) - rows0 < LANE) if rows0 >= ROW_TILES[0] else rows0
    R = _round_up(rows0, tm)
    home = (B, n_chunks, C, D)
    x_tail = jnp.concatenate([
        x_sample.reshape(NS, D), jnp.zeros((sp - NS, D), F32),
        meta_tokens.astype(F32), jnp.zeros((B * C - sp - n_meta, D), F32)], axis=0)
    sample_blk = tp // LANE

    w_f32 = (ffn_w_gate, ffn_w_up, ffn_w_down)
    d_ff = ffn_w_gate.shape[-1]
    n_cast_tiles = _round_up(d_ff, FF_TILE) // CAST_TILE
    w_bf = _cast_first(w_f32, 0, 0)

    hg_in = _cast_side_by_side([hg_wq, hg_wg, hg_wi])
    hg_f = _cast_side_by_side([hg_wf])
    hg_out = _cast_side_by_side([hg_wo])
    rt_in = _cast_side_by_side([rt_wg, rt_wq, rt_wk, rt_wv])
    rt_out = _cast_side_by_side([rt_wo])

    half = D // HB_HEADS // 2
    cos_p, sin_p = _rope_tables(jnp.arange(n_meta + L, dtype=jnp.int32), half)
    cos_s, sin_s = _rope_tables(PAST_LEN + jnp.arange(1, dtype=jnp.int32), half)

    def chunk_table(t):
        meta = jnp.concatenate([t[:n_meta], jnp.zeros((C - n_meta, half), F32)], axis=0)
        return jnp.concatenate([meta, t[n_meta:]], axis=0)

    cos_tab, sin_tab = chunk_table(cos_p), chunk_table(sin_p)
    log_gamma = jnp.log1p(-jnp.exp2(-5.0 - jnp.arange(HB_HEADS, dtype=F32)))
    tables = _hgrn_tables(C)

    new_hgrn_p, new_ret_p = [], []
    new_hgrn_s = new_ret_s = None
    for i in range(depth):
        if i == 0:
            x = _ffn((x_prompt.reshape(home), x_tail), norm_ffn[i, 0], w_bf, i, 0, tm, mode="first", n_rows=R)
        else:
            x = _ffn(x, norm_ffn[i, 0], w_bf, i, 0, tm)
        j = i // 2
        job = None
        if i % 2 == 0:
            later = [(l, n) for l in range(i, depth) for n in (0, 1)][1:]
            job = CastJob(tuple(later[:4] if i + 2 < depth else later), d_ff, n_cast_tiles)
        if i % 2 == 0:
            pb = _proj(x, norm_mix[i], hg_in, j, 2 * D, tm)
            g, k = _gate_proj(x, norm_mix[i], hg_f, hg_lb_logits, j, tm)
            og, s_p, og_s, new_hgrn_s, w_bf = _hgrn_mixer(
                pb, g, k, hg_norm[j], state_hgrn, new_hgrn_s, j, B, L, tables, sample_blk, w_f32, w_bf, job)
            new_hgrn_p.append(s_p)
            w_out = hg_out
        else:
            pb = _proj(x, norm_mix[i], rt_in, j, 2 * D, tm)
            og, s_p, og_s, new_ret_s, w_bf = _ret_mixer(
                pb, log_gamma, cos_tab, sin_tab, cos_s, sin_s, state_ret, new_ret_s, j, B, L,
                sample_blk, w_f32, w_bf, job)
            new_ret_p.append(s_p)
            w_out = rt_out
        og = lax.dynamic_update_slice(og, og_s[:NS].astype(BF16), (tp, 0))
        x = _outproj(og, w_out, j, x, tm)
        if i < depth - 1:
            x = _ffn(x, norm_ffn[i, 1], w_bf, i, 1, tm)
    y_home, y_tail = _ffn(x, norm_ffn[depth - 1, 1], w_bf, depth - 1, 1, tm, mode="last", fgain=norm_final,
                          home=home)
    y_prompt = y_home.reshape(B, L, D)
    y_sample = y_tail[:NS].reshape(NS, 1, D)
    return (y_prompt, y_sample, jnp.stack(new_hgrn_p), new_hgrn_s, jnp.stack(new_ret_p), new_ret_s)
```

```python
import functools
import math
from typing import NamedTuple

import numpy as np
import jax
import jax.numpy as jnp
from jax import lax
from jax.experimental import pallas as pl
from jax.experimental.pallas import tpu as pltpu

F32 = jnp.float32
BF16 = jnp.bfloat16

EPS = 1e-6
LB_FLOOR = 1e-30
ROPE_BASE = 10000.0
N_META = 16
PAST_LEN = 16384
HA_DK = 128
HB_HEADS = 8

LANE = 128
SUBLANE = 8
CHUNK = 128
HGRN_HEADS_PER_STEP = 2
SAMPLES_PER_STEP = SUBLANE
FF_TILE = 512
CAST_TILE = 512
CAST_BLOCK_ELEMS = 1 << 21
PROJ_TILE = 2048
GATE_TILE = 1024
OUT_TILE = 1024
OUT_WEIGHT_VMEM = 16 * 1024 * 1024
ROW_TILES = (768, 512, 256, 128)
VMEM_LIMIT = 56 * 1024 * 1024

_NT = (((1,), (1,)), ((), ()))
_TN = (((0,), (0,)), ((), ()))


def _round_up(a, m):
    return (a + m - 1) // m * m


def _cparams(n_axes):
    return pltpu.CompilerParams(dimension_semantics=("arbitrary",) * n_axes,
                                vmem_limit_bytes=VMEM_LIMIT)


def _rms(x):
    return x * lax.rsqrt(jnp.mean(x * x, axis=-1, keepdims=True) + EPS)


def _sigmoid(x):
    return 1.0 / (1.0 + jnp.exp(-x))


def _silu(x):
    return x * _sigmoid(x)


def _cast_slab(src_ref, dst_ref, axis, active, tile, n_tiles, d_ff):
    last = tile == n_tiles - 1

    @pl.when(active & jnp.logical_not(last))
    def _():
        dst_ref[...] = src_ref[...].astype(BF16)

    @pl.when(active & last)
    def _():
        w = src_ref[...]
        idx = lax.broadcasted_iota(jnp.int32, w.shape, axis)
        dst_ref[...] = jnp.where(idx < d_ff - (n_tiles - 1) * CAST_TILE, w, 0.0).astype(BF16)


def _cast_first_body(wg_ref, wu_ref, wd_ref, og_ref, ou_ref, od_ref, *, d_ff):
    j = pl.program_id(0)
    n = pl.num_programs(0)
    every_step = j >= 0
    _cast_slab(wg_ref, og_ref, 1, every_step, j, n, d_ff)
    _cast_slab(wu_ref, ou_ref, 1, every_step, j, n, d_ff)
    _cast_slab(wd_ref, od_ref, 0, every_step, j, n, d_ff)


def _cast_specs(D, index):
    per_ff = FF_TILE // CAST_TILE

    def src(kind):
        def f(*ids):
            l, i, t = index(kind, *ids)
            return (l, i, t, 0) if kind == 2 else (l, i, 0, t)
        shape = (None, None, CAST_TILE, D) if kind == 2 else (None, None, D, CAST_TILE)
        return pl.BlockSpec(shape, f)

    def dst(kind):
        def f(*ids):
            l, i, t = index(kind, *ids)
            return (l, i, t, 0) if kind == 2 else (l, i, t // per_ff, 0, t % per_ff)
        shape = (None, None, CAST_TILE, D) if kind == 2 else (None, None, None, D, CAST_TILE)
        return pl.BlockSpec(shape, f)

    return [src(k) for k in range(3)], [dst(k) for k in range(3)]


def _cast_first(w_f32, layer, idx):
    wg, wu, wd = w_f32
    n_l, n_i, D, F = wg.shape
    f_pad = _round_up(F, FF_TILE)
    n_tiles = f_pad // CAST_TILE
    assert 0 < F - (n_tiles - 1) * CAST_TILE <= CAST_TILE
    srcs, dsts = _cast_specs(D, lambda kind, j: (layer, idx, j))
    return pl.pallas_call(
        functools.partial(_cast_first_body, d_ff=F),
        grid=(n_tiles,),
        in_specs=srcs,
        out_specs=dsts,
        out_shape=[jax.ShapeDtypeStruct((n_l, n_i, f_pad // FF_TILE, D, FF_TILE), BF16)] * 2 + [
            jax.ShapeDtypeStruct((n_l, n_i, f_pad, D), BF16)],
        compiler_params=_cparams(1),
        name="cast_first",
    )(wg, wu, wd)


def _cast_into_body(w_ref, *rest):
    rest[-1][...] = w_ref[...].astype(BF16)


def _cast_into(w, out, n_cols, col0):
    n, K, F = w.shape
    ct = max(LANE, min(F, CAST_BLOCK_ELEMS // K))
    assert F % ct == 0 and col0 % ct == 0
    in_specs = [pl.BlockSpec((None, K, ct), lambda l, j: (l, 0, j))]
    args = [w]
    if out is not None:
        in_specs.append(pl.BlockSpec(memory_space=pl.ANY))
        args.append(out)
    return pl.pallas_call(
        _cast_into_body,
        grid=(n, F // ct),
        in_specs=in_specs,
        out_specs=pl.BlockSpec((None, K, ct), lambda l, j: (l, 0, col0 // ct + j)),
        out_shape=jax.ShapeDtypeStruct((n, K, n_cols), BF16),
        input_output_aliases={1: 0} if out is not None else {},
        compiler_params=_cparams(2),
        name="cast_into",
    )(*args)


def _cast_side_by_side(ws):
    n_cols = sum(w.shape[2] for w in ws)
    out, col0 = None, 0
    for w in ws:
        out = _cast_into(w, out, n_cols, col0)
        col0 += w.shape[2]
    return out


class CastJob(NamedTuple):
    targets: tuple
    d_ff: int
    n_tiles: int

    @property
    def span(self):
        return len(self.targets) * self.n_tiles

    def index(self, kind, t):
        tt = jnp.clip(t - kind * self.span, 0, self.span - 1)
        which = tt // self.n_tiles
        layer, idx = self.targets[0]
        for n, (l, i) in enumerate(self.targets[1:], 1):
            layer = jnp.where(which == n, l, layer)
            idx = jnp.where(which == n, i, idx)
        return layer, idx, tt % self.n_tiles


def _cast_step(job, t, src_refs, dst_refs):
    for kind in range(3):
        active = (t >= kind * job.span) & (t < (kind + 1) * job.span)
        tile = (t - kind * job.span) % job.n_tiles
        _cast_slab(src_refs[kind], dst_refs[kind], 0 if kind == 2 else 1, active, tile, job.n_tiles, job.d_ff)


def _ffn_body(*refs, mode, n_chunks, n_tail):
    refs = list(refs)
    x_refs = [refs.pop(0) for _ in range(2 if mode == "first" else 1)]
    gain_ref, wg_ref, wu_ref, wd_ref, fgain_ref = refs[:5]
    if mode == "last":
        y4_ref, yt_ref, h_ref, acc_ref = refs[5:]
    else:
        acc_ref, h_ref = refs[5:]
    i = pl.program_id(0)
    j = pl.program_id(1)

    def init(x):
        h_ref[...] = (_rms(x) * gain_ref[...]).astype(BF16)
        acc_ref[...] = x

    if mode == "first":
        pl.when((j == 0) & (i < n_chunks))(lambda: init(x_refs[0][...].reshape(acc_ref.shape)))
        pl.when((j == 0) & (i == n_chunks))(lambda: init(x_refs[1][...]))
    elif mode == "last":
        @pl.when(j == 0)
        def _():
            rows = lax.broadcasted_iota(jnp.int32, acc_ref.shape, 0)
            init(jnp.where(rows < jnp.where(i == n_chunks, n_tail, acc_ref.shape[0]), x_refs[0][...], 0.0))
    else:
        pl.when(j == 0)(lambda: init(x_refs[0][...]))

    h = h_ref[...]
    g = jnp.dot(h, wg_ref[...], preferred_element_type=F32)
    u = jnp.dot(h, wu_ref[...], preferred_element_type=F32)
    a = (0.5 * _silu(g) * u).astype(BF16)
    acc_ref[...] += jnp.dot(a, wd_ref[...], preferred_element_type=F32)

    if mode == "last":
        last_j = j == pl.num_programs(1) - 1

        @pl.when(last_j & (i < n_chunks))
        def _():
            y4_ref[...] = (_rms(acc_ref[...]) * fgain_ref[...]).reshape(y4_ref.shape)

        @pl.when(last_j & (i == n_chunks))
        def _():
            yt_ref[...] = _rms(acc_ref[...]) * fgain_ref[...]


def _ffn(x, gain, w_bf, layer, idx, tm, mode="mid", fgain=None, n_rows=None, home=None):
    wg, wu, wd = w_bf
    nj = wg.shape[2]
    D = wg.shape[3]
    w_specs = [
        pl.BlockSpec((None, None, None, D, FF_TILE), lambda i, j: (layer, idx, j, 0, 0)),
        pl.BlockSpec((None, None, None, D, FF_TILE), lambda i, j: (layer, idx, j, 0, 0)),
        pl.BlockSpec((None, None, FF_TILE, D), lambda i, j: (layer, idx, j, 0)),
    ]
    vec_spec = pl.BlockSpec((1, D), lambda i, j: (0, 0))
    if fgain is None:
        fgain = gain
    n_chunks = n_tail = 0
    if mode == "mid":
        R = x.shape[0]
        grid = (R // tm, nj)
        x_args, x_specs = [x], [pl.BlockSpec((tm, D), lambda i, j: (i, 0))]
        out_specs = pl.BlockSpec((tm, D), lambda i, j: (i, 0))
        out_shape = jax.ShapeDtypeStruct((R, D), F32)
        scratch = [pltpu.VMEM((tm, D), BF16)]
    else:
        if mode == "first":
            home, R = x[0].shape, n_rows
        else:
            R = x.shape[0]
        B, n_chunks, C, _ = home
        tm = B * C
        n_tail = R - n_chunks * tm
        assert 0 < n_tail <= tm
        grid = (n_chunks + 1, nj)
        home_spec = pl.BlockSpec((B, None, C, D), lambda i, j: (0, jnp.minimum(i, n_chunks - 1), 0, 0))
        tail_spec = pl.BlockSpec((tm, D), lambda i, j: (0, 0))
        rows_spec = pl.BlockSpec((tm, D), lambda i, j: (i, 0))
        if mode == "first":
            x_args, x_specs = list(x), [home_spec, tail_spec]
            out_specs, out_shape = rows_spec, jax.ShapeDtypeStruct((R, D), F32)
            scratch = [pltpu.VMEM((tm, D), BF16)]
        else:
            x_args, x_specs = [x], [rows_spec]
            out_specs = [home_spec, tail_spec]
            out_shape = [jax.ShapeDtypeStruct(home, F32), jax.ShapeDtypeStruct((tm, D), F32)]
            scratch = [pltpu.VMEM((tm, D), BF16), pltpu.VMEM((tm, D), F32)]
    return pl.pallas_call(
        functools.partial(_ffn_body, mode=mode, n_chunks=n_chunks, n_tail=n_tail),
        grid=grid,
        in_specs=x_specs + [vec_spec] + w_specs + [vec_spec],
        out_specs=out_specs,
        out_shape=out_shape,
        scratch_shapes=scratch,
        compiler_params=_cparams(2),
        name="ffn_" + mode,
    )(*x_args, gain.reshape(1, D), wg, wu, wd, fgain.reshape(1, D))


def _proj_body(x_ref, gain_ref, w_ref, o_ref, h_ref, *, n_silu):
    j = pl.program_id(1)

    @pl.when(j == 0)
    def _():
        h_ref[...] = (_rms(x_ref[...]) * gain_ref[...]).astype(BF16)

    y = jnp.dot(h_ref[...], w_ref[...], preferred_element_type=F32)
    if n_silu > 0:
        y = jnp.where(j < n_silu, _silu(y), y)
    o_ref[...] = y.astype(BF16)


def _proj(x, gain, w, layer, n_silu_cols, tm):
    R, D = x.shape
    N = w.shape[2]
    return pl.pallas_call(
        functools.partial(_proj_body, n_silu=n_silu_cols // PROJ_TILE),
        grid=(R // tm, N // PROJ_TILE),
        in_specs=[
            pl.BlockSpec((tm, D), lambda i, j: (i, 0)),
            pl.BlockSpec((1, D), lambda i, j: (0, 0)),
            pl.BlockSpec((None, D, PROJ_TILE), lambda i, j: (layer, 0, j)),
        ],
        out_specs=pl.BlockSpec((tm, PROJ_TILE), lambda i, j: (i, j)),
        out_shape=jax.ShapeDtypeStruct((R, N), BF16),
        scratch_shapes=[pltpu.VMEM((tm, D), BF16)],
        compiler_params=_cparams(2),
        name="proj",
    )(x, gain.reshape(1, D), w)


def _lower_bound(logits, layer):
    e = jnp.exp(logits - jnp.max(logits, axis=0, keepdims=True))
    probs = e / jnp.sum(e, axis=0, keepdims=True)
    cs = probs[0:1]
    for i in range(1, layer + 1):
        cs = cs + probs[i:i + 1]
    return cs - probs[0:1]


def _gate_proj_body(x_ref, gain_ref, w_ref, logit_ref, g_ref, k_ref, h_ref, *, layer):
    j = pl.program_id(1)

    @pl.when(j == 0)
    def _():
        h_ref[...] = (_rms(x_ref[...]) * gain_ref[...]).astype(BF16)

    z = jnp.dot(h_ref[...], w_ref[...], preferred_element_type=F32)
    lb = _lower_bound(logit_ref[...], layer)
    t = jnp.exp(-jnp.abs(z))
    r = 1.0 / (1.0 + t)
    tr = t * r
    pos = z >= 0.0
    g_ref[...] = jnp.log(jnp.maximum(lb, LB_FLOOR) + (1.0 - lb) * jnp.where(pos, r, tr))
    k_ref[...] = ((1.0 - lb) * jnp.where(pos, tr, r)).astype(BF16)


def _gate_proj(x, gain, w, logits, layer, tm):
    R, D = x.shape
    N = w.shape[2]
    spec = pl.BlockSpec((tm, GATE_TILE), lambda i, j: (i, j))
    return pl.pallas_call(
        functools.partial(_gate_proj_body, layer=layer),
        grid=(R // tm, N // GATE_TILE),
        in_specs=[
            pl.BlockSpec((tm, D), lambda i, j: (i, 0)),
            pl.BlockSpec((1, D), lambda i, j: (0, 0)),
            pl.BlockSpec((None, D, GATE_TILE), lambda i, j: (layer, 0, j)),
            pl.BlockSpec((logits.shape[0], GATE_TILE), lambda i, j: (0, j)),
        ],
        out_specs=[spec, spec],
        out_shape=[jax.ShapeDtypeStruct((R, N), F32), jax.ShapeDtypeStruct((R, N), BF16)],
        scratch_shapes=[pltpu.VMEM((tm, D), BF16)],
        compiler_params=_cparams(2),
        name="gate_proj",
    )(x, gain.reshape(1, D), w, logits)


def _outproj_body(a_ref, w_ref, x_ref, o_ref):
    o_ref[...] = x_ref[...] + jnp.dot(a_ref[...], w_ref[...], preferred_element_type=F32)


def _outproj(a, w, layer, x, tm):
    R, K = a.shape
    D = w.shape[2]
    tn = D if 2 * K * D * w.dtype.itemsize <= OUT_WEIGHT_VMEM else OUT_TILE
    return pl.pallas_call(
        _outproj_body,
        grid=(R // tm, D // tn),
        in_specs=[
            pl.BlockSpec((tm, K), lambda i, j: (i, 0)),
            pl.BlockSpec((None, K, tn), lambda i, j: (layer, 0, j)),
            pl.BlockSpec((tm, tn), lambda i, j: (i, j)),
        ],
        out_specs=pl.BlockSpec((tm, tn), lambda i, j: (i, j)),
        out_shape=jax.ShapeDtypeStruct((R, D), F32),
        compiler_params=_cparams(2),
        name="outproj",
    )(a, w, x)


def _col_bcast(x):
    bs = x.shape[0]
    return jnp.concatenate([x] * (LANE // bs), axis=0).T


def _sample_rows(s):
    grp = s - 1
    n = SAMPLES_PER_STEP
    rows = pl.ds(pl.multiple_of(grp * n, n), n)
    rows2 = pl.ds(pl.multiple_of((grp // 2) * 2 * n, 2 * n), 2 * n)
    return rows, rows2, (grp % 2) == 1


def _half(x2, second):
    n = SAMPLES_PER_STEP
    return jnp.where(second, x2[n:], x2[:n])


def _mixer_call(body, name, grid, in_specs, args, out_specs, out_shape, scratch_shapes, state, new_state,
                w_f32, w_bf, job, D):
    n_s = grid[1]
    any_spec = pl.BlockSpec(memory_space=pl.ANY)
    in_specs, args, out_specs, out_shape = list(in_specs), list(args), list(out_specs), list(out_shape)
    i_state = len(out_shape) - 1
    if job is None:
        w_bf = ()
    else:
        srcs, dsts = _cast_specs(D, lambda kind, h, s: job.index(kind, h * n_s + s))
        in_specs += srcs
        args += list(w_f32)
        out_specs += dsts
        out_shape += [jax.ShapeDtypeStruct(w.shape, w.dtype) for w in w_bf]
    aliases = {}
    if new_state is not None:
        aliases[len(args)] = i_state
        in_specs.append(any_spec)
        args.append(new_state)
    for n, w in enumerate(w_bf):
        aliases[len(args)] = i_state + 1 + n
        in_specs.append(any_spec)
        args.append(w)
    return pl.pallas_call(
        functools.partial(body, n_alias=len(aliases)),
        grid=grid, in_specs=in_specs, out_specs=out_specs, out_shape=out_shape,
        scratch_shapes=scratch_shapes, input_output_aliases=aliases,
        compiler_params=_cparams(2), name=name,
    )(*args)


def _hgrn_tables(C):
    n_lvl = int(math.log2(C))
    assert 1 << n_lvl == C
    n_fine = min(n_lvl, int(math.log2(SUBLANE)) - 1)
    r = np.arange(C)
    msum = np.zeros((n_fine + 1, C, C), np.float32)
    pair = np.zeros((n_lvl + 1, C, C), np.float32)
    u = r[None, :]
    for l in range(n_lvl):
        half = 1 << l
        blk = 2 * half
        m = r - r % blk + half - 1
        up = (r % blk) >= half
        if l < n_fine:
            msum[l] = np.where(up[:, None], (u > m[:, None]) & (u <= r[:, None]),
                               (u > r[:, None]) & (u <= m[:, None]))
        pair[l] = ((r[:, None] // blk) == (r[None, :] // blk)) & up[:, None] & ~up[None, :]
    pair[n_lvl] = np.eye(C)
    msum[n_fine] = u <= r[:, None]
    return jnp.asarray(msum.reshape(-1, C), BF16), jnp.asarray(pair, F32)


def _split3(g):
    hi = g.astype(BF16)
    r1 = g - hi.astype(F32)
    mid = r1.astype(BF16)
    lo = (r1 - mid.astype(F32)).astype(BF16)
    return hi, mid, lo


def _hgrn_chunks(qs, gs, ks, vs, gates, hgains, sts, msum, pair_ref, valid, side=()):
    side = list(side)
    n_lvl = pair_ref.shape[0] - 1
    per_phase = -(-len(side) // (n_lvl + 1))

    def run_side():
        for _ in range(min(per_phase, len(side))):
            side.pop(0)()

    n = range(len(qs))
    C = qs[0].shape[0]
    n_fine = msum.shape[0] // C - 1
    qf = [q.astype(F32) for q in qs]
    kf = [k.astype(F32) for k in ks]
    if valid is not None:
        gs = [g * valid for g in gs]
        kf = [k * valid for k in kf]
        ks = [k.astype(BF16) for k in kf]
    pieces = [_split3(g) for g in gs]
    e2 = [jnp.dot(msum, jnp.concatenate(p[:2], axis=1), preferred_element_type=F32) for p in pieces]
    e = [t[:, :LANE] + t[:, LANE:] for t in e2]
    c = [e[b][n_fine * C:] + jnp.dot(msum[n_fine * C:], pieces[b][2], preferred_element_type=F32)
         for b in n]
    x_fine = [jnp.exp(t[:n_fine * C]) for t in e]
    run_side()

    scores = [lax.dot_general(qs[b], ks[b], _NT, preferred_element_type=F32) * pair_ref[n_lvl] for b in n]
    for l in range(n_lvl):
        if l < n_fine:
            xl = [t[l * C:(l + 1) * C] for t in x_fine]
        else:
            half = 1 << l
            xl = []
            for t in c:
                c3 = t.reshape(C // (2 * half), 2 * half, LANE)
                cm = c3[:, half - 1:half, :]
                xl.append(jnp.exp(-jnp.abs(c3 - cm)).reshape(C, LANE))
        scores = [scores[b] + lax.dot_general((qf[b] * xl[b]).astype(BF16), (kf[b] * xl[b]).astype(BF16), _NT,
                                              preferred_element_type=F32) * pair_ref[l] for b in n]
        run_side()
    while side:
        side.pop(0)()
    xc = [jnp.exp(t) for t in c]
    xt = [jnp.exp(t[C - 1:C, :] - t) for t in c]
    o = [lax.dot_general((qf[b] * xc[b]).astype(BF16), sts[b].astype(BF16), _NT, preferred_element_type=F32)
         for b in n]
    o = [o[b] + jnp.dot(scores[b].astype(BF16), vs[b], preferred_element_type=F32) for b in n]
    st_new = [sts[b] * xc[b][C - 1:C, :] + lax.dot_general(vs[b], (kf[b] * xt[b]).astype(BF16), _TN,
                                                           preferred_element_type=F32) for b in n]
    og = [(_rms(o[b]) * hgains[b] * gates[b].astype(F32)).astype(BF16) for b in n]
    return og, st_new


def _hgrn_mixer_body(*refs, n_batch, n_chunks, n_heads, job, n_alias):
    (q_ref, gate_ref, v_ref, g_ref, k_ref, hgain_ref, msum_ref, pair_ref,
     qs_ref, gates_ref, vs_ref, gs_ref, ks_ref, s_ref) = refs[:14]
    n_w = 0 if job is None else 3
    w_src = refs[14:14 + n_w]
    n_in = 14 + n_w + n_alias
    og_ref, so_ref, ogs_ref, sn_ref = refs[n_in:n_in + 4]
    w_dst = refs[n_in + 4:n_in + 4 + n_w]
    st_ref, o_scr = refs[n_in + 4 + n_w:]
    C = CHUNK
    s = pl.program_id(1)
    if job is not None:
        _cast_step(job, pl.program_id(0) * pl.num_programs(1) + s, w_src, w_dst)

    def step(seqs, sts, valid, side=()):
        tiles = [(r, pl.ds(hh * LANE, LANE)) for r, hh in seqs]
        return _hgrn_chunks([q_ref[t] for t in tiles], [g_ref[t] for t in tiles], [k_ref[t] for t in tiles],
                            [v_ref[t] for t in tiles], [gate_ref[t] for t in tiles],
                            [hgain_ref[:, t[1]] for t in tiles], sts, msum_ref[...], pair_ref, valid, side)

    @pl.when(s == 0)
    def _():
        valid = (lax.broadcasted_iota(jnp.int32, (C, 1), 0) < N_META).astype(F32)
        seqs = [(pl.ds(C, C), hh) for hh in range(n_heads)]
        og, st_new = step(seqs, [jnp.zeros((LANE, HA_DK), F32)] * n_heads, valid)
        og_ref[...] = jnp.zeros_like(og_ref)
        for hh in range(n_heads):
            og_ref[pl.ds(C, C), pl.ds(hh * LANE, LANE)] = og[hh]
            for b in range(n_batch):
                st_ref[hh * n_batch + b] = st_new[hh]

    @pl.when(s > 0)
    def _():
        rows, rows2, second = _sample_rows(s)
        side = []
        for hh in range(n_heads):
            lanes = pl.ds(hh * LANE, LANE)
            q2 = qs_ref[rows2, lanes]
            v1 = _half(vs_ref[rows2, lanes].astype(F32), second)
            ft = _col_bcast(jnp.exp(gs_ref[rows, lanes]))
            kt = _col_bcast(_half(ks_ref[rows2, lanes].astype(F32), second))

            def update(b, hh=hh, q2=q2, v1=v1, ft=ft, kt=kt):
                sn = ft[:, b:b + 1] * s_ref[b, hh] + kt[:, b:b + 1] * v1[b:b + 1, :]
                sn_ref[b, hh] = sn
                res = jnp.dot(q2, sn.astype(BF16), preferred_element_type=F32)
                o_scr[hh, b:b + 1, :] = _half(res, second)[b:b + 1, :]

            def finish(hh=hh, lanes=lanes):
                gate1 = _half(gates_ref[rows2, lanes].astype(F32), second)
                ogs_ref[rows, lanes] = _rms(o_scr[hh]) * hgain_ref[:, lanes] * gate1

            side += [functools.partial(update, b) for b in range(SAMPLES_PER_STEP)] + [finish]

        seqs = [(pl.ds(b * C, C), hh) for hh in range(n_heads) for b in range(n_batch)]
        og, st_new = step(seqs, [st_ref[i] for i in range(len(seqs))], None, side)
        for i, (r, hh) in enumerate(seqs):
            og_ref[r, pl.ds(hh * LANE, LANE)] = og[i]
            st_ref[i] = st_new[i]

    @pl.when(s == n_chunks)
    def _():
        for hh in range(n_heads):
            for b in range(n_batch):
                so_ref[b, hh] = st_ref[hh * n_batch + b].T


def _hgrn_mixer(pb, g, k, hgain, state, new_state, layer, n_batch, seq, tables, sample_blk, w_f32, w_bf, job):
    R = pb.shape[0]
    D = pb.shape[1] // 3
    H = D // HA_DK
    C = CHUNK
    B = n_batch
    hp = HGRN_HEADS_PER_STEP
    n_chunks = seq // C
    n_sample = state.shape[1]
    msum, pair = tables
    grid = (H // hp, 1 + n_chunks)
    assert B >= 2 and R >= n_chunks * B * C + 2 * C and H % hp == 0
    assert n_sample == SAMPLES_PER_STEP * n_chunks and (job is None or 3 * job.span <= grid[0] * grid[1])

    def spec(col0):
        return pl.BlockSpec((B * C, hp * LANE),
                            lambda h, s: (jnp.where(s == 0, n_chunks, s - 1), col0 // hp + h))

    def sspec(col0):
        return pl.BlockSpec((LANE, hp * LANE), lambda h, s: (sample_blk, col0 // hp + h))

    state_spec = pl.BlockSpec((None, SAMPLES_PER_STEP, hp, HA_DK, LANE),
                              lambda h, s: (layer, jnp.maximum(s - 1, 0), h, 0, 0))
    out = _mixer_call(
        functools.partial(_hgrn_mixer_body, n_batch=B, n_chunks=n_chunks, n_heads=hp, job=job),
        "hgrn_mixer", grid,
        in_specs=[spec(0), spec(H), spec(2 * H), spec(0), spec(0),
                  pl.BlockSpec((1, hp * LANE), lambda h, s: (0, h)),
                  pl.BlockSpec(msum.shape, lambda h, s: (0, 0)),
                  pl.BlockSpec(pair.shape, lambda h, s: (0, 0, 0)),
                  sspec(0), sspec(H), sspec(2 * H), sspec(0), sspec(0), state_spec],
        args=[pb, pb, pb, g, k, hgain.reshape(1, D), msum, pair, pb, pb, pb, g, k, state],
        out_specs=[spec(0), pl.BlockSpec((B, hp, HA_DK, LANE), lambda h, s: (0, h, 0, 0)),
                   pl.BlockSpec((LANE, hp * LANE), lambda h, s: (0, h)), state_spec],
        out_shape=[jax.ShapeDtypeStruct((R, D), BF16), jax.ShapeDtypeStruct((B, H, HA_DK, LANE), F32),
                   jax.ShapeDtypeStruct((LANE, D), F32), jax.ShapeDtypeStruct(state.shape, F32)],
        scratch_shapes=[pltpu.VMEM((hp * B, LANE, HA_DK), F32), pltpu.VMEM((hp, SAMPLES_PER_STEP, LANE), F32)],
        state=state, new_state=new_state, w_f32=w_f32, w_bf=w_bf, job=job, D=D)
    return out[0], out[1], out[2], out[3], (tuple(out[4:]) if job is not None else w_bf)


def _rotary(x, cos, sin):
    half = x.shape[-1] // 2
    x1, x2 = x[:, :half], x[:, half:]
    return jnp.concatenate([x1 * cos - x2 * sin, x1 * sin + x2 * cos], axis=-1)


def _ret_chunks(qs, ks, vs, gates, cos, sin, lg, sts, n_valid):
    n = range(len(qs))
    C, dk = qs[0].shape
    t_col = lax.broadcasted_iota(jnp.int32, (C, 1), 0)
    t_row = lax.broadcasted_iota(jnp.int32, (1, C), 1)
    n_col = jnp.minimum(t_col + 1, n_valid).astype(F32)
    n_row = jnp.minimum(t_row + 1, n_valid).astype(F32)
    n_last = float(min(C, n_valid))
    decay = jnp.where(t_col >= t_row, jnp.exp(lg * (n_col - n_row)), 0.0)
    q_scale = jnp.exp(lg * n_col)
    k_scale = jnp.exp(lg * (n_last - n_col))
    k_norm = dk ** -0.5
    if n_valid < C:
        k_norm = k_norm * (t_col < n_valid).astype(F32)
    q = [_rotary(t.astype(F32), cos, sin) for t in qs]
    k = [_rotary(t.astype(F32), cos, sin) * k_norm for t in ks]
    scores = [lax.dot_general(q[b].astype(BF16), k[b].astype(BF16), _NT, preferred_element_type=F32) * decay
              for b in n]
    o = [jnp.dot((q[b] * q_scale).astype(BF16), sts[b].astype(BF16), preferred_element_type=F32) for b in n]
    o = [o[b] + jnp.dot(scores[b].astype(BF16), vs[b], preferred_element_type=F32) for b in n]
    st_new = [jnp.exp(lg * n_last) * sts[b] + lax.dot_general((k[b] * k_scale).astype(BF16), vs[b], _TN,
                                                              preferred_element_type=F32) for b in n]
    og = [(_rms(o[b]) * gates[b].astype(F32)).astype(BF16) for b in n]
    return og, st_new


def _ret_mixer_body(*refs, n_batch, n_chunks, job, n_alias):
    (lg_ref, q_ref, k_ref, v_ref, gate_ref, cos_ref, sin_ref,
     qs_ref, ks_ref, vs_ref, gates_ref, coss_ref, sins_ref, s_ref) = refs[:14]
    n_w = 0 if job is None else 3
    w_src = refs[14:14 + n_w]
    n_in = 14 + n_w + n_alias
    og_ref, so_ref, ogs_ref, sn_ref = refs[n_in:n_in + 4]
    w_dst = refs[n_in + 4:n_in + 4 + n_w]
    st_ref, o_scr = refs[n_in + 4 + n_w:]
    C = CHUNK
    lg = lg_ref[pl.program_id(0)]
    s = pl.program_id(1)
    if job is not None:
        _cast_step(job, pl.program_id(0) * pl.num_programs(1) + s, w_src, w_dst)

    def step(rows, sts, n_valid):
        return _ret_chunks([q_ref[r, :] for r in rows], [k_ref[r, :] for r in rows], [v_ref[r, :] for r in rows],
                           [gate_ref[r, :] for r in rows], cos_ref[...], sin_ref[...], lg, sts, n_valid)

    @pl.when(s == 0)
    def _():
        og, st_new = step([pl.ds(C, C)], [jnp.zeros(st_ref.shape[1:], F32)], N_META)
        og_ref[...] = jnp.zeros_like(og_ref)
        og_ref[pl.ds(C, C), :] = og[0]
        for b in range(n_batch):
            st_ref[b] = st_new[0]

    @pl.when(s > 0)
    def _():
        rows, rows2, second = _sample_rows(s)
        dk = qs_ref.shape[-1]
        gamma = jnp.exp(lg)
        cos, sin = coss_ref[...], sins_ref[...]
        q = _rotary(_half(qs_ref[rows2, :].astype(F32), second), cos, sin)
        k = _rotary(_half(ks_ref[rows2, :].astype(F32), second), cos, sin) * (dk ** -0.5)
        v = _half(vs_ref[rows2, :].astype(F32), second)
        qt = jnp.concatenate([_col_bcast(q[:, :LANE]), _col_bcast(q[:, LANE:])], axis=0)
        kt = jnp.concatenate([_col_bcast(k[:, :LANE]), _col_bcast(k[:, LANE:])], axis=0)
        for b in range(SAMPLES_PER_STEP):
            sn = gamma * s_ref[b] + kt[:, b:b + 1] * v[b:b + 1, :]
            sn_ref[b] = sn
            o_scr[b:b + 1, :] = jnp.sum(qt[:, b:b + 1] * sn, axis=0, keepdims=True)
        ogs_ref[rows, :] = _rms(o_scr[...]) * _half(gates_ref[rows2, :].astype(F32), second)

        og, st_new = step([pl.ds(b * C, C) for b in range(n_batch)], [st_ref[b] for b in range(n_batch)], C)
        for b in range(n_batch):
            og_ref[pl.ds(b * C, C), :] = og[b]
            st_ref[b] = st_new[b]

    @pl.when(s == n_chunks)
    def _():
        so_ref[...] = st_ref[...]


def _ret_mixer(p, log_gamma, cos_tab, sin_tab, cos_s, sin_s, state, new_state, layer, n_batch, seq,
               sample_blk, w_f32, w_bf, job):
    R = p.shape[0]
    D = p.shape[1] // 6
    H = HB_HEADS
    dk = D // H
    dv = 2 * D // H
    C = CHUNK
    B = n_batch
    n_chunks = seq // C
    n_sample = state.shape[1]
    qk_off = 2 * D // dk
    v_off = 4 * D // dv
    grid = (H, 1 + n_chunks)
    assert B >= 2 and R >= n_chunks * B * C + 2 * C and dk == 2 * LANE
    assert n_sample == SAMPLES_PER_STEP * n_chunks and (job is None or 3 * job.span <= grid[0] * grid[1])

    def spec(width, col0):
        return pl.BlockSpec((B * C, width), lambda h, s: (jnp.where(s == 0, n_chunks, s - 1), col0 + h))

    def sspec(width, col0):
        return pl.BlockSpec((LANE, width), lambda h, s: (sample_blk, col0 + h))

    tab_spec = pl.BlockSpec((C, dk // 2), lambda h, s: (s, 0))
    pos_spec = pl.BlockSpec((1, dk // 2), lambda h, s: (0, 0))
    state_spec = pl.BlockSpec((None, SAMPLES_PER_STEP, None, dk, dv),
                              lambda h, s: (layer, jnp.maximum(s - 1, 0), h, 0, 0))
    out = _mixer_call(
        functools.partial(_ret_mixer_body, n_batch=B, n_chunks=n_chunks, job=job),
        "ret_mixer", grid,
        in_specs=[pl.BlockSpec(memory_space=pltpu.SMEM),
                  spec(dk, qk_off), spec(dk, qk_off + H), spec(dv, v_off), spec(dv, 0), tab_spec, tab_spec,
                  sspec(dk, qk_off), sspec(dk, qk_off + H), sspec(dv, v_off), sspec(dv, 0), pos_spec, pos_spec,
                  state_spec],
        args=[log_gamma, p, p, p, p, cos_tab, sin_tab, p, p, p, p, cos_s, sin_s, state],
        out_specs=[spec(dv, 0), pl.BlockSpec((B, None, dk, dv), lambda h, s: (0, h, 0, 0)),
                   pl.BlockSpec((LANE, dv), lambda h, s: (0, h)), state_spec],
        out_shape=[jax.ShapeDtypeStruct((R, 2 * D), BF16), jax.ShapeDtypeStruct((B, H, dk, dv), F32),
                   jax.ShapeDtypeStruct((LANE, 2 * D), F32), jax.ShapeDtypeStruct(state.shape, F32)],
        scratch_shapes=[pltpu.VMEM((B, dk, dv), F32), pltpu.VMEM((SAMPLES_PER_STEP, dv), F32)],
        state=state, new_state=new_state, w_f32=w_f32, w_bf=w_bf, job=job, D=D)
    return out[0], out[1], out[2], out[3], (tuple(out[4:]) if job is not None else w_bf)


def _rope_tables(pos, half):
    inv = ROPE_BASE ** (-jnp.linspace(0.0, 1.0, half, dtype=F32))
    ang = pos.astype(F32)[:, None] * inv[None, :]
    return jnp.cos(ang), jnp.sin(ang)


def kernel(x_prompt, x_sample, state_hgrn, state_ret, meta_tokens, norm_ffn, ffn_w_gate, ffn_w_up, ffn_w_down, norm_mix, hg_wq, hg_wf, hg_wi, hg_wg, hg_wo, hg_norm, hg_lb_logits, rt_wq, rt_wk, rt_wv, rt_wg, rt_wo, norm_final):
    B, L, D = x_prompt.shape
    NS = x_sample.shape[0]
    depth = norm_mix.shape[0]
    n_meta = meta_tokens.shape[0]
    C = CHUNK
    assert n_meta == N_META and x_sample.shape[1] == 1 and L % C == 0 and NS <= LANE
    n_chunks = L // C

    tp = B * L
    sp = LANE
    rows0 = tp + sp + C
    tm = next(t for t in ROW_TILES if _round_up(rows0, t) - rows0 < LANE) if rows0 >= ROW_TILES[0] else rows0
    R = _round_up(rows0, tm)
    home = (B, n_chunks, C, D)
    x_tail = jnp.concatenate([
        x_sample.reshape(NS, D), jnp.zeros((sp - NS, D), F32),
        meta_tokens.astype(F32), jnp.zeros((B * C - sp - n_meta, D), F32)], axis=0)
    sample_blk = tp // LANE

    w_f32 = (ffn_w_gate, ffn_w_up, ffn_w_down)
    d_ff = ffn_w_gate.shape[-1]
    n_cast_tiles = _round_up(d_ff, FF_TILE) // CAST_TILE
    w_bf = _cast_first(w_f32, 0, 0)

    hg_in = _cast_side_by_side([hg_wq, hg_wg, hg_wi])
    hg_f = _cast_side_by_side([hg_wf])
    hg_out = _cast_side_by_side([hg_wo])
    rt_in = _cast_side_by_side([rt_wg, rt_wq, rt_wk, rt_wv])
    rt_out = _cast_side_by_side([rt_wo])

    half = D // HB_HEADS // 2
    cos_p, sin_p = _rope_tables(jnp.arange(n_meta + L, dtype=jnp.int32), half)
    cos_s, sin_s = _rope_tables(PAST_LEN + jnp.arange(1, dtype=jnp.int32), half)

    def chunk_table(t):
        meta = jnp.concatenate([t[:n_meta], jnp.zeros((C - n_meta, half), F32)], axis=0)
        return jnp.concatenate([meta, t[n_meta:]], axis=0)

    cos_tab, sin_tab = chunk_table(cos_p), chunk_table(sin_p)
    log_gamma = jnp.log1p(-jnp.exp2(-5.0 - jnp.arange(HB_HEADS, dtype=F32)))
    tables = _hgrn_tables(C)

    new_hgrn_p, new_ret_p = [], []
    new_hgrn_s = new_ret_s = None
    for i in range(depth):
        if i == 0:
            x = _ffn((x_prompt.reshape(home), x_tail), norm_ffn[i, 0], w_bf, i, 0, tm, mode="first", n_rows=R)
        else:
            x = _ffn(x, norm_ffn[i, 0], w_bf, i, 0, tm)
        j = i // 2
        job = None
        if i % 2 == 0:
            later = [(l, n) for l in range(i, depth) for n in (0, 1)][1:]
            job = CastJob(tuple(later[:4] if i + 2 < depth else later), d_ff, n_cast_tiles)
        if i % 2 == 0:
            pb = _proj(x, norm_mix[i], hg_in, j, 2 * D, tm)
            g, k = _gate_proj(x, norm_mix[i], hg_f, hg_lb_logits, j, tm)
            og, s_p, og_s, new_hgrn_s, w_bf = _hgrn_mixer(
                pb, g, k, hg_norm[j], state_hgrn, new_hgrn_s, j, B, L, tables, sample_blk, w_f32, w_bf, job)
            new_hgrn_p.append(s_p)
            w_out = hg_out
        else:
            pb = _proj(x, norm_mix[i], rt_in, j, 2 * D, tm)
            og, s_p, og_s, new_ret_s, w_bf = _ret_mixer(
                pb, log_gamma, cos_tab, sin_tab, cos_s, sin_s, state_ret, new_ret_s, j, B, L,
                sample_blk, w_f32, w_bf, job)
            new_ret_p.append(s_p)
            w_out = rt_out
        og = lax.dynamic_update_slice(og, og_s[:NS].astype(BF16), (tp, 0))
        x = _outproj(og, w_out, j, x, tm)
        if i < depth - 1:
            x = _ffn(x, norm_ffn[i, 1], w_bf, i, 1, tm)
    y_home, y_tail = _ffn(x, norm_ffn[depth - 1, 1], w_bf, depth - 1, 1, tm, mode="last", fgain=norm_final,
                          home=home)
    y_prompt = y_home.reshape(B, L, D)
    y_sample = y_tail[:NS].reshape(NS, 1, D)
    return (y_prompt, y_sample, jnp.stack(new_hgrn_p), new_hgrn_s, jnp.stack(new_ret_p), new_ret_s)
```

```python
import functools
import math
from typing import NamedTuple

import numpy as np
import jax
import jax.numpy as jnp
from jax import lax
from jax.experimental import pallas as pl
from jax.experimental.pallas import tpu as pltpu

F32 = jnp.float32
BF16 = jnp.bfloat16

EPS = 1e-6
LB_FLOOR = 1e-30
ROPE_BASE = 10000.0
N_META = 16
PAST_LEN = 16384
HA_DK = 128
HB_HEADS = 8

LANE = 128
SUBLANE = 8
CHUNK = 128
HGRN_HEADS_PER_STEP = 2
SAMPLES_PER_STEP = SUBLANE
FF_TILE = 512
CAST_TILE = 512
CAST_BLOCK_ELEMS = 1 << 21
PROJ_TILE = 2048
GATE_TILE = 1024
OUT_TILE = 1024
OUT_WEIGHT_VMEM = 16 * 1024 * 1024
ROW_TILES = (768, 512, 256, 128)
VMEM_LIMIT = 56 * 1024 * 1024

_NT = (((1,), (1,)), ((), ()))
_TN = (((0,), (0,)), ((), ()))


def _round_up(a, m):
    return (a + m - 1) // m * m


def _cparams(n_axes):
    return pltpu.CompilerParams(dimension_semantics=("arbitrary",) * n_axes,
                                vmem_limit_bytes=VMEM_LIMIT)


def _rms(x):
    return x * lax.rsqrt(jnp.mean(x * x, axis=-1, keepdims=True) + EPS)


def _sigmoid(x):
    return 1.0 / (1.0 + jnp.exp(-x))


def _silu(x):
    return x * _sigmoid(x)


def _cast_slab(src_ref, dst_ref, axis, active, tile, n_tiles, d_ff):
    last = tile == n_tiles - 1

    @pl.when(active & jnp.logical_not(last))
    def _():
        dst_ref[...] = src_ref[...].astype(BF16)

    @pl.when(active & last)
    def _():
        w = src_ref[...]
        idx = lax.broadcasted_iota(jnp.int32, w.shape, axis)
        dst_ref[...] = jnp.where(idx < d_ff - (n_tiles - 1) * CAST_TILE, w, 0.0).astype(BF16)


def _cast_first_body(wg_ref, wu_ref, wd_ref, og_ref, ou_ref, od_ref, *, d_ff):
    j = pl.program_id(0)
    n = pl.num_programs(0)
    every_step = j >= 0
    _cast_slab(wg_ref, og_ref, 1, every_step, j, n, d_ff)
    _cast_slab(wu_ref, ou_ref, 1, every_step, j, n, d_ff)
    _cast_slab(wd_ref, od_ref, 0, every_step, j, n, d_ff)


def _cast_specs(D, index):
    per_ff = FF_TILE // CAST_TILE

    def src(kind):
        def f(*ids):
            l, i, t = index(kind, *ids)
            return (l, i, t, 0) if kind == 2 else (l, i, 0, t)
        shape = (None, None, CAST_TILE, D) if kind == 2 else (None, None, D, CAST_TILE)
        return pl.BlockSpec(shape, f)

    def dst(kind):
        def f(*ids):
            l, i, t = index(kind, *ids)
            return (l, i, t, 0) if kind == 2 else (l, i, t // per_ff, 0, t % per_ff)
        shape = (None, None, CAST_TILE, D) if kind == 2 else (None, None, None, D, CAST_TILE)
        return pl.BlockSpec(shape, f)

    return [src(k) for k in range(3)], [dst(k) for k in range(3)]


def _cast_first(w_f32, layer, idx):
    wg, wu, wd = w_f32
    n_l, n_i, D, F = wg.shape
    f_pad = _round_up(F, FF_TILE)
    n_tiles = f_pad // CAST_TILE
    assert 0 < F - (n_tiles - 1) * CAST_TILE <= CAST_TILE
    srcs, dsts = _cast_specs(D, lambda kind, j: (layer, idx, j))
    return pl.pallas_call(
        functools.partial(_cast_first_body, d_ff=F),
        grid=(n_tiles,),
        in_specs=srcs,
        out_specs=dsts,
        out_shape=[jax.ShapeDtypeStruct((n_l, n_i, f_pad // FF_TILE, D, FF_TILE), BF16)] * 2 + [
            jax.ShapeDtypeStruct((n_l, n_i, f_pad, D), BF16)],
        compiler_params=_cparams(1),
        name="cast_first",
    )(wg, wu, wd)


def _cast_side_by_side_body(*refs, first_tile):
    o_ref = refs[-1]
    j = pl.program_id(1)
    for k, w_ref in enumerate(refs[:-1]):
        def convert(w_ref=w_ref):
            o_ref[...] = w_ref[...].astype(BF16)
        pl.when((j >= first_tile[k]) & (j < first_tile[k + 1]))(convert)


def _cast_side_by_side(ws):
    n, K, _ = ws[0].shape
    widths = [w.shape[2] for w in ws]
    ct = LANE
    while 2 * ct * K * len(ws) <= CAST_BLOCK_ELEMS and all(F % (2 * ct) == 0 for F in widths):
        ct *= 2
    assert all(F % ct == 0 for F in widths)
    first_tile = [0]
    for F in widths:
        first_tile.append(first_tile[-1] + F // ct)

    def src_spec(k):
        lo, hi = first_tile[k], first_tile[k + 1]
        return pl.BlockSpec((None, K, ct), lambda l, j: (l, 0, jnp.clip(j - lo, 0, hi - lo - 1)))

    return pl.pallas_call(
        functools.partial(_cast_side_by_side_body, first_tile=tuple(first_tile)),
        grid=(n, first_tile[-1]),
        in_specs=[src_spec(k) for k in range(len(ws))],
        out_specs=pl.BlockSpec((None, K, ct), lambda l, j: (l, 0, j)),
        out_shape=jax.ShapeDtypeStruct((n, K, sum(widths)), BF16),
        compiler_params=_cparams(2),
        name="cast_side_by_side",
    )(*ws)


class CastJob(NamedTuple):
    targets: tuple
    d_ff: int
    n_tiles: int

    @property
    def span(self):
        return len(self.targets) * self.n_tiles

    def index(self, kind, t):
        tt = jnp.clip(t - kind * self.span, 0, self.span - 1)
        which = tt // self.n_tiles
        layer, idx = self.targets[0]
        for n, (l, i) in enumerate(self.targets[1:], 1):
            layer = jnp.where(which == n, l, layer)
            idx = jnp.where(which == n, i, idx)
        return layer, idx, tt % self.n_tiles


def _cast_step(job, t, src_refs, dst_refs):
    for kind in range(3):
        active = (t >= kind * job.span) & (t < (kind + 1) * job.span)
        tile = (t - kind * job.span) % job.n_tiles
        _cast_slab(src_refs[kind], dst_refs[kind], 0 if kind == 2 else 1, active, tile, job.n_tiles, job.d_ff)


def _ffn_body(*refs, mode, n_chunks, n_tail):
    refs = list(refs)
    x_refs = [refs.pop(0) for _ in range(2 if mode == "first" else 1)]
    gain_ref, wg_ref, wu_ref, wd_ref, fgain_ref = refs[:5]
    if mode == "last":
        y4_ref, yt_ref, h_ref, acc_ref = refs[5:]
    else:
        acc_ref, h_ref = refs[5:]
    i = pl.program_id(0)
    j = pl.program_id(1)

    def init(x):
        h_ref[...] = (_rms(x) * gain_ref[...]).astype(BF16)
        acc_ref[...] = x

    if mode == "first":
        pl.when((j == 0) & (i < n_chunks))(lambda: init(x_refs[0][...].reshape(acc_ref.shape)))
        pl.when((j == 0) & (i == n_chunks))(lambda: init(x_refs[1][...]))
    elif mode == "last":
        @pl.when(j == 0)
        def _():
            rows = lax.broadcasted_iota(jnp.int32, acc_ref.shape, 0)
            init(jnp.where(rows < jnp.where(i == n_chunks, n_tail, acc_ref.shape[0]), x_refs[0][...], 0.0))
    else:
        pl.when(j == 0)(lambda: init(x_refs[0][...]))

    h = h_ref[...]
    g = jnp.dot(h, wg_ref[...], preferred_element_type=F32)
    u = jnp.dot(h, wu_ref[...], preferred_element_type=F32)
    a = (0.5 * _silu(g) * u).astype(BF16)
    acc_ref[...] += jnp.dot(a, wd_ref[...], preferred_element_type=F32)

    if mode == "last":
        last_j = j == pl.num_programs(1) - 1

        @pl.when(last_j & (i < n_chunks))
        def _():
            y4_ref[...] = (_rms(acc_ref[...]) * fgain_ref[...]).reshape(y4_ref.shape)

        @pl.when(last_j & (i == n_chunks))
        def _():
            yt_ref[...] = _rms(acc_ref[...]) * fgain_ref[...]


def _ffn(x, gain, w_bf, layer, idx, tm, mode="mid", fgain=None, n_rows=None, home=None):
    wg, wu, wd = w_bf
    nj = wg.shape[2]
    D = wg.shape[3]
    w_specs = [
        pl.BlockSpec((None, None, None, D, FF_TILE), lambda i, j: (layer, idx, j, 0, 0)),
        pl.BlockSpec((None, None, None, D, FF_TILE), lambda i, j: (layer, idx, j, 0, 0)),
        pl.BlockSpec((None, None, FF_TILE, D), lambda i, j: (layer, idx, j, 0)),
    ]
    vec_spec = pl.BlockSpec((1, D), lambda i, j: (0, 0))
    if fgain is None:
        fgain = gain
    n_chunks = n_tail = 0
    if mode == "mid":
        R = x.shape[0]
        grid = (R // tm, nj)
        x_args, x_specs = [x], [pl.BlockSpec((tm, D), lambda i, j: (i, 0))]
        out_specs = pl.BlockSpec((tm, D), lambda i, j: (i, 0))
        out_shape = jax.ShapeDtypeStruct((R, D), F32)
        scratch = [pltpu.VMEM((tm, D), BF16)]
    else:
        if mode == "first":
            home, R = x[0].shape, n_rows
        else:
            R = x.shape[0]
        B, n_chunks, C, _ = home
        tm = B * C
        n_tail = R - n_chunks * tm
        assert 0 < n_tail <= tm
        grid = (n_chunks + 1, nj)
        home_spec = pl.BlockSpec((B, None, C, D), lambda i, j: (0, jnp.minimum(i, n_chunks - 1), 0, 0))
        tail_spec = pl.BlockSpec((tm, D), lambda i, j: (0, 0))
        rows_spec = pl.BlockSpec((tm, D), lambda i, j: (i, 0))
        if mode == "first":
            x_args, x_specs = list(x), [home_spec, tail_spec]
            out_specs, out_shape = rows_spec, jax.ShapeDtypeStruct((R, D), F32)
            scratch = [pltpu.VMEM((tm, D), BF16)]
        else:
            x_args, x_specs = [x], [rows_spec]
            out_specs = [home_spec, tail_spec]
            out_shape = [jax.ShapeDtypeStruct(home, F32), jax.ShapeDtypeStruct((tm, D), F32)]
            scratch = [pltpu.VMEM((tm, D), BF16), pltpu.VMEM((tm, D), F32)]
    return pl.pallas_call(
        functools.partial(_ffn_body, mode=mode, n_chunks=n_chunks, n_tail=n_tail),
        grid=grid,
        in_specs=x_specs + [vec_spec] + w_specs + [vec_spec],
        out_specs=out_specs,
        out_shape=out_shape,
        scratch_shapes=scratch,
        compiler_params=_cparams(2),
        name="ffn_" + mode,
    )(*x_args, gain.reshape(1, D), wg, wu, wd, fgain.reshape(1, D))


def _proj_body(x_ref, gain_ref, w_ref, o_ref, h_ref, *, n_silu):
    j = pl.program_id(1)

    @pl.when(j == 0)
    def _():
        h_ref[...] = (_rms(x_ref[...]) * gain_ref[...]).astype(BF16)

    y = jnp.dot(h_ref[...], w_ref[...], preferred_element_type=F32)
    if n_silu > 0:
        y = jnp.where(j < n_silu, _silu(y), y)
    o_ref[...] = y.astype(BF16)


def _proj(x, gain, w, layer, n_silu_cols, tm):
    R, D = x.shape
    N = w.shape[2]
    return pl.pallas_call(
        functools.partial(_proj_body, n_silu=n_silu_cols // PROJ_TILE),
        grid=(R // tm, N // PROJ_TILE),
        in_specs=[
            pl.BlockSpec((tm, D), lambda i, j: (i, 0)),
            pl.BlockSpec((1, D), lambda i, j: (0, 0)),
            pl.BlockSpec((None, D, PROJ_TILE), lambda i, j: (layer, 0, j)),
        ],
        out_specs=pl.BlockSpec((tm, PROJ_TILE), lambda i, j: (i, j)),
        out_shape=jax.ShapeDtypeStruct((R, N), BF16),
        scratch_shapes=[pltpu.VMEM((tm, D), BF16)],
        compiler_params=_cparams(2),
        name="proj",
    )(x, gain.reshape(1, D), w)


def _lower_bound(logits, layer):
    e = jnp.exp(logits - jnp.max(logits, axis=0, keepdims=True))
    probs = e / jnp.sum(e, axis=0, keepdims=True)
    cs = probs[0:1]
    for i in range(1, layer + 1):
        cs = cs + probs[i:i + 1]
    return cs - probs[0:1]


def _gate_proj_body(x_ref, gain_ref, w_ref, logit_ref, g_ref, k_ref, h_ref, *, layer):
    j = pl.program_id(1)

    @pl.when(j == 0)
    def _():
        h_ref[...] = (_rms(x_ref[...]) * gain_ref[...]).astype(BF16)

    z = jnp.dot(h_ref[...], w_ref[...], preferred_element_type=F32)
    lb = _lower_bound(logit_ref[...], layer)
    t = jnp.exp(-jnp.abs(z))
    r = 1.0 / (1.0 + t)
    tr = t * r
    pos = z >= 0.0
    g_ref[...] = jnp.log(jnp.maximum(lb, LB_FLOOR) + (1.0 - lb) * jnp.where(pos, r, tr))
    k_ref[...] = ((1.0 - lb) * jnp.where(pos, tr, r)).astype(BF16)


def _gate_proj(x, gain, w, logits, layer, tm):
    R, D = x.shape
    N = w.shape[2]
    spec = pl.BlockSpec((tm, GATE_TILE), lambda i, j: (i, j))
    return pl.pallas_call(
        functools.partial(_gate_proj_body, layer=layer),
        grid=(R // tm, N // GATE_TILE),
        in_specs=[
            pl.BlockSpec((tm, D), lambda i, j: (i, 0)),
            pl.BlockSpec((1, D), lambda i, j: (0, 0)),
            pl.BlockSpec((None, D, GATE_TILE), lambda i, j: (layer, 0, j)),
            pl.BlockSpec((logits.shape[0], GATE_TILE), lambda i, j: (0, j)),
        ],
        out_specs=[spec, spec],
        out_shape=[jax.ShapeDtypeStruct((R, N), F32), jax.ShapeDtypeStruct((R, N), BF16)],
        scratch_shapes=[pltpu.VMEM((tm, D), BF16)],
        compiler_params=_cparams(2),
        name="gate_proj",
    )(x, gain.reshape(1, D), w, logits)


def _outproj_body(a_ref, w_ref, x_ref, o_ref):
    o_ref[...] = x_ref[...] + jnp.dot(a_ref[...], w_ref[...], preferred_element_type=F32)


def _outproj(a, w, layer, x, tm):
    R, K = a.shape
    D = w.shape[2]
    tn = D if 2 * K * D * w.dtype.itemsize <= OUT_WEIGHT_VMEM else OUT_TILE
    return pl.pallas_call(
        _outproj_body,
        grid=(R // tm, D // tn),
        in_specs=[
            pl.BlockSpec((tm, K), lambda i, j: (i, 0)),
            pl.BlockSpec((None, K, tn), lambda i, j: (layer, 0, j)),
            pl.BlockSpec((tm, tn), lambda i, j: (i, j)),
        ],
        out_specs=pl.BlockSpec((tm, tn), lambda i, j: (i, j)),
        out_shape=jax.ShapeDtypeStruct((R, D), F32),
        compiler_params=_cparams(2),
        name="outproj",
    )(a, w, x)


def _col_bcast(x):
    bs = x.shape[0]
    return jnp.concatenate([x] * (LANE // bs), axis=0).T


def _sample_rows(s):
    grp = s - 1
    n = SAMPLES_PER_STEP
    rows = pl.ds(pl.multiple_of(grp * n, n), n)
    rows2 = pl.ds(pl.multiple_of((grp // 2) * 2 * n, 2 * n), 2 * n)
    return rows, rows2, (grp % 2) == 1


def _half(x2, second):
    n = SAMPLES_PER_STEP
    return jnp.where(second, x2[n:], x2[:n])


def _mixer_call(body, name, grid, in_specs, args, out_specs, out_shape, scratch_shapes, state, new_state,
                w_f32, w_bf, job, D):
    n_s = grid[1]
    any_spec = pl.BlockSpec(memory_space=pl.ANY)
    in_specs, args, out_specs, out_shape = list(in_specs), list(args), list(out_specs), list(out_shape)
    i_state = len(out_shape) - 1
    if job is None:
        w_bf = ()
    else:
        srcs, dsts = _cast_specs(D, lambda kind, h, s: job.index(kind, h * n_s + s))
        in_specs += srcs
        args += list(w_f32)
        out_specs += dsts
        out_shape += [jax.ShapeDtypeStruct(w.shape, w.dtype) for w in w_bf]
    aliases = {}
    if new_state is not None:
        aliases[len(args)] = i_state
        in_specs.append(any_spec)
        args.append(new_state)
    for n, w in enumerate(w_bf):
        aliases[len(args)] = i_state + 1 + n
        in_specs.append(any_spec)
        args.append(w)
    return pl.pallas_call(
        functools.partial(body, n_alias=len(aliases)),
        grid=grid, in_specs=in_specs, out_specs=out_specs, out_shape=out_shape,
        scratch_shapes=scratch_shapes, input_output_aliases=aliases,
        compiler_params=_cparams(2), name=name,
    )(*args)


def _hgrn_tables(C):
    n_lvl = int(math.log2(C))
    assert 1 << n_lvl == C
    n_fine = min(n_lvl, int(math.log2(SUBLANE)) - 1)
    r = np.arange(C)
    msum = np.zeros((n_fine + 1, C, C), np.float32)
    pair = np.zeros((n_lvl + 1, C, C), np.float32)
    u = r[None, :]
    for l in range(n_lvl):
        half = 1 << l
        blk = 2 * half
        m = r - r % blk + half - 1
        up = (r % blk) >= half
        if l < n_fine:
            msum[l] = np.where(up[:, None], (u > m[:, None]) & (u <= r[:, None]),
                               (u > r[:, None]) & (u <= m[:, None]))
        pair[l] = ((r[:, None] // blk) == (r[None, :] // blk)) & up[:, None] & ~up[None, :]
    pair[n_lvl] = np.eye(C)
    msum[n_fine] = u <= r[:, None]
    return jnp.asarray(msum.reshape(-1, C), BF16), jnp.asarray(pair, F32)


def _split3(g):
    hi = g.astype(BF16)
    r1 = g - hi.astype(F32)
    mid = r1.astype(BF16)
    lo = (r1 - mid.astype(F32)).astype(BF16)
    return hi, mid, lo


def _hgrn_chunks(qs, gs, ks, vs, gates, hgains, sts, msum, pair_ref, valid, side=()):
    side = list(side)
    n_lvl = pair_ref.shape[0] - 1
    per_phase = -(-len(side) // (n_lvl + 1))

    def run_side():
        for _ in range(min(per_phase, len(side))):
            side.pop(0)()

    n = range(len(qs))
    C = qs[0].shape[0]
    n_fine = msum.shape[0] // C - 1
    qf = [q.astype(F32) for q in qs]
    kf = [k.astype(F32) for k in ks]
    if valid is not None:
        gs = [g * valid for g in gs]
        kf = [k * valid for k in kf]
        ks = [k.astype(BF16) for k in kf]
    pieces = [_split3(g) for g in gs]
    e2 = [jnp.dot(msum, jnp.concatenate(p[:2], axis=1), preferred_element_type=F32) for p in pieces]
    e = [t[:, :LANE] + t[:, LANE:] for t in e2]
    c = [e[b][n_fine * C:] + jnp.dot(msum[n_fine * C:], pieces[b][2], preferred_element_type=F32)
         for b in n]
    x_fine = [jnp.exp(t[:n_fine * C]) for t in e]
    run_side()

    scores = [lax.dot_general(qs[b], ks[b], _NT, preferred_element_type=F32) * pair_ref[n_lvl] for b in n]
    for l in range(n_lvl):
        if l < n_fine:
            xl = [t[l * C:(l + 1) * C] for t in x_fine]
        else:
            half = 1 << l
            xl = []
            for t in c:
                c3 = t.reshape(C // (2 * half), 2 * half, LANE)
                cm = c3[:, half - 1:half, :]
                xl.append(jnp.exp(-jnp.abs(c3 - cm)).reshape(C, LANE))
        scores = [scores[b] + lax.dot_general((qf[b] * xl[b]).astype(BF16), (kf[b] * xl[b]).astype(BF16), _NT,
                                              preferred_element_type=F32) * pair_ref[l] for b in n]
        run_side()
    while side:
        side.pop(0)()
    xc = [jnp.exp(t) for t in c]
    xt = [jnp.exp(t[C - 1:C, :] - t) for t in c]
    o = [lax.dot_general((qf[b] * xc[b]).astype(BF16), sts[b].astype(BF16), _NT, preferred_element_type=F32)
         for b in n]
    o = [o[b] + jnp.dot(scores[b].astype(BF16), vs[b], preferred_element_type=F32) for b in n]
    st_new = [sts[b] * xc[b][C - 1:C, :] + lax.dot_general(vs[b], (kf[b] * xt[b]).astype(BF16), _TN,
                                                           preferred_element_type=F32) for b in n]
    og = [(_rms(o[b]) * hgains[b] * gates[b].astype(F32)).astype(BF16) for b in n]
    return og, st_new


def _hgrn_mixer_body(*refs, n_batch, n_chunks, n_heads, job, n_alias):
    (q_ref, gate_ref, v_ref, g_ref, k_ref, hgain_ref, msum_ref, pair_ref,
     qs_ref, gates_ref, vs_ref, gs_ref, ks_ref, s_ref) = refs[:14]
    n_w = 0 if job is None else 3
    w_src = refs[14:14 + n_w]
    n_in = 14 + n_w + n_alias
    og_ref, so_ref, ogs_ref, sn_ref = refs[n_in:n_in + 4]
    w_dst = refs[n_in + 4:n_in + 4 + n_w]
    st_ref, o_scr = refs[n_in + 4 + n_w:]
    C = CHUNK
    s = pl.program_id(1)
    if job is not None:
        _cast_step(job, pl.program_id(0) * pl.num_programs(1) + s, w_src, w_dst)

    def step(seqs, sts, valid, side=()):
        tiles = [(r, pl.ds(hh * LANE, LANE)) for r, hh in seqs]
        return _hgrn_chunks([q_ref[t] for t in tiles], [g_ref[t] for t in tiles], [k_ref[t] for t in tiles],
                            [v_ref[t] for t in tiles], [gate_ref[t] for t in tiles],
                            [hgain_ref[:, t[1]] for t in tiles], sts, msum_ref[...], pair_ref, valid, side)

    @pl.when(s == 0)
    def _():
        valid = (lax.broadcasted_iota(jnp.int32, (C, 1), 0) < N_META).astype(F32)
        seqs = [(pl.ds(C, C), hh) for hh in range(n_heads)]
        og, st_new = step(seqs, [jnp.zeros((LANE, HA_DK), F32)] * n_heads, valid)
        og_ref[...] = jnp.zeros_like(og_ref)
        for hh in range(n_heads):
            og_ref[pl.ds(C, C), pl.ds(hh * LANE, LANE)] = og[hh]
            for b in range(n_batch):
                st_ref[hh * n_batch + b] = st_new[hh]

    @pl.when(s > 0)
    def _():
        rows, rows2, second = _sample_rows(s)
        side = []
        for hh in range(n_heads):
            lanes = pl.ds(hh * LANE, LANE)
            q2 = qs_ref[rows2, lanes]
            v1 = _half(vs_ref[rows2, lanes].astype(F32), second)
            ft = _col_bcast(jnp.exp(gs_ref[rows, lanes]))
            kt = _col_bcast(_half(ks_ref[rows2, lanes].astype(F32), second))

            def update(b, hh=hh, q2=q2, v1=v1, ft=ft, kt=kt):
                sn = ft[:, b:b + 1] * s_ref[b, hh] + kt[:, b:b + 1] * v1[b:b + 1, :]
                sn_ref[b, hh] = sn
                res = jnp.dot(q2, sn.astype(BF16), preferred_element_type=F32)
                o_scr[hh, b:b + 1, :] = _half(res, second)[b:b + 1, :]

            def finish(hh=hh, lanes=lanes):
                gate1 = _half(gates_ref[rows2, lanes].astype(F32), second)
                ogs_ref[rows, lanes] = _rms(o_scr[hh]) * hgain_ref[:, lanes] * gate1

            side += [functools.partial(update, b) for b in range(SAMPLES_PER_STEP)] + [finish]

        seqs = [(pl.ds(b * C, C), hh) for hh in range(n_heads) for b in range(n_batch)]
        og, st_new = step(seqs, [st_ref[i] for i in range(len(seqs))], None, side)
        for i, (r, hh) in enumerate(seqs):
            og_ref[r, pl.ds(hh * LANE, LANE)] = og[i]
            st_ref[i] = st_new[i]

    @pl.when(s == n_chunks)
    def _():
        for hh in range(n_heads):
            for b in range(n_batch):
                so_ref[b, hh] = st_ref[hh * n_batch + b].T


def _hgrn_mixer(pb, g, k, hgain, state, new_state, layer, n_batch, seq, tables, sample_blk, w_f32, w_bf, job):
    R = pb.shape[0]
    D = pb.shape[1] // 3
    H = D // HA_DK
    C = CHUNK
    B = n_batch
    hp = HGRN_HEADS_PER_STEP
    n_chunks = seq // C
    n_sample = state.shape[1]
    msum, pair = tables
    grid = (H // hp, 1 + n_chunks)
    assert B >= 2 and R >= n_chunks * B * C + 2 * C and H % hp == 0
    assert n_sample == SAMPLES_PER_STEP * n_chunks and (job is None or 3 * job.span <= grid[0] * grid[1])

    def spec(col0):
        return pl.BlockSpec((B * C, hp * LANE),
                            lambda h, s: (jnp.where(s == 0, n_chunks, s - 1), col0 // hp + h))

    def sspec(col0):
        return pl.BlockSpec((LANE, hp * LANE), lambda h, s: (sample_blk, col0 // hp + h))

    state_spec = pl.BlockSpec((None, SAMPLES_PER_STEP, hp, HA_DK, LANE),
                              lambda h, s: (layer, jnp.maximum(s - 1, 0), h, 0, 0))
    out = _mixer_call(
        functools.partial(_hgrn_mixer_body, n_batch=B, n_chunks=n_chunks, n_heads=hp, job=job),
        "hgrn_mixer", grid,
        in_specs=[spec(0), spec(H), spec(2 * H), spec(0), spec(0),
                  pl.BlockSpec((1, hp * LANE), lambda h, s: (0, h)),
                  pl.BlockSpec(msum.shape, lambda h, s: (0, 0)),
                  pl.BlockSpec(pair.shape, lambda h, s: (0, 0, 0)),
                  sspec(0), sspec(H), sspec(2 * H), sspec(0), sspec(0), state_spec],
        args=[pb, pb, pb, g, k, hgain.reshape(1, D), msum, pair, pb, pb, pb, g, k, state],
        out_specs=[spec(0), pl.BlockSpec((B, hp, HA_DK, LANE), lambda h, s: (0, h, 0, 0)),
                   pl.BlockSpec((LANE, hp * LANE), lambda h, s: (0, h)), state_spec],
        out_shape=[jax.ShapeDtypeStruct((R, D), BF16), jax.ShapeDtypeStruct((B, H, HA_DK, LANE), F32),
                   jax.ShapeDtypeStruct((LANE, D), F32), jax.ShapeDtypeStruct(state.shape, F32)],
        scratch_shapes=[pltpu.VMEM((hp * B, LANE, HA_DK), F32), pltpu.VMEM((hp, SAMPLES_PER_STEP, LANE), F32)],
        state=state, new_state=new_state, w_f32=w_f32, w_bf=w_bf, job=job, D=D)
    return out[0], out[1], out[2], out[3], (tuple(out[4:]) if job is not None else w_bf)


def _rotary(x, cos, sin):
    half = x.shape[-1] // 2
    x1, x2 = x[:, :half], x[:, half:]
    return jnp.concatenate([x1 * cos - x2 * sin, x1 * sin + x2 * cos], axis=-1)


def _ret_chunks(qs, ks, vs, gates, cos, sin, lg, sts, n_valid):
    n = range(len(qs))
    C, dk = qs[0].shape
    t_col = lax.broadcasted_iota(jnp.int32, (C, 1), 0)
    t_row = lax.broadcasted_iota(jnp.int32, (1, C), 1)
    n_col = jnp.minimum(t_col + 1, n_valid).astype(F32)
    n_row = jnp.minimum(t_row + 1, n_valid).astype(F32)
    n_last = float(min(C, n_valid))
    decay = jnp.where(t_col >= t_row, jnp.exp(lg * (n_col - n_row)), 0.0)
    q_scale = jnp.exp(lg * n_col)
    k_scale = jnp.exp(lg * (n_last - n_col))
    k_norm = dk ** -0.5
    if n_valid < C:
        k_norm = k_norm * (t_col < n_valid).astype(F32)
    q = [_rotary(t.astype(F32), cos, sin) for t in qs]
    k = [_rotary(t.astype(F32), cos, sin) * k_norm for t in ks]
    scores = [lax.dot_general(q[b].astype(BF16), k[b].astype(BF16), _NT, preferred_element_type=F32) * decay
              for b in n]
    o = [jnp.dot((q[b] * q_scale).astype(BF16), sts[b].astype(BF16), preferred_element_type=F32) for b in n]
    o = [o[b] + jnp.dot(scores[b].astype(BF16), vs[b], preferred_element_type=F32) for b in n]
    st_new = [jnp.exp(lg * n_last) * sts[b] + lax.dot_general((k[b] * k_scale).astype(BF16), vs[b], _TN,
                                                              preferred_element_type=F32) for b in n]
    og = [(_rms(o[b]) * gates[b].astype(F32)).astype(BF16) for b in n]
    return og, st_new


def _ret_mixer_body(*refs, n_batch, n_chunks, job, n_alias):
    (lg_ref, q_ref, k_ref, v_ref, gate_ref, cos_ref, sin_ref,
     qs_ref, ks_ref, vs_ref, gates_ref, coss_ref, sins_ref, s_ref) = refs[:14]
    n_w = 0 if job is None else 3
    w_src = refs[14:14 + n_w]
    n_in = 14 + n_w + n_alias
    og_ref, so_ref, ogs_ref, sn_ref = refs[n_in:n_in + 4]
    w_dst = refs[n_in + 4:n_in + 4 + n_w]
    st_ref, o_scr = refs[n_in + 4 + n_w:]
    C = CHUNK
    lg = lg_ref[pl.program_id(0)]
    s = pl.program_id(1)
    if job is not None:
        _cast_step(job, pl.program_id(0) * pl.num_programs(1) + s, w_src, w_dst)

    def step(rows, sts, n_valid):
        return _ret_chunks([q_ref[r, :] for r in rows], [k_ref[r, :] for r in rows], [v_ref[r, :] for r in rows],
                           [gate_ref[r, :] for r in rows], cos_ref[...], sin_ref[...], lg, sts, n_valid)

    @pl.when(s == 0)
    def _():
        og, st_new = step([pl.ds(C, C)], [jnp.zeros(st_ref.shape[1:], F32)], N_META)
        og_ref[...] = jnp.zeros_like(og_ref)
        og_ref[pl.ds(C, C), :] = og[0]
        for b in range(n_batch):
            st_ref[b] = st_new[0]

    @pl.when(s > 0)
    def _():
        rows, rows2, second = _sample_rows(s)
        dk = qs_ref.shape[-1]
        gamma = jnp.exp(lg)
        cos, sin = coss_ref[...], sins_ref[...]
        q = _rotary(_half(qs_ref[rows2, :].astype(F32), second), cos, sin)
        k = _rotary(_half(ks_ref[rows2, :].astype(F32), second), cos, sin) * (dk ** -0.5)
        v = _half(vs_ref[rows2, :].astype(F32), second)
        qt = jnp.concatenate([_col_bcast(q[:, :LANE]), _col_bcast(q[:, LANE:])], axis=0)
        kt = jnp.concatenate([_col_bcast(k[:, :LANE]), _col_bcast(k[:, LANE:])], axis=0)
        for b in range(SAMPLES_PER_STEP):
            sn = gamma * s_ref[b] + kt[:, b:b + 1] * v[b:b + 1, :]
            sn_ref[b] = sn
            o_scr[b:b + 1, :] = jnp.sum(qt[:, b:b + 1] * sn, axis=0, keepdims=True)
        ogs_ref[rows, :] = _rms(o_scr[...]) * _half(gates_ref[rows2, :].astype(F32), second)

        og, st_new = step([pl.ds(b * C, C) for b in range(n_batch)], [st_ref[b] for b in range(n_batch)], C)
        for b in range(n_batch):
            og_ref[pl.ds(b * C, C), :] = og[b]
            st_ref[b] = st_new[b]

    @pl.when(s == n_chunks)
    def _():
        so_ref[...] = st_ref[...]


def _ret_mixer(p, log_gamma, cos_tab, sin_tab, cos_s, sin_s, state, new_state, layer, n_batch, seq,
               sample_blk, w_f32, w_bf, job):
    R = p.shape[0]
    D = p.shape[1] // 6
    H = HB_HEADS
    dk = D // H
    dv = 2 * D // H
    C = CHUNK
    B = n_batch
    n_chunks = seq // C
    n_sample = state.shape[1]
    qk_off = 2 * D // dk
    v_off = 4 * D // dv
    grid = (H, 1 + n_chunks)
    assert B >= 2 and R >= n_chunks * B * C + 2 * C and dk == 2 * LANE
    assert n_sample == SAMPLES_PER_STEP * n_chunks and (job is None or 3 * job.span <= grid[0] * grid[1])

    def spec(width, col0):
        return pl.BlockSpec((B * C, width), lambda h, s: (jnp.where(s == 0, n_chunks, s - 1), col0 + h))

    def sspec(width, col0):
        return pl.BlockSpec((LANE, width), lambda h, s: (sample_blk, col0 + h))

    tab_spec = pl.BlockSpec((C, dk // 2), lambda h, s: (s, 0))
    pos_spec = pl.BlockSpec((1, dk // 2), lambda h, s: (0, 0))
    state_spec = pl.BlockSpec((None, SAMPLES_PER_STEP, None, dk, dv),
                              lambda h, s: (layer, jnp.maximum(s - 1, 0), h, 0, 0))
    out = _mixer_call(
        functools.partial(_ret_mixer_body, n_batch=B, n_chunks=n_chunks, job=job),
        "ret_mixer", grid,
        in_specs=[pl.BlockSpec(memory_space=pltpu.SMEM),
                  spec(dk, qk_off), spec(dk, qk_off + H), spec(dv, v_off), spec(dv, 0), tab_spec, tab_spec,
                  sspec(dk, qk_off), sspec(dk, qk_off + H), sspec(dv, v_off), sspec(dv, 0), pos_spec, pos_spec,
                  state_spec],
        args=[log_gamma, p, p, p, p, cos_tab, sin_tab, p, p, p, p, cos_s, sin_s, state],
        out_specs=[spec(dv, 0), pl.BlockSpec((B, None, dk, dv), lambda h, s: (0, h, 0, 0)),
                   pl.BlockSpec((LANE, dv), lambda h, s: (0, h)), state_spec],
        out_shape=[jax.ShapeDtypeStruct((R, 2 * D), BF16), jax.ShapeDtypeStruct((B, H, dk, dv), F32),
                   jax.ShapeDtypeStruct((LANE, 2 * D), F32), jax.ShapeDtypeStruct(state.shape, F32)],
        scratch_shapes=[pltpu.VMEM((B, dk, dv), F32), pltpu.VMEM((SAMPLES_PER_STEP, dv), F32)],
        state=state, new_state=new_state, w_f32=w_f32, w_bf=w_bf, job=job, D=D)
    return out[0], out[1], out[2], out[3], (tuple(out[4:]) if job is not None else w_bf)


def _rope_tables(pos, half):
    inv = ROPE_BASE ** (-jnp.linspace(0.0, 1.0, half, dtype=F32))
    ang = pos.astype(F32)[:, None] * inv[None, :]
    return jnp.cos(ang), jnp.sin(ang)


def kernel(x_prompt, x_sample, state_hgrn, state_ret, meta_tokens, norm_ffn, ffn_w_gate, ffn_w_up, ffn_w_down, norm_mix, hg_wq, hg_wf, hg_wi, hg_wg, hg_wo, hg_norm, hg_lb_logits, rt_wq, rt_wk, rt_wv, rt_wg, rt_wo, norm_final):
    B, L, D = x_prompt.shape
    NS = x_sample.shape[0]
    depth = norm_mix.shape[0]
    n_meta = meta_tokens.shape[0]
    C = CHUNK
    assert n_meta == N_META and x_sample.shape[1] == 1 and L % C == 0 and NS <= LANE
    n_chunks = L // C

    tp = B * L
    sp = LANE
    rows0 = tp + sp + C
    tm = next(t for t in ROW_TILES if _round_up(rows0, t) - rows0 < LANE) if rows0 >= ROW_TILES[0] else rows0
    R = _round_up(rows0, tm)
    home = (B, n_chunks, C, D)
    x_tail = jnp.concatenate([
        x_sample.reshape(NS, D), jnp.zeros((sp - NS, D), F32),
        meta_tokens.astype(F32), jnp.zeros((B * C - sp - n_meta, D), F32)], axis=0)
    sample_blk = tp // LANE

    w_f32 = (ffn_w_gate, ffn_w_up, ffn_w_down)
    d_ff = ffn_w_gate.shape[-1]
    n_cast_tiles = _round_up(d_ff, FF_TILE) // CAST_TILE
    w_bf = _cast_first(w_f32, 0, 0)

    hg_in = _cast_side_by_side([hg_wq, hg_wg, hg_wi])
    hg_f = _cast_side_by_side([hg_wf])
    hg_out = _cast_side_by_side([hg_wo])
    rt_in = _cast_side_by_side([rt_wg, rt_wq, rt_wk, rt_wv])
    rt_out = _cast_side_by_side([rt_wo])

    half = D // HB_HEADS // 2
    cos_p, sin_p = _rope_tables(jnp.arange(n_meta + L, dtype=jnp.int32), half)
    cos_s, sin_s = _rope_tables(PAST_LEN + jnp.arange(1, dtype=jnp.int32), half)

    def chunk_table(t):
        meta = jnp.concatenate([t[:n_meta], jnp.zeros((C - n_meta, half), F32)], axis=0)
        return jnp.concatenate([meta, t[n_meta:]], axis=0)

    cos_tab, sin_tab = chunk_table(cos_p), chunk_table(sin_p)
    log_gamma = jnp.log1p(-jnp.exp2(-5.0 - jnp.arange(HB_HEADS, dtype=F32)))
    tables = _hgrn_tables(C)

    new_hgrn_p, new_ret_p = [], []
    new_hgrn_s = new_ret_s = None
    for i in range(depth):
        if i == 0:
            x = _ffn((x_prompt.reshape(home), x_tail), norm_ffn[i, 0], w_bf, i, 0, tm, mode="first", n_rows=R)
        else:
            x = _ffn(x, norm_ffn[i, 0], w_bf, i, 0, tm)
        j = i // 2
        job = None
        if i % 2 == 0:
            later = [(l, n) for l in range(i, depth) for n in (0, 1)][1:]
            job = CastJob(tuple(later[:4] if i + 2 < depth else later), d_ff, n_cast_tiles)
        if i % 2 == 0:
            pb = _proj(x, norm_mix[i], hg_in, j, 2 * D, tm)
            g, k = _gate_proj(x, norm_mix[i], hg_f, hg_lb_logits, j, tm)
            og, s_p, og_s, new_hgrn_s, w_bf = _hgrn_mixer(
                pb, g, k, hg_norm[j], state_hgrn, new_hgrn_s, j, B, L, tables, sample_blk, w_f32, w_bf, job)
            new_hgrn_p.append(s_p)
            w_out = hg_out
        else:
            pb = _proj(x, norm_mix[i], rt_in, j, 2 * D, tm)
            og, s_p, og_s, new_ret_s, w_bf = _ret_mixer(
                pb, log_gamma, cos_tab, sin_tab, cos_s, sin_s, state_ret, new_ret_s, j, B, L,
                sample_blk, w_f32, w_bf, job)
            new_ret_p.append(s_p)
            w_out = rt_out
        og = lax.dynamic_update_slice(og, og_s[:NS].astype(BF16), (tp, 0))
        x = _outproj(og, w_out, j, x, tm)
        if i < depth - 1:
            x = _ffn(x, norm_ffn[i, 1], w_bf, i, 1, tm)
    y_home, y_tail = _ffn(x, norm_ffn[depth - 1, 1], w_bf, depth - 1, 1, tm, mode="last", fgain=norm_final,
                          home=home)
    y_prompt = y_home.reshape(B, L, D)
    y_sample = y_tail[:NS].reshape(NS, 1, D)
    return (y_prompt, y_sample, jnp.stack(new_hgrn_p), new_hgrn_s, jnp.stack(new_ret_p), new_ret_s)
```

```python
import functools
import math
from typing import NamedTuple

import numpy as np
import jax
import jax.numpy as jnp
from jax import lax
from jax.experimental import pallas as pl
from jax.experimental.pallas import tpu as pltpu

F32 = jnp.float32
BF16 = jnp.bfloat16

EPS = 1e-6
LB_FLOOR = 1e-30
ROPE_BASE = 10000.0
N_META = 16
PAST_LEN = 16384
HA_DK = 128
HB_HEADS = 8

LANE = 128
SUBLANE = 8
CHUNK = 128
HGRN_HEADS_PER_STEP = 2
SAMPLES_PER_STEP = SUBLANE
FF_TILE = 512
CAST_TILE = 512
CAST_BLOCK_ELEMS = 1 << 21
PROJ_TILE = 2048
GATE_TILE = 2048
OUT_TILE = 1024
OUT_WEIGHT_VMEM = 16 * 1024 * 1024
ROW_TILES = (768, 512, 256, 128)
VMEM_LIMIT = 56 * 1024 * 1024

_NT = (((1,), (1,)), ((), ()))
_TN = (((0,), (0,)), ((), ()))


def _round_up(a, m):
    return (a + m - 1) // m * m


def _cparams(n_axes):
    return pltpu.CompilerParams(dimension_semantics=("arbitrary",) * n_axes,
                                vmem_limit_bytes=VMEM_LIMIT)


def _rms(x):
    return x * lax.rsqrt(jnp.mean(x * x, axis=-1, keepdims=True) + EPS)


def _sigmoid(x):
    return 1.0 / (1.0 + jnp.exp(-x))


def _silu(x):
    return x * _sigmoid(x)


def _cast_slab(src_ref, dst_ref, axis, active, tile, n_tiles, d_ff):
    last = tile == n_tiles - 1

    @pl.when(active & jnp.logical_not(last))
    def _():
        dst_ref[...] = src_ref[...].astype(BF16)

    @pl.when(active & last)
    def _():
        w = src_ref[...]
        idx = lax.broadcasted_iota(jnp.int32, w.shape, axis)
        dst_ref[...] = jnp.where(idx < d_ff - (n_tiles - 1) * CAST_TILE, w, 0.0).astype(BF16)


def _cast_first_body(wg_ref, wu_ref, wd_ref, og_ref, ou_ref, od_ref, *, d_ff):
    j = pl.program_id(0)
    n = pl.num_programs(0)
    every_step = j >= 0
    _cast_slab(wg_ref, og_ref, 1, every_step, j, n, d_ff)
    _cast_slab(wu_ref, ou_ref, 1, every_step, j, n, d_ff)
    _cast_slab(wd_ref, od_ref, 0, every_step, j, n, d_ff)


def _cast_specs(D, index):
    per_ff = FF_TILE // CAST_TILE

    def src(kind):
        def f(*ids):
            l, i, t = index(kind, *ids)
            return (l, i, t, 0) if kind == 2 else (l, i, 0, t)
        shape = (None, None, CAST_TILE, D) if kind == 2 else (None, None, D, CAST_TILE)
        return pl.BlockSpec(shape, f)

    def dst(kind):
        def f(*ids):
            l, i, t = index(kind, *ids)
            return (l, i, t, 0) if kind == 2 else (l, i, t // per_ff, 0, t % per_ff)
        shape = (None, None, CAST_TILE, D) if kind == 2 else (None, None, None, D, CAST_TILE)
        return pl.BlockSpec(shape, f)

    return [src(k) for k in range(3)], [dst(k) for k in range(3)]


def _cast_first(w_f32, layer, idx):
    wg, wu, wd = w_f32
    n_l, n_i, D, F = wg.shape
    f_pad = _round_up(F, FF_TILE)
    n_tiles = f_pad // CAST_TILE
    assert 0 < F - (n_tiles - 1) * CAST_TILE <= CAST_TILE
    srcs, dsts = _cast_specs(D, lambda kind, j: (layer, idx, j))
    return pl.pallas_call(
        functools.partial(_cast_first_body, d_ff=F),
        grid=(n_tiles,),
        in_specs=srcs,
        out_specs=dsts,
        out_shape=[jax.ShapeDtypeStruct((n_l, n_i, f_pad // FF_TILE, D, FF_TILE), BF16)] * 2 + [
            jax.ShapeDtypeStruct((n_l, n_i, f_pad, D), BF16)],
        compiler_params=_cparams(1),
        name="cast_first",
    )(wg, wu, wd)


def _cast_into_body(w_ref, *rest):
    rest[-1][...] = w_ref[...].astype(BF16)


def _cast_into(w, out, n_cols, col0):
    n, K, F = w.shape
    ct = max(LANE, min(F, CAST_BLOCK_ELEMS // K))
    assert F % ct == 0 and col0 % ct == 0
    in_specs = [pl.BlockSpec((None, K, ct), lambda l, j: (l, 0, j))]
    args = [w]
    if out is not None:
        in_specs.append(pl.BlockSpec(memory_space=pl.ANY))
        args.append(out)
    return pl.pallas_call(
        _cast_into_body,
        grid=(n, F // ct),
        in_specs=in_specs,
        out_specs=pl.BlockSpec((None, K, ct), lambda l, j: (l, 0, col0 // ct + j)),
        out_shape=jax.ShapeDtypeStruct((n, K, n_cols), BF16),
        input_output_aliases={1: 0} if out is not None else {},
        compiler_params=_cparams(2),
        name="cast_into",
    )(*args)


def _cast_side_by_side(ws):
    n_cols = sum(w.shape[2] for w in ws)
    out, col0 = None, 0
    for w in ws:
        out = _cast_into(w, out, n_cols, col0)
        col0 += w.shape[2]
    return out


class CastJob(NamedTuple):
    targets: tuple
    d_ff: int
    n_tiles: int

    @property
    def span(self):
        return len(self.targets) * self.n_tiles

    def index(self, kind, t):
        tt = jnp.clip(t - kind * self.span, 0, self.span - 1)
        which = tt // self.n_tiles
        layer, idx = self.targets[0]
        for n, (l, i) in enumerate(self.targets[1:], 1):
            layer = jnp.where(which == n, l, layer)
            idx = jnp.where(which == n, i, idx)
        return layer, idx, tt % self.n_tiles


def _cast_step(job, t, src_refs, dst_refs):
    for kind in range(3):
        active = (t >= kind * job.span) & (t < (kind + 1) * job.span)
        tile = (t - kind * job.span) % job.n_tiles
        _cast_slab(src_refs[kind], dst_refs[kind], 0 if kind == 2 else 1, active, tile, job.n_tiles, job.d_ff)


def _ffn_body(*refs, mode, n_chunks, n_tail):
    refs = list(refs)
    x_refs = [refs.pop(0) for _ in range(2 if mode == "first" else 1)]
    gain_ref, wg_ref, wu_ref, wd_ref, fgain_ref = refs[:5]
    if mode == "last":
        y4_ref, yt_ref, h_ref, acc_ref = refs[5:]
    else:
        acc_ref, h_ref = refs[5:]
    i = pl.program_id(0)
    j = pl.program_id(1)

    def init(x):
        h_ref[...] = (_rms(x) * gain_ref[...]).astype(BF16)
        acc_ref[...] = x

    if mode == "first":
        pl.when((j == 0) & (i < n_chunks))(lambda: init(x_refs[0][...].reshape(acc_ref.shape)))
        pl.when((j == 0) & (i == n_chunks))(lambda: init(x_refs[1][...]))
    elif mode == "last":
        @pl.when(j == 0)
        def _():
            rows = lax.broadcasted_iota(jnp.int32, acc_ref.shape, 0)
            init(jnp.where(rows < jnp.where(i == n_chunks, n_tail, acc_ref.shape[0]), x_refs[0][...], 0.0))
    else:
        pl.when(j == 0)(lambda: init(x_refs[0][...]))

    h = h_ref[...]
    g = jnp.dot(h, wg_ref[...], preferred_element_type=F32)
    u = jnp.dot(h, wu_ref[...], preferred_element_type=F32)
    a = (0.5 * _silu(g) * u).astype(BF16)
    acc_ref[...] += jnp.dot(a, wd_ref[...], preferred_element_type=F32)

    if mode == "last":
        last_j = j == pl.num_programs(1) - 1

        @pl.when(last_j & (i < n_chunks))
        def _():
            y4_ref[...] = (_rms(acc_ref[...]) * fgain_ref[...]).reshape(y4_ref.shape)

        @pl.when(last_j & (i == n_chunks))
        def _():
            yt_ref[...] = _rms(acc_ref[...]) * fgain_ref[...]


def _ffn(x, gain, w_bf, layer, idx, tm, mode="mid", fgain=None, n_rows=None, home=None):
    wg, wu, wd = w_bf
    nj = wg.shape[2]
    D = wg.shape[3]
    w_specs = [
        pl.BlockSpec((None, None, None, D, FF_TILE), lambda i, j: (layer, idx, j, 0, 0)),
        pl.BlockSpec((None, None, None, D, FF_TILE), lambda i, j: (layer, idx, j, 0, 0)),
        pl.BlockSpec((None, None, FF_TILE, D), lambda i, j: (layer, idx, j, 0)),
    ]
    vec_spec = pl.BlockSpec((1, D), lambda i, j: (0, 0))
    if fgain is None:
        fgain = gain
    n_chunks = n_tail = 0
    if mode == "mid":
        R = x.shape[0]
        grid = (R // tm, nj)
        x_args, x_specs = [x], [pl.BlockSpec((tm, D), lambda i, j: (i, 0))]
        out_specs = pl.BlockSpec((tm, D), lambda i, j: (i, 0))
        out_shape = jax.ShapeDtypeStruct((R, D), F32)
        scratch = [pltpu.VMEM((tm, D), BF16)]
    else:
        if mode == "first":
            home, R = x[0].shape, n_rows
        else:
            R = x.shape[0]
        B, n_chunks, C, _ = home
        tm = B * C
        n_tail = R - n_chunks * tm
        assert 0 < n_tail <= tm
        grid = (n_chunks + 1, nj)
        home_spec = pl.BlockSpec((B, None, C, D), lambda i, j: (0, jnp.minimum(i, n_chunks - 1), 0, 0))
        tail_spec = pl.BlockSpec((tm, D), lambda i, j: (0, 0))
        rows_spec = pl.BlockSpec((tm, D), lambda i, j: (i, 0))
        if mode == "first":
            x_args, x_specs = list(x), [home_spec, tail_spec]
            out_specs, out_shape = rows_spec, jax.ShapeDtypeStruct((R, D), F32)
            scratch = [pltpu.VMEM((tm, D), BF16)]
        else:
            x_args, x_specs = [x], [rows_spec]
            out_specs = [home_spec, tail_spec]
            out_shape = [jax.ShapeDtypeStruct(home, F32), jax.ShapeDtypeStruct((tm, D), F32)]
            scratch = [pltpu.VMEM((tm, D), BF16), pltpu.VMEM((tm, D), F32)]
    return pl.pallas_call(
        functools.partial(_ffn_body, mode=mode, n_chunks=n_chunks, n_tail=n_tail),
        grid=grid,
        in_specs=x_specs + [vec_spec] + w_specs + [vec_spec],
        out_specs=out_specs,
        out_shape=out_shape,
        scratch_shapes=scratch,
        compiler_params=_cparams(2),
        name="ffn_" + mode,
    )(*x_args, gain.reshape(1, D), wg, wu, wd, fgain.reshape(1, D))


def _proj_body(x_ref, gain_ref, w_ref, o_ref, h_ref, *, n_silu):
    j = pl.program_id(1)

    @pl.when(j == 0)
    def _():
        h_ref[...] = (_rms(x_ref[...]) * gain_ref[...]).astype(BF16)

    y = jnp.dot(h_ref[...], w_ref[...], preferred_element_type=F32)
    if n_silu > 0:
        y = jnp.where(j < n_silu, _silu(y), y)
    o_ref[...] = y.astype(BF16)


def _proj(x, gain, w, layer, n_silu_cols, tm):
    R, D = x.shape
    N = w.shape[2]
    return pl.pallas_call(
        functools.partial(_proj_body, n_silu=n_silu_cols // PROJ_TILE),
        grid=(R // tm, N // PROJ_TILE),
        in_specs=[
            pl.BlockSpec((tm, D), lambda i, j: (i, 0)),
            pl.BlockSpec((1, D), lambda i, j: (0, 0)),
            pl.BlockSpec((None, D, PROJ_TILE), lambda i, j: (layer, 0, j)),
        ],
        out_specs=pl.BlockSpec((tm, PROJ_TILE), lambda i, j: (i, j)),
        out_shape=jax.ShapeDtypeStruct((R, N), BF16),
        scratch_shapes=[pltpu.VMEM((tm, D), BF16)],
        compiler_params=_cparams(2),
        name="proj",
    )(x, gain.reshape(1, D), w)


def _lower_bound(logits, layer):
    e = jnp.exp(logits - jnp.max(logits, axis=0, keepdims=True))
    probs = e / jnp.sum(e, axis=0, keepdims=True)
    cs = probs[0:1]
    for i in range(1, layer + 1):
        cs = cs + probs[i:i + 1]
    return cs - probs[0:1]


def _gate_proj_body(x_ref, gain_ref, w_ref, logit_ref, g_ref, k_ref, h_ref, *, layer):
    j = pl.program_id(1)

    @pl.when(j == 0)
    def _():
        h_ref[...] = (_rms(x_ref[...]) * gain_ref[...]).astype(BF16)

    z = jnp.dot(h_ref[...], w_ref[...], preferred_element_type=F32)
    lb = _lower_bound(logit_ref[...], layer)
    t = jnp.exp(-jnp.abs(z))
    r = 1.0 / (1.0 + t)
    tr = t * r
    pos = z >= 0.0
    g_ref[...] = jnp.log(jnp.maximum(lb, LB_FLOOR) + (1.0 - lb) * jnp.where(pos, r, tr))
    k_ref[...] = ((1.0 - lb) * jnp.where(pos, tr, r)).astype(BF16)


def _gate_proj(x, gain, w, logits, layer, tm):
    R, D = x.shape
    N = w.shape[2]
    spec = pl.BlockSpec((tm, GATE_TILE), lambda i, j: (i, j))
    return pl.pallas_call(
        functools.partial(_gate_proj_body, layer=layer),
        grid=(R // tm, N // GATE_TILE),
        in_specs=[
            pl.BlockSpec((tm, D), lambda i, j: (i, 0)),
            pl.BlockSpec((1, D), lambda i, j: (0, 0)),
            pl.BlockSpec((None, D, GATE_TILE), lambda i, j: (layer, 0, j)),
            pl.BlockSpec((logits.shape[0], GATE_TILE), lambda i, j: (0, j)),
        ],
        out_specs=[spec, spec],
        out_shape=[jax.ShapeDtypeStruct((R, N), F32), jax.ShapeDtypeStruct((R, N), BF16)],
        scratch_shapes=[pltpu.VMEM((tm, D), BF16)],
        compiler_params=_cparams(2),
        name="gate_proj",
    )(x, gain.reshape(1, D), w, logits)


def _outproj_body(a_ref, w_ref, x_ref, o_ref):
    o_ref[...] = x_ref[...] + jnp.dot(a_ref[...], w_ref[...], preferred_element_type=F32)


def _outproj(a, w, layer, x, tm):
    R, K = a.shape
    D = w.shape[2]
    tn = D if 2 * K * D * w.dtype.itemsize <= OUT_WEIGHT_VMEM else OUT_TILE
    return pl.pallas_call(
        _outproj_body,
        grid=(R // tm, D // tn),
        in_specs=[
            pl.BlockSpec((tm, K), lambda i, j: (i, 0)),
            pl.BlockSpec((None, K, tn), lambda i, j: (layer, 0, j)),
            pl.BlockSpec((tm, tn), lambda i, j: (i, j)),
        ],
        out_specs=pl.BlockSpec((tm, tn), lambda i, j: (i, j)),
        out_shape=jax.ShapeDtypeStruct((R, D), F32),
        compiler_params=_cparams(2),
        name="outproj",
    )(a, w, x)


def _col_bcast(x):
    bs = x.shape[0]
    return jnp.concatenate([x] * (LANE // bs), axis=0).T


def _sample_rows(s):
    grp = s - 1
    n = SAMPLES_PER_STEP
    rows = pl.ds(pl.multiple_of(grp * n, n), n)
    rows2 = pl.ds(pl.multiple_of((grp // 2) * 2 * n, 2 * n), 2 * n)
    return rows, rows2, (grp % 2) == 1


def _half(x2, second):
    n = SAMPLES_PER_STEP
    return jnp.where(second, x2[n:], x2[:n])


def _mixer_call(body, name, grid, in_specs, args, out_specs, out_shape, scratch_shapes, state, new_state,
                w_f32, w_bf, job, D):
    n_s = grid[1]
    any_spec = pl.BlockSpec(memory_space=pl.ANY)
    in_specs, args, out_specs, out_shape = list(in_specs), list(args), list(out_specs), list(out_shape)
    i_state = len(out_shape) - 1
    if job is None:
        w_bf = ()
    else:
        srcs, dsts = _cast_specs(D, lambda kind, h, s: job.index(kind, h * n_s + s))
        in_specs += srcs
        args += list(w_f32)
        out_specs += dsts
        out_shape += [jax.ShapeDtypeStruct(w.shape, w.dtype) for w in w_bf]
    aliases = {}
    if new_state is not None:
        aliases[len(args)] = i_state
        in_specs.append(any_spec)
        args.append(new_state)
    for n, w in enumerate(w_bf):
        aliases[len(args)] = i_state + 1 + n
        in_specs.append(any_spec)
        args.append(w)
    return pl.pallas_call(
        functools.partial(body, n_alias=len(aliases)),
        grid=grid, in_specs=in_specs, out_specs=out_specs, out_shape=out_shape,
        scratch_shapes=scratch_shapes, input_output_aliases=aliases,
        compiler_params=_cparams(2), name=name,
    )(*args)


def _hgrn_tables(C):
    n_lvl = int(math.log2(C))
    assert 1 << n_lvl == C
    n_fine = min(n_lvl, int(math.log2(SUBLANE)) - 1)
    r = np.arange(C)
    msum = np.zeros((n_fine + 1, C, C), np.float32)
    pair = np.zeros((n_lvl + 1, C, C), np.float32)
    u = r[None, :]
    for l in range(n_lvl):
        half = 1 << l
        blk = 2 * half
        m = r - r % blk + half - 1
        up = (r % blk) >= half
        if l < n_fine:
            msum[l] = np.where(up[:, None], (u > m[:, None]) & (u <= r[:, None]),
                               (u > r[:, None]) & (u <= m[:, None]))
        pair[l] = ((r[:, None] // blk) == (r[None, :] // blk)) & up[:, None] & ~up[None, :]
    pair[n_lvl] = np.eye(C)
    msum[n_fine] = u <= r[:, None]
    return jnp.asarray(msum.reshape(-1, C), BF16), jnp.asarray(pair, F32)


def _split3(g):
    hi = g.astype(BF16)
    r1 = g - hi.astype(F32)
    mid = r1.astype(BF16)
    lo = (r1 - mid.astype(F32)).astype(BF16)
    return hi, mid, lo


def _hgrn_chunks(qs, gs, ks, vs, gates, hgains, sts, msum, pair_ref, valid, side=()):
    side = list(side)
    n_lvl = pair_ref.shape[0] - 1
    per_phase = -(-len(side) // (n_lvl + 1))

    def run_side():
        for _ in range(min(per_phase, len(side))):
            side.pop(0)()

    n = range(len(qs))
    C = qs[0].shape[0]
    n_fine = msum.shape[0] // C - 1
    qf = [q.astype(F32) for q in qs]
    kf = [k.astype(F32) for k in ks]
    if valid is not None:
        gs = [g * valid for g in gs]
        kf = [k * valid for k in kf]
        ks = [k.astype(BF16) for k in kf]
    pieces = [_split3(g) for g in gs]
    e2 = [jnp.dot(msum, jnp.concatenate(p[:2], axis=1), preferred_element_type=F32) for p in pieces]
    e = [t[:, :LANE] + t[:, LANE:] for t in e2]
    c = [e[b][n_fine * C:] + jnp.dot(msum[n_fine * C:], pieces[b][2], preferred_element_type=F32)
         for b in n]
    x_fine = [jnp.exp(t[:n_fine * C]) for t in e]
    run_side()

    scores = [lax.dot_general(qs[b], ks[b], _NT, preferred_element_type=F32) * pair_ref[n_lvl] for b in n]
    for l in range(n_lvl):
        if l < n_fine:
            xl = [t[l * C:(l + 1) * C] for t in x_fine]
        else:
            half = 1 << l
            xl = []
            for t in c:
                c3 = t.reshape(C // (2 * half), 2 * half, LANE)
                cm = c3[:, half - 1:half, :]
                xl.append(jnp.exp(-jnp.abs(c3 - cm)).reshape(C, LANE))
        scores = [scores[b] + lax.dot_general((qf[b] * xl[b]).astype(BF16), (kf[b] * xl[b]).astype(BF16), _NT,
                                              preferred_element_type=F32) * pair_ref[l] for b in n]
        run_side()
    while side:
        side.pop(0)()
    xc = [jnp.exp(t) for t in c]
    xt = [jnp.exp(t[C - 1:C, :] - t) for t in c]
    o = [lax.dot_general((qf[b] * xc[b]).astype(BF16), sts[b].astype(BF16), _NT, preferred_element_type=F32)
         for b in n]
    o = [o[b] + jnp.dot(scores[b].astype(BF16), vs[b], preferred_element_type=F32) for b in n]
    st_new = [sts[b] * xc[b][C - 1:C, :] + lax.dot_general(vs[b], (kf[b] * xt[b]).astype(BF16), _TN,
                                                           preferred_element_type=F32) for b in n]
    og = [(_rms(o[b]) * hgains[b] * gates[b].astype(F32)).astype(BF16) for b in n]
    return og, st_new


def _hgrn_mixer_body(*refs, n_batch, n_chunks, n_heads, job, n_alias):
    (q_ref, gate_ref, v_ref, g_ref, k_ref, hgain_ref, msum_ref, pair_ref,
     qs_ref, gates_ref, vs_ref, gs_ref, ks_ref, s_ref) = refs[:14]
    n_w = 0 if job is None else 3
    w_src = refs[14:14 + n_w]
    n_in = 14 + n_w + n_alias
    og_ref, so_ref, ogs_ref, sn_ref = refs[n_in:n_in + 4]
    w_dst = refs[n_in + 4:n_in + 4 + n_w]
    st_ref, o_scr = refs[n_in + 4 + n_w:]
    C = CHUNK
    s = pl.program_id(1)
    if job is not None:
        _cast_step(job, pl.program_id(0) * pl.num_programs(1) + s, w_src, w_dst)

    def step(seqs, sts, valid, side=()):
        tiles = [(r, pl.ds(hh * LANE, LANE)) for r, hh in seqs]
        return _hgrn_chunks([q_ref[t] for t in tiles], [g_ref[t] for t in tiles], [k_ref[t] for t in tiles],
                            [v_ref[t] for t in tiles], [gate_ref[t] for t in tiles],
                            [hgain_ref[:, t[1]] for t in tiles], sts, msum_ref[...], pair_ref, valid, side)

    @pl.when(s == 0)
    def _():
        valid = (lax.broadcasted_iota(jnp.int32, (C, 1), 0) < N_META).astype(F32)
        seqs = [(pl.ds(C, C), hh) for hh in range(n_heads)]
        og, st_new = step(seqs, [jnp.zeros((LANE, HA_DK), F32)] * n_heads, valid)
        og_ref[...] = jnp.zeros_like(og_ref)
        for hh in range(n_heads):
            og_ref[pl.ds(C, C), pl.ds(hh * LANE, LANE)] = og[hh]
            for b in range(n_batch):
                st_ref[hh * n_batch + b] = st_new[hh]

    @pl.when(s > 0)
    def _():
        rows, rows2, second = _sample_rows(s)
        side = []
        for hh in range(n_heads):
            lanes = pl.ds(hh * LANE, LANE)
            q2 = qs_ref[rows2, lanes]
            v1 = _half(vs_ref[rows2, lanes].astype(F32), second)
            ft = _col_bcast(jnp.exp(gs_ref[rows, lanes]))
            kt = _col_bcast(_half(ks_ref[rows2, lanes].astype(F32), second))

            def update(b, hh=hh, q2=q2, v1=v1, ft=ft, kt=kt):
                sn = ft[:, b:b + 1] * s_ref[b, hh] + kt[:, b:b + 1] * v1[b:b + 1, :]
                sn_ref[b, hh] = sn
                res = jnp.dot(q2, sn.astype(BF16), preferred_element_type=F32)
                o_scr[hh, b:b + 1, :] = _half(res, second)[b:b + 1, :]

            def finish(hh=hh, lanes=lanes):
                gate1 = _half(gates_ref[rows2, lanes].astype(F32), second)
                ogs_ref[rows, lanes] = _rms(o_scr[hh]) * hgain_ref[:, lanes] * gate1

            side += [functools.partial(update, b) for b in range(SAMPLES_PER_STEP)] + [finish]

        seqs = [(pl.ds(b * C, C), hh) for hh in range(n_heads) for b in range(n_batch)]
        og, st_new = step(seqs, [st_ref[i] for i in range(len(seqs))], None, side)
        for i, (r, hh) in enumerate(seqs):
            og_ref[r, pl.ds(hh * LANE, LANE)] = og[i]
            st_ref[i] = st_new[i]

    @pl.when(s == n_chunks)
    def _():
        for hh in range(n_heads):
            for b in range(n_batch):
                so_ref[b, hh] = st_ref[hh * n_batch + b].T


def _hgrn_mixer(pb, g, k, hgain, state, new_state, layer, n_batch, seq, tables, sample_blk, w_f32, w_bf, job):
    R = pb.shape[0]
    D = pb.shape[1] // 3
    H = D // HA_DK
    C = CHUNK
    B = n_batch
    hp = HGRN_HEADS_PER_STEP
    n_chunks = seq // C
    n_sample = state.shape[1]
    msum, pair = tables
    grid = (H // hp, 1 + n_chunks)
    assert B >= 2 and R >= n_chunks * B * C + 2 * C and H % hp == 0
    assert n_sample == SAMPLES_PER_STEP * n_chunks and (job is None or 3 * job.span <= grid[0] * grid[1])

    def spec(col0):
        return pl.BlockSpec((B * C, hp * LANE),
                            lambda h, s: (jnp.where(s == 0, n_chunks, s - 1), col0 // hp + h))

    def sspec(col0):
        return pl.BlockSpec((LANE, hp * LANE), lambda h, s: (sample_blk, col0 // hp + h))

    state_spec = pl.BlockSpec((None, SAMPLES_PER_STEP, hp, HA_DK, LANE),
                              lambda h, s: (layer, jnp.maximum(s - 1, 0), h, 0, 0))
    out = _mixer_call(
        functools.partial(_hgrn_mixer_body, n_batch=B, n_chunks=n_chunks, n_heads=hp, job=job),
        "hgrn_mixer", grid,
        in_specs=[spec(0), spec(H), spec(2 * H), spec(0), spec(0),
                  pl.BlockSpec((1, hp * LANE), lambda h, s: (0, h)),
                  pl.BlockSpec(msum.shape, lambda h, s: (0, 0)),
                  pl.BlockSpec(pair.shape, lambda h, s: (0, 0, 0)),
                  sspec(0), sspec(H), sspec(2 * H), sspec(0), sspec(0), state_spec],
        args=[pb, pb, pb, g, k, hgain.reshape(1, D), msum, pair, pb, pb, pb, g, k, state],
        out_specs=[spec(0), pl.BlockSpec((B, hp, HA_DK, LANE), lambda h, s: (0, h, 0, 0)),
                   pl.BlockSpec((LANE, hp * LANE), lambda h, s: (0, h)), state_spec],
        out_shape=[jax.ShapeDtypeStruct((R, D), BF16), jax.ShapeDtypeStruct((B, H, HA_DK, LANE), F32),
                   jax.ShapeDtypeStruct((LANE, D), F32), jax.ShapeDtypeStruct(state.shape, F32)],
        scratch_shapes=[pltpu.VMEM((hp * B, LANE, HA_DK), F32), pltpu.VMEM((hp, SAMPLES_PER_STEP, LANE), F32)],
        state=state, new_state=new_state, w_f32=w_f32, w_bf=w_bf, job=job, D=D)
    return out[0], out[1], out[2], out[3], (tuple(out[4:]) if job is not None else w_bf)


def _rotary(x, cos, sin):
    half = x.shape[-1] // 2
    x1, x2 = x[:, :half], x[:, half:]
    return jnp.concatenate([x1 * cos - x2 * sin, x1 * sin + x2 * cos], axis=-1)


def _ret_chunks(qs, ks, vs, gates, cos, sin, lg, sts, n_valid):
    n = range(len(qs))
    C, dk = qs[0].shape
    t_col = lax.broadcasted_iota(jnp.int32, (C, 1), 0)
    t_row = lax.broadcasted_iota(jnp.int32, (1, C), 1)
    n_col = jnp.minimum(t_col + 1, n_valid).astype(F32)
    n_row = jnp.minimum(t_row + 1, n_valid).astype(F32)
    n_last = float(min(C, n_valid))
    decay = jnp.where(t_col >= t_row, jnp.exp(lg * (n_col - n_row)), 0.0)
    q_scale = jnp.exp(lg * n_col)
    k_scale = jnp.exp(lg * (n_last - n_col))
    k_norm = dk ** -0.5
    if n_valid < C:
        k_norm = k_norm * (t_col < n_valid).astype(F32)
    q = [_rotary(t.astype(F32), cos, sin) for t in qs]
    k = [_rotary(t.astype(F32), cos, sin) * k_norm for t in ks]
    scores = [lax.dot_general(q[b].astype(BF16), k[b].astype(BF16), _NT, preferred_element_type=F32) * decay
              for b in n]
    o = [jnp.dot((q[b] * q_scale).astype(BF16), sts[b].astype(BF16), preferred_element_type=F32) for b in n]
    o = [o[b] + jnp.dot(scores[b].astype(BF16), vs[b], preferred_element_type=F32) for b in n]
    st_new = [jnp.exp(lg * n_last) * sts[b] + lax.dot_general((k[b] * k_scale).astype(BF16), vs[b], _TN,
                                                              preferred_element_type=F32) for b in n]
    og = [(_rms(o[b]) * gates[b].astype(F32)).astype(BF16) for b in n]
    return og, st_new


def _ret_mixer_body(*refs, n_batch, n_chunks, job, n_alias):
    (lg_ref, q_ref, k_ref, v_ref, gate_ref, cos_ref, sin_ref,
     qs_ref, ks_ref, vs_ref, gates_ref, coss_ref, sins_ref, s_ref) = refs[:14]
    n_w = 0 if job is None else 3
    w_src = refs[14:14 + n_w]
    n_in = 14 + n_w + n_alias
    og_ref, so_ref, ogs_ref, sn_ref = refs[n_in:n_in + 4]
    w_dst = refs[n_in + 4:n_in + 4 + n_w]
    st_ref, o_scr = refs[n_in + 4 + n_w:]
    C = CHUNK
    lg = lg_ref[pl.program_id(0)]
    s = pl.program_id(1)
    if job is not None:
        _cast_step(job, pl.program_id(0) * pl.num_programs(1) + s, w_src, w_dst)

    def step(rows, sts, n_valid):
        return _ret_chunks([q_ref[r, :] for r in rows], [k_ref[r, :] for r in rows], [v_ref[r, :] for r in rows],
                           [gate_ref[r, :] for r in rows], cos_ref[...], sin_ref[...], lg, sts, n_valid)

    @pl.when(s == 0)
    def _():
        og, st_new = step([pl.ds(C, C)], [jnp.zeros(st_ref.shape[1:], F32)], N_META)
        og_ref[...] = jnp.zeros_like(og_ref)
        og_ref[pl.ds(C, C), :] = og[0]
        for b in range(n_batch):
            st_ref[b] = st_new[0]

    @pl.when(s > 0)
    def _():
        rows, rows2, second = _sample_rows(s)
        dk = qs_ref.shape[-1]
        gamma = jnp.exp(lg)
        cos, sin = coss_ref[...], sins_ref[...]
        q = _rotary(_half(qs_ref[rows2, :].astype(F32), second), cos, sin)
        k = _rotary(_half(ks_ref[rows2, :].astype(F32), second), cos, sin) * (dk ** -0.5)
        v = _half(vs_ref[rows2, :].astype(F32), second)
        qt = jnp.concatenate([_col_bcast(q[:, :LANE]), _col_bcast(q[:, LANE:])], axis=0)
        kt = jnp.concatenate([_col_bcast(k[:, :LANE]), _col_bcast(k[:, LANE:])], axis=0)
        for b in range(SAMPLES_PER_STEP):
            sn = gamma * s_ref[b] + kt[:, b:b + 1] * v[b:b + 1, :]
            sn_ref[b] = sn
            o_scr[b:b + 1, :] = jnp.sum(qt[:, b:b + 1] * sn, axis=0, keepdims=True)
        ogs_ref[rows, :] = _rms(o_scr[...]) * _half(gates_ref[rows2, :].astype(F32), second)

        og, st_new = step([pl.ds(b * C, C) for b in range(n_batch)], [st_ref[b] for b in range(n_batch)], C)
        for b in range(n_batch):
            og_ref[pl.ds(b * C, C), :] = og[b]
            st_ref[b] = st_new[b]

    @pl.when(s == n_chunks)
    def _():
        so_ref[...] = st_ref[...]


def _ret_mixer(p, log_gamma, cos_tab, sin_tab, cos_s, sin_s, state, new_state, layer, n_batch, seq,
               sample_blk, w_f32, w_bf, job):
    R = p.shape[0]
    D = p.shape[1] // 6
    H = HB_HEADS
    dk = D // H
    dv = 2 * D // H
    C = CHUNK
    B = n_batch
    n_chunks = seq // C
    n_sample = state.shape[1]
    qk_off = 2 * D // dk
    v_off = 4 * D // dv
    grid = (H, 1 + n_chunks)
    assert B >= 2 and R >= n_chunks * B * C + 2 * C and dk == 2 * LANE
    assert n_sample == SAMPLES_PER_STEP * n_chunks and (job is None or 3 * job.span <= grid[0] * grid[1])

    def spec(width, col0):
        return pl.BlockSpec((B * C, width), lambda h, s: (jnp.where(s == 0, n_chunks, s - 1), col0 + h))

    def sspec(width, col0):
        return pl.BlockSpec((LANE, width), lambda h, s: (sample_blk, col0 + h))

    tab_spec = pl.BlockSpec((C, dk // 2), lambda h, s: (s, 0))
    pos_spec = pl.BlockSpec((1, dk // 2), lambda h, s: (0, 0))
    state_spec = pl.BlockSpec((None, SAMPLES_PER_STEP, None, dk, dv),
                              lambda h, s: (layer, jnp.maximum(s - 1, 0), h, 0, 0))
    out = _mixer_call(
        functools.partial(_ret_mixer_body, n_batch=B, n_chunks=n_chunks, job=job),
        "ret_mixer", grid,
        in_specs=[pl.BlockSpec(memory_space=pltpu.SMEM),
                  spec(dk, qk_off), spec(dk, qk_off + H), spec(dv, v_off), spec(dv, 0), tab_spec, tab_spec,
                  sspec(dk, qk_off), sspec(dk, qk_off + H), sspec(dv, v_off), sspec(dv, 0), pos_spec, pos_spec,
                  state_spec],
        args=[log_gamma, p, p, p, p, cos_tab, sin_tab, p, p, p, p, cos_s, sin_s, state],
        out_specs=[spec(dv, 0), pl.BlockSpec((B, None, dk, dv), lambda h, s: (0, h, 0, 0)),
                   pl.BlockSpec((LANE, dv), lambda h, s: (0, h)), state_spec],
        out_shape=[jax.ShapeDtypeStruct((R, 2 * D), BF16), jax.ShapeDtypeStruct((B, H, dk, dv), F32),
                   jax.ShapeDtypeStruct((LANE, 2 * D), F32), jax.ShapeDtypeStruct(state.shape, F32)],
        scratch_shapes=[pltpu.VMEM((B, dk, dv), F32), pltpu.VMEM((SAMPLES_PER_STEP, dv), F32)],
        state=state, new_state=new_state, w_f32=w_f32, w_bf=w_bf, job=job, D=D)
    return out[0], out[1], out[2], out[3], (tuple(out[4:]) if job is not None else w_bf)


def _rope_tables(pos, half):
    inv = ROPE_BASE ** (-jnp.linspace(0.0, 1.0, half, dtype=F32))
    ang = pos.astype(F32)[:, None] * inv[None, :]
    return jnp.cos(ang), jnp.sin(ang)


def kernel(x_prompt, x_sample, state_hgrn, state_ret, meta_tokens, norm_ffn, ffn_w_gate, ffn_w_up, ffn_w_down, norm_mix, hg_wq, hg_wf, hg_wi, hg_wg, hg_wo, hg_norm, hg_lb_logits, rt_wq, rt_wk, rt_wv, rt_wg, rt_wo, norm_final):
    B, L, D = x_prompt.shape
    NS = x_sample.shape[0]
    depth = norm_mix.shape[0]
    n_meta = meta_tokens.shape[0]
    C = CHUNK
    assert n_meta == N_META and x_sample.shape[1] == 1 and L % C == 0 and NS <= LANE
    n_chunks = L // C

    tp = B * L
    sp = LANE
    rows0 = tp + sp + C
    tm = next(t for t in ROW_TILES if _round_up(rows0, t) - rows0 < LANE) if rows0 >= ROW_TILES[0] else rows0
    R = _round_up(rows0, tm)
    home = (B, n_chunks, C, D)
    x_tail = jnp.concatenate([
        x_sample.reshape(NS, D), jnp.zeros((sp - NS, D), F32),
        meta_tokens.astype(F32), jnp.zeros((B * C - sp - n_meta, D), F32)], axis=0)
    sample_blk = tp // LANE

    w_f32 = (ffn_w_gate, ffn_w_up, ffn_w_down)
    d_ff = ffn_w_gate.shape[-1]
    n_cast_tiles = _round_up(d_ff, FF_TILE) // CAST_TILE
    w_bf = _cast_first(w_f32, 0, 0)

    hg_in = _cast_side_by_side([hg_wq, hg_wg, hg_wi])
    hg_f = _cast_side_by_side([hg_wf])
    hg_out = _cast_side_by_side([hg_wo])
    rt_in = _cast_side_by_side([rt_wg, rt_wq, rt_wk, rt_wv])
    rt_out = _cast_side_by_side([rt_wo])

    half = D // HB_HEADS // 2
    cos_p, sin_p = _rope_tables(jnp.arange(n_meta + L, dtype=jnp.int32), half)
    cos_s, sin_s = _rope_tables(PAST_LEN + jnp.arange(1, dtype=jnp.int32), half)

    def chunk_table(t):
        meta = jnp.concatenate([t[:n_meta], jnp.zeros((C - n_meta, half), F32)], axis=0)
        return jnp.concatenate([meta, t[n_meta:]], axis=0)

    cos_tab, sin_tab = chunk_table(cos_p), chunk_table(sin_p)
    log_gamma = jnp.log1p(-jnp.exp2(-5.0 - jnp.arange(HB_HEADS, dtype=F32)))
    tables = _hgrn_tables(C)

    new_hgrn_p, new_ret_p = [], []
    new_hgrn_s = new_ret_s = None
    for i in range(depth):
        if i == 0:
            x = _ffn((x_prompt.reshape(home), x_tail), norm_ffn[i, 0], w_bf, i, 0, tm, mode="first", n_rows=R)
        else:
            x = _ffn(x, norm_ffn[i, 0], w_bf, i, 0, tm)
        j = i // 2
        job = None
        if i % 2 == 0:
            later = [(l, n) for l in range(i, depth) for n in (0, 1)][1:]
            job = CastJob(tuple(later[:4] if i + 2 < depth else later), d_ff, n_cast_tiles)
        if i % 2 == 0:
            pb = _proj(x, norm_mix[i], hg_in, j, 2 * D, tm)
            g, k = _gate_proj(x, norm_mix[i], hg_f, hg_lb_logits, j, tm)
            og, s_p, og_s, new_hgrn_s, w_bf = _hgrn_mixer(
                pb, g, k, hg_norm[j], state_hgrn, new_hgrn_s, j, B, L, tables, sample_blk, w_f32, w_bf, job)
            new_hgrn_p.append(s_p)
            w_out = hg_out
        else:
            pb = _proj(x, norm_mix[i], rt_in, j, 2 * D, tm)
            og, s_p, og_s, new_ret_s, w_bf = _ret_mixer(
                pb, log_gamma, cos_tab, sin_tab, cos_s, sin_s, state_ret, new_ret_s, j, B, L,
                sample_blk, w_f32, w_bf, job)
            new_ret_p.append(s_p)
            w_out = rt_out
        og = lax.dynamic_update_slice(og, og_s[:NS].astype(BF16), (tp, 0))
        x = _outproj(og, w_out, j, x, tm)
        if i < depth - 1:
            x = _ffn(x, norm_ffn[i, 1], w_bf, i, 1, tm)
    y_home, y_tail = _ffn(x, norm_ffn[depth - 1, 1], w_bf, depth - 1, 1, tm, mode="last", fgain=norm_final,
                          home=home)
    y_prompt = y_home.reshape(B, L, D)
    y_sample = y_tail[:NS].reshape(NS, 1, D)
    return (y_prompt, y_sample, jnp.stack(new_hgrn_p), new_hgrn_s, jnp.stack(new_ret_p), new_ret_s)
```
